```python
import jax, jax.numpy as jnp
from jax import lax
import numpy as np

D_MODEL = 1024
BATCH = 4
SEQ = 4096
DEPTH = 1

CHUNK = 64
Q_BLOCK = 128
ML_HEADS = 4
ML_DQK = D_MODEL // 8
ML_DV = D_MODEL // 4
ML_QK = ML_HEADS * ML_DQK
ML_V = ML_HEADS * ML_DV
FOX_HEADS = 8
FOX_DH = D_MODEL // 8
FOX_W = FOX_HEADS * FOX_DH
D_FF = 2816
CONV_W = 3
GATE_CAP = 15.0
EPS = 1e-6
SPLITS = (ML_QK, ML_QK, ML_V, ML_HEADS, ML_HEADS, ML_V,
          FOX_W, FOX_W, FOX_W, FOX_HEADS, D_MODEL, D_MODEL)
D_IN = 2 * ML_QK + 2 * ML_V + 2 * ML_HEADS + 3 * FOX_W + FOX_HEADS + 2 * D_MODEL

kernel_name = "hybrid_mlstm_fox_convffn"


def rmsnorm(x, w):
    xf = x.astype(jnp.float32)
    y = xf * lax.rsqrt(jnp.mean(xf * xf, axis=-1, keepdims=True) + EPS)
    return (y * w.astype(jnp.float32)).astype(x.dtype)


def softcap(z):
    return GATE_CAP * jnp.tanh(z / GATE_CAP)


def causal_dwconv(u, w, b):
    S = u.shape[1]
    up = jnp.pad(u, ((0, 0), (CONV_W - 1, 0), (0, 0)))
    y = b
    for j in range(CONV_W):
        y = y + w[j] * up[:, j:j + S]
    return y


def mlstm_chunkwise(q, k, v, log_i, log_f):
    B, S, H, DK = q.shape
    DV = v.shape[-1]
    NC = S // CHUNK
    q = q.astype(jnp.float32) * (DK ** -0.5)
    k = k.astype(jnp.float32)
    v = v.astype(jnp.float32)

    def to_chunks(t):
        t = t.reshape((B, NC, CHUNK, H) + t.shape[3:])
        return jnp.moveaxis(t, (1, 3), (0, 2))

    causal = jnp.tril(jnp.ones((CHUNK, CHUNK), dtype=bool))

    def step(carry, xs):
        C, n, m = carry
        qc, kc, vc, li, lf = xs
        b = jnp.cumsum(lf, axis=-1)
        dmat = b[..., :, None] - b[..., None, :] + li[..., None, :]
        dmat = jnp.where(causal, dmat, -jnp.inf)
        inter = b + m[..., None]
        m_t = jnp.maximum(inter, jnp.max(dmat, axis=-1))
        w_intra = jnp.exp(dmat - m_t[..., None])
        w_inter = jnp.exp(inter - m_t)
        s = jnp.einsum('bhtd,bhsd->bhts', qc, kc) * w_intra
        num = (jnp.einsum('bhts,bhsv->bhtv', s, vc)
               + w_inter[..., None] * jnp.einsum('bhvd,bhtd->bhtv', C, qc))
        den = jnp.sum(s, axis=-1) + w_inter * jnp.einsum('bhd,bhtd->bht', n, qc)
        h = num / jnp.maximum(jnp.abs(den), jnp.exp(-m_t))[..., None]
        b_last = b[..., -1]
        g = b_last[..., None] - b + li
        m_new = jnp.maximum(b_last + m, jnp.max(g, axis=-1))
        a = jnp.exp(b_last + m - m_new)
        wk = jnp.exp(g - m_new[..., None])
        C_new = a[..., None, None] * C + jnp.einsum('bhs,bhsv,bhsd->bhvd', wk, vc, kc)
        n_new = a[..., None] * n + jnp.einsum('bhs,bhsd->bhd', wk, kc)
        return (C_new, n_new, m_new), h

    init = (jnp.zeros((B, H, DV, DK), jnp.float32),
            jnp.zeros((B, H, DK), jnp.float32),
            jnp.zeros((B, H), jnp.float32))
    xs = (to_chunks(q), to_chunks(k), to_chunks(v), to_chunks(log_i), to_chunks(log_f))
    _, hs = lax.scan(step, init, xs)
    return jnp.moveaxis(hs, (0, 2), (1, 3)).reshape(B, S, H, DV)


def forgetting_attention(q, k, v, log_f):
    B, S, H, D = q.shape
    nb = S // Q_BLOCK
    F = jnp.moveaxis(jnp.cumsum(log_f, axis=1), 1, 2)
    qb = q.reshape(B, nb, Q_BLOCK, H, D).transpose(1, 0, 3, 2, 4)
    Fb = F.reshape(B, H, nb, Q_BLOCK).transpose(2, 0, 1, 3)
    kpos = jnp.arange(S)
    scale = D ** -0.5

    def block(args):
        qi, Fi, i = args
        logits = (jnp.einsum('bhqd,bshd->bhqs', qi, k).astype(jnp.float32) * scale
                  + Fi[..., None] - F[:, :, None, :])
        qpos = i * Q_BLOCK + jnp.arange(Q_BLOCK)
        logits = jnp.where(kpos[None, :] <= qpos[:, None], logits, -jnp.inf)
        p = jax.nn.softmax(logits, axis=-1)
        return jnp.einsum('bhqs,bshd->bqhd', p.astype(v.dtype), v)

    out = lax.map(block, (qb, Fb, jnp.arange(nb)))
    return out.transpose(1, 0, 2, 3, 4).reshape(B, S, H, D)


def setup_inputs(seed: int = 0) -> dict:
    key = jax.random.key(seed)
    ks = jax.random.split(key, 19)
    f32 = jnp.float32

    def dense(k, fan_in, shape):
        return jax.random.normal(k, (DEPTH,) + shape, f32) * fan_in ** -0.5

    def around(k, n, center):
        return center + 0.05 * jax.random.normal(k, (DEPTH, n), f32)

    return {
        "x": jax.random.normal(ks[0], (BATCH, SEQ, D_MODEL), f32),
        "norm_mix_pre": around(ks[1], D_MODEL, 1.0),
        "w_in": dense(ks[2], D_MODEL, (D_MODEL, D_IN)),
        "b_ml_i": around(ks[3], ML_HEADS, 0.0),
        "b_ml_f": around(ks[4], ML_HEADS, jnp.linspace(3.0, 6.0, ML_HEADS)),
        "ml_head_norm": around(ks[5], ML_V, 1.0),
        "b_fox_f": around(ks[6], FOX_HEADS, 2.0),
        "b_gate_a": around(ks[7], D_MODEL, 0.0),
        "b_gate_b": around(ks[8], D_MODEL, 0.0),
        "w_branch_a": dense(ks[9], ML_V, (ML_V, D_MODEL)),
        "w_branch_b": dense(ks[10], FOX_W, (FOX_W, D_MODEL)),
        "w_out": dense(ks[11], D_MODEL, (D_MODEL, D_MODEL)),
        "norm_mix_post": around(ks[12], D_MODEL, 1.0),
        "norm_ffn_pre": around(ks[13], D_MODEL, 1.0),
        "w_up": dense(ks[14], D_MODEL, (D_MODEL, 2 * D_FF)),
        "conv_w": dense(ks[15], CONV_W, (CONV_W, 2 * D_FF)),
        "conv_b": around(ks[16], 2 * D_FF, 0.0),
        "w_down": dense(ks[17], D_FF, (D_FF, D_MODEL)),
        "norm_ffn_post": around(ks[18], D_MODEL, 1.0),
    }


def reference(x, norm_mix_pre, w_in, b_ml_i, b_ml_f, ml_head_norm, b_fox_f, b_gate_a, b_gate_b,
              w_branch_a, w_branch_b, w_out, norm_mix_post, norm_ffn_pre, w_up, conv_w, conv_b,
              w_down, norm_ffn_post):
    B, S, _ = x.shape
    f32 = jnp.float32
    offsets = np.cumsum(SPLITS)[:-1].tolist()
    for l in range(DEPTH):
        h = rmsnorm(x, norm_mix_pre[l])
        proj = h @ w_in[l]
        (q_m, k_m, v_m, i_m, f_m, o_m,
         q_f, k_f, v_f, f_f, g_a, g_b) = jnp.split(proj, offsets, axis=-1)

        log_i = softcap(i_m.astype(f32) + b_ml_i[l])
        log_f = jax.nn.log_sigmoid(softcap(f_m.astype(f32) + b_ml_f[l]))
        h_a = mlstm_chunkwise(q_m.reshape(B, S, ML_HEADS, ML_DQK),
                              k_m.reshape(B, S, ML_HEADS, ML_DQK),
                              v_m.reshape(B, S, ML_HEADS, ML_DV), log_i, log_f)
        h_a = rmsnorm(h_a, ml_head_norm[l].reshape(ML_HEADS, ML_DV))
        h_a = h_a.reshape(B, S, ML_V).astype(x.dtype) * jax.nn.sigmoid(o_m)
        y_a = h_a @ w_branch_a[l]

        log_fg = jax.nn.log_sigmoid(f_f.astype(f32) + b_fox_f[l])
        h_b = forgetting_attention(q_f.reshape(B, S, FOX_HEADS, FOX_DH),
                                   k_f.reshape(B, S, FOX_HEADS, FOX_DH),
                                   v_f.reshape(B, S, FOX_HEADS, FOX_DH), log_fg)
        y_b = h_b.reshape(B, S, FOX_W) @ w_branch_b[l]

        merged = jax.nn.sigmoid(g_a + b_gate_a[l]) * y_a + jax.nn.sigmoid(g_b + b_gate_b[l]) * y_b
        x = x + rmsnorm(merged @ w_out[l], norm_mix_post[l])

        h = rmsnorm(x, norm_ffn_pre[l])
        u = causal_dwconv(h @ w_up[l], conv_w[l], conv_b[l])
        a, g = jnp.split(u, 2, axis=-1)
        x = x + rmsnorm((jax.nn.gelu(g) * a) @ w_down[l], norm_ffn_post[l])
    return x
```

```python
import functools

import jax
import jax.numpy as jnp
from jax import lax
from jax.experimental import pallas as pl
from jax.experimental.pallas import tpu as pltpu

D_MODEL = 1024
ML_HEADS = 4
ML_DQK = 128
ML_DV = 256
ML_QK = ML_HEADS * ML_DQK
ML_V = ML_HEADS * ML_DV
FOX_HEADS = 8
FOX_DH = 128
FOX_W = FOX_HEADS * FOX_DH
D_FF = 2816
CONV_W = 3
GATE_CAP = 15.0
EPS = 1e-6

LANES = 128
SUBLANES = 8
GATE_W = LANES
N_GATES = 2 * ML_HEADS + FOX_HEADS
ML_CHUNK = 128
VMEM_LIMIT = 56 * 1024 * 1024

PROJ_W = 2 * ML_QK + 2 * ML_V + 3 * FOX_W + 2 * D_MODEL
COL_QM, COL_KM, COL_VM, COL_OM = 0, ML_QK, 2 * ML_QK, 2 * ML_QK + ML_V
COL_QF = COL_OM + ML_V
COL_KF = COL_QF + FOX_W
COL_VF = COL_KF + FOX_W
COL_GA = COL_VF + FOX_W
COL_GB = COL_GA + D_MODEL

f32 = jnp.float32
bf16 = jnp.bfloat16


def _log_sigmoid(z):
    return jnp.minimum(z, 0.0) - jnp.log1p(jnp.exp(-jnp.abs(z)))


def _rms(v):
    return v * lax.rsqrt(jnp.mean(v * v, axis=-1, keepdims=True) + EPS)


def _dot(a, b):
    return jnp.dot(a, b, preferred_element_type=f32)


def _dot_nt(a, b):
    return lax.dot_general(a, b, (((1,), (1,)), ((), ())), preferred_element_type=f32)


def _dot_tn(a, b):
    return lax.dot_general(a, b, (((0,), (0,)), ((), ())), preferred_element_type=f32)


def _inproj_kernel(x_ref, nw_ref, w_ref, wg_ref, bg_ref, tri_ref,
                   proj_ref, gcol_ref, grow_ref, h_sc, carry_sc, *, blocks_per_seq):
    i = pl.program_id(0)
    j = pl.program_id(1)
    tm = x_ref.shape[0]

    @pl.when(j == 0)
    def _():
        hb = (_rms(x_ref[...]) * nw_ref[...]).astype(bf16)
        h_sc[...] = hb
        g = _dot(hb, wg_ref[...]) + bg_ref[...]
        lane = lax.broadcasted_iota(jnp.int32, (tm, GATE_W), 1)
        cap = GATE_CAP * jnp.tanh(g / GATE_CAP)
        act = jnp.where(lane < ML_HEADS, cap,
                        jnp.where(lane < 2 * ML_HEADS, _log_sigmoid(cap), _log_sigmoid(g)))
        act = jnp.where(lane < N_GATES, act, 0.0)

        @pl.when(i % blocks_per_seq == 0)
        def _():
            carry_sc[...] = jnp.zeros_like(carry_sc)

        tri = tri_ref[...]
        sub_lane = lax.broadcasted_iota(jnp.int32, (ML_CHUNK, GATE_W), 1)
        for r in range(tm // ML_CHUNK):
            a = act[r * ML_CHUNK:(r + 1) * ML_CHUNK]
            a_hi = a.astype(bf16)
            r1 = a - a_hi.astype(f32)
            a_mid = r1.astype(bf16)
            a_lo = (r1 - a_mid.astype(f32)).astype(bf16)
            cs = _dot(tri, a_hi) + _dot(tri, a_mid) + _dot(tri, a_lo)
            glob = cs + carry_sc[...]
            carry_sc[...] = glob[ML_CHUNK - 1:ML_CHUNK, :]
            out = jnp.where(sub_lane < ML_HEADS, a, jnp.where(sub_lane < 2 * ML_HEADS, cs, glob))
            gcol_ref[r * ML_CHUNK:(r + 1) * ML_CHUNK, :] = out
            grow_ref[:, r * ML_CHUNK:(r + 1) * ML_CHUNK] = out.T[:N_GATES, :]

    proj_ref[...] = _dot(h_sc[...], w_ref[...]).astype(bf16)


def _inproj(x2, nw, w_big, w_gate, b_gate, tri, *, seq, tm=1024, tn=1024):
    t = x2.shape[0]
    kern = functools.partial(_inproj_kernel, blocks_per_seq=seq // tm)
    return pl.pallas_call(
        kern,
        grid=(t // tm, PROJ_W // tn),
        in_specs=[
            pl.BlockSpec((tm, D_MODEL), lambda i, j: (i, 0)),
            pl.BlockSpec((1, D_MODEL), lambda i, j: (0, 0)),
            pl.BlockSpec((D_MODEL, tn), lambda i, j: (0, j)),
            pl.BlockSpec((D_MODEL, GATE_W), lambda i, j: (0, 0)),
            pl.BlockSpec((1, GATE_W), lambda i, j: (0, 0)),
            pl.BlockSpec((ML_CHUNK, ML_CHUNK), lambda i, j: (0, 0)),
        ],
        out_specs=[
            pl.BlockSpec((tm, tn), lambda i, j: (i, j)),
            pl.BlockSpec((tm, GATE_W), lambda i, j: (i, 0)),
            pl.BlockSpec((N_GATES, tm), lambda i, j: (0, i)),
        ],
        out_shape=[
            jax.ShapeDtypeStruct((t, PROJ_W), bf16),
            jax.ShapeDtypeStruct((t, GATE_W), f32),
            jax.ShapeDtypeStruct((N_GATES, t), f32),
        ],
        scratch_shapes=[pltpu.VMEM((tm, D_MODEL), bf16), pltpu.VMEM((1, GATE_W), f32)],
        compiler_params=pltpu.CompilerParams(
            dimension_semantics=("arbitrary", "arbitrary"), vmem_limit_bytes=VMEM_LIMIT),
        name="inproj",
    )(x2, nw, w_big, w_gate, b_gate, tri)


def _mlstm_kernel(q_ref, k_ref, v_ref, o_ref, gcol_ref, grow_ref, hnw_ref, out_ref, s_sc, m_sc):
    c = pl.program_id(1)
    L = ML_CHUNK

    @pl.when(c == 0)
    def _():
        s_sc[...] = jnp.zeros_like(s_sc)
        m_sc[...] = jnp.zeros_like(m_sc)

    row = lax.broadcasted_iota(jnp.int32, (L, L), 0)
    col = lax.broadcasted_iota(jnp.int32, (L, L), 1)
    causal = col <= row
    gc = gcol_ref[...]
    gr = grow_ref[...]
    ones_col = (lax.broadcasted_iota(jnp.int32, (L, LANES), 1) == 0).astype(bf16)
    scale = ML_DQK ** -0.5

    for h in range(ML_HEADS):
        q = q_ref[:, h * ML_DQK:(h + 1) * ML_DQK]
        k = k_ref[:, h * ML_DQK:(h + 1) * ML_DQK]
        v = v_ref[:, h * ML_DV:(h + 1) * ML_DV]
        v_aug = jnp.concatenate([v, ones_col], axis=1)
        li_c = gc[:, h:h + 1]
        b_c = gc[:, ML_HEADS + h:ML_HEADS + h + 1]
        li_r = gr[h:h + 1, :]
        b_r = gr[ML_HEADS + h:ML_HEADS + h + 1, :]
        m_prev = m_sc[h:h + 1, 0:1]
        state = s_sc[h]

        dmat = jnp.where(causal, b_c - b_r + li_r, -jnp.inf)
        inter = b_c + m_prev
        m_t = jnp.maximum(inter, jnp.max(dmat, axis=-1, keepdims=True))
        w_intra = jnp.exp(dmat - m_t)
        w_inter = jnp.exp(inter - m_t)
        s = _dot_nt(q, k) * scale * w_intra
        num = _dot(s.astype(bf16), v_aug) + w_inter * (_dot(q, state.astype(bf16)) * scale)
        den = num[:, ML_DV:ML_DV + 1]
        hh = num[:, :ML_DV] * (1.0 / jnp.maximum(jnp.abs(den), jnp.exp(-m_t)))

        b_last = b_r[:, L - 1:L]
        g_r = b_last - b_r + li_r
        m_new = jnp.maximum(b_last + m_prev, jnp.max(g_r, axis=-1, keepdims=True))
        a = jnp.exp(b_last + m_prev - m_new)
        wk_c = jnp.exp(b_last - b_c + li_c - m_new)
        upd = _dot_tn(k, (wk_c * v_aug.astype(f32)).astype(bf16))
        s_sc[h] = a * state + upd
        m_sc[h:h + 1, :] = jnp.broadcast_to(m_new, (1, LANES))

        hn = _rms(hh) * hnw_ref[:, h * ML_DV:(h + 1) * ML_DV]
        gate = jax.nn.sigmoid(o_ref[:, h * ML_DV:(h + 1) * ML_DV].astype(f32))
        out_ref[:, h * ML_DV:(h + 1) * ML_DV] = (hn * gate).astype(bf16)


def _mlstm(proj, gcol, grow, hnw, *, batch, seq):
    t = proj.shape[0]
    nc = seq // ML_CHUNK
    L = ML_CHUNK
    return pl.pallas_call(
        _mlstm_kernel,
        grid=(batch, nc),
        in_specs=[
            pl.BlockSpec((L, ML_QK), lambda b, c: (b * nc + c, COL_QM // ML_QK)),
            pl.BlockSpec((L, ML_QK), lambda b, c: (b * nc + c, COL_KM // ML_QK)),
            pl.BlockSpec((L, ML_V), lambda b, c: (b * nc + c, COL_VM // ML_V)),
            pl.BlockSpec((L, ML_V), lambda b, c: (b * nc + c, COL_OM // ML_V)),
            pl.BlockSpec((L, GATE_W), lambda b, c: (b * nc + c, 0)),
            pl.BlockSpec((N_GATES, L), lambda b, c: (0, b * nc + c)),
            pl.BlockSpec((1, ML_V), lambda b, c: (0, 0)),
        ],
        out_specs=pl.BlockSpec((L, ML_V), lambda b, c: (b * nc + c, 0)),
        out_shape=jax.ShapeDtypeStruct((t, ML_V), bf16),
        scratch_shapes=[pltpu.VMEM((ML_HEADS, ML_DQK, ML_DV + LANES), f32),
                        pltpu.VMEM((SUBLANES, LANES), f32)],
        compiler_params=pltpu.CompilerParams(
            dimension_semantics=("arbitrary", "arbitrary"), vmem_limit_bytes=VMEM_LIMIT),
        name="mlstm",
    )(proj, proj, proj, proj, gcol, grow, hnw)


def _fox_kernel(q_ref, k_ref, v_ref, gcol_ref, grow_ref, out_ref, *, blk):
    h = pl.program_id(1)
    qi = pl.program_id(2)
    scale = FOX_DH ** -0.5
    q = q_ref[...]
    lane = lax.broadcasted_iota(jnp.int32, (blk, GATE_W), 1)
    f_q = jnp.sum(jnp.where(lane == 2 * ML_HEADS + h, gcol_ref[...], 0.0), axis=1, keepdims=True)

    def step(j, carry, masked):
        m, l, acc = carry
        start = pl.multiple_of(j * blk, blk)
        k = k_ref[pl.ds(start, blk), :]
        v = v_ref[pl.ds(start, blk), :]
        f_k = grow_ref[pl.ds(2 * ML_HEADS + h, 1), pl.ds(start, blk)]
        s = _dot_nt(q, k) * scale + f_q - f_k
        if masked:
            row = lax.broadcasted_iota(jnp.int32, (blk, blk), 0)
            col = lax.broadcasted_iota(jnp.int32, (blk, blk), 1)
            s = jnp.where(col <= row, s, -jnp.inf)
        m_new = jnp.maximum(m, jnp.max(s, axis=-1, keepdims=True))
        alpha = jnp.exp(m - m_new)
        p = jnp.exp(s - m_new)
        l = alpha * l + jnp.sum(p, axis=-1, keepdims=True)
        acc = alpha * acc + _dot(p.astype(bf16), v)
        return m_new, l, acc

    init = (jnp.full((blk, 1), -jnp.inf, f32), jnp.zeros((blk, 1), f32), jnp.zeros((blk, FOX_DH), f32))
    carry = lax.fori_loop(0, qi, functools.partial(step, masked=False), init)
    _, l, acc = step(qi, carry, masked=True)
    out_ref[...] = (acc * (1.0 / l)).astype(bf16)


def _fox(proj, gcol, grow, *, batch, seq, blk=512):
    t = proj.shape[0]
    nq = seq // blk
    kern = functools.partial(_fox_kernel, blk=blk)
    return pl.pallas_call(
        kern,
        grid=(batch, FOX_HEADS, nq),
        in_specs=[
            pl.BlockSpec((blk, FOX_DH), lambda b, h, i: (b * nq + i, COL_QF // FOX_DH + h)),
            pl.BlockSpec((seq, FOX_DH), lambda b, h, i: (b, COL_KF // FOX_DH + h)),
            pl.BlockSpec((seq, FOX_DH), lambda b, h, i: (b, COL_VF // FOX_DH + h)),
            pl.BlockSpec((blk, GATE_W), lambda b, h, i: (b * nq + i, 0)),
            pl.BlockSpec((N_GATES, seq), lambda b, h, i: (0, b)),
        ],
        out_specs=pl.BlockSpec((blk, FOX_DH), lambda b, h, i: (b * nq + i, h)),
        out_shape=jax.ShapeDtypeStruct((t, FOX_W), bf16),
        compiler_params=pltpu.CompilerParams(
            dimension_semantics=("arbitrary", "arbitrary", "arbitrary"), vmem_limit_bytes=VMEM_LIMIT),
        name="fox",
    )(proj, proj, proj, gcol, grow)


def _merge_kernel(x_ref, ha_ref, hb_ref, ga_ref, gb_ref, wa_ref, wb_ref, wo_ref,
                  bga_ref, bgb_ref, npost_ref, npre_ref, x1_ref, h2_ref):
    ya = _dot(ha_ref[...], wa_ref[...])
    yb = _dot(hb_ref[...], wb_ref[...])
    merged = (jax.nn.sigmoid(ga_ref[...].astype(f32) + bga_ref[...]) * ya
              + jax.nn.sigmoid(gb_ref[...].astype(f32) + bgb_ref[...]) * yb)
    z = _dot(merged.astype(bf16), wo_ref[...])
    x1 = x_ref[...] + _rms(z) * npost_ref[...]
    x1_ref[...] = x1
    h2_ref[...] = (_rms(x1) * npre_ref[...]).astype(bf16)


def _merge(x2, h_a, h_b, proj, wa, wb, wo, bga, bgb, npost, npre, *, tm=512):
    t = x2.shape[0]
    tok = lambda i: (i, 0)
    const = lambda i: (0, 0)
    wspec = pl.BlockSpec((D_MODEL, D_MODEL), const, pipeline_mode=pl.Buffered(1))
    vspec = pl.BlockSpec((1, D_MODEL), const)
    return pl.pallas_call(
        _merge_kernel,
        grid=(t // tm,),
        in_specs=[
            pl.BlockSpec((tm, D_MODEL), tok),
            pl.BlockSpec((tm, ML_V), tok),
            pl.BlockSpec((tm, FOX_W), tok),
            pl.BlockSpec((tm, D_MODEL), lambda i: (i, COL_GA // D_MODEL)),
            pl.BlockSpec((tm, D_MODEL), lambda i: (i, COL_GB // D_MODEL)),
            wspec, wspec, wspec, vspec, vspec, vspec, vspec,
        ],
        out_specs=[pl.BlockSpec((tm, D_MODEL), tok), pl.BlockSpec((tm, D_MODEL), tok)],
        out_shape=[jax.ShapeDtypeStruct((t, D_MODEL), f32), jax.ShapeDtypeStruct((t, D_MODEL), bf16)],
        compiler_params=pltpu.CompilerParams(
            dimension_semantics=("arbitrary",), vmem_limit_bytes=VMEM_LIMIT),
        name="merge",
    )(x2, h_a, h_b, proj, proj, wa, wb, wo, bga, bgb, npost, npre)


def _ffn_kernel(x1_ref, h2_ref, wup_ref, cw_ref, cb_ref, wdn_ref, npost_ref, out_ref,
                halo_sc, ubuf_sc, act_sc, *, blocks_per_seq, tf):
    i = pl.program_id(0)
    tm = x1_ref.shape[0]

    @pl.when(i % blocks_per_seq == 0)
    def _():
        halo_sc[...] = jnp.zeros_like(halo_sc)

    h2 = h2_ref[...]

    def conv_cols(slot, col0):
        u = _dot(h2, wup_ref[:, col0:col0 + tf])
        ubuf_sc[slot, 0:SUBLANES, :] = halo_sc[:, col0:col0 + tf]
        ubuf_sc[slot, SUBLANES:SUBLANES + tm, :] = u
        halo_sc[:, col0:col0 + tf] = u[tm - SUBLANES:tm, :]
        w = cw_ref[:, col0:col0 + tf]
        return (cb_ref[:, col0:col0 + tf]
                + w[0:1] * ubuf_sc[slot, SUBLANES - 2:SUBLANES - 2 + tm, :]
                + w[1:2] * ubuf_sc[slot, SUBLANES - 1:SUBLANES - 1 + tm, :]
                + w[2:3] * u)

    for c in range(D_FF // tf):
        a = conv_cols(0, c * tf)
        g = conv_cols(1, D_FF + c * tf)
        act_sc[:, c * tf:(c + 1) * tf] = (jax.nn.gelu(g) * a).astype(bf16)

    y = _dot(act_sc[...], wdn_ref[...])
    out_ref[...] = x1_ref[...] + _rms(y) * npost_ref[...]


def _ffn(x1, h2, wup, cw, cb, wdn, npost, *, seq, tm=512, tf=256):
    t = x1.shape[0]
    tok = lambda i: (i, 0)
    const = lambda i: (0, 0)
    kern = functools.partial(_ffn_kernel, blocks_per_seq=seq // tm, tf=tf)
    return pl.pallas_call(
        kern,
        grid=(t // tm,),
        in_specs=[
            pl.BlockSpec((tm, D_MODEL), tok),
            pl.BlockSpec((tm, D_MODEL), tok),
            pl.BlockSpec((D_MODEL, 2 * D_FF), const, pipeline_mode=pl.Buffered(1)),
            pl.BlockSpec((CONV_W, 2 * D_FF), const),
            pl.BlockSpec((1, 2 * D_FF), const),
            pl.BlockSpec((D_FF, D_MODEL), const, pipeline_mode=pl.Buffered(1)),
            pl.BlockSpec((1, D_MODEL), const),
        ],
        out_specs=pl.BlockSpec((tm, D_MODEL), tok),
        out_shape=jax.ShapeDtypeStruct((t, D_MODEL), f32),
        scratch_shapes=[pltpu.VMEM((SUBLANES, 2 * D_FF), f32),
                        pltpu.VMEM((2, tm + SUBLANES, tf), f32),
                        pltpu.VMEM((tm, D_FF), bf16)],
        compiler_params=pltpu.CompilerParams(
            dimension_semantics=("arbitrary",), vmem_limit_bytes=VMEM_LIMIT),
        name="ffn",
    )(x1, h2, wup, cw, cb, wdn, npost)


def _layer(x2, p, l, *, batch, seq):
    w_in = p["w_in"][l]
    o_i = 2 * ML_QK + ML_V
    o_o = o_i + 2 * ML_HEADS
    o_qf = o_o + ML_V
    o_ff = o_qf + 3 * FOX_W
    o_ga = o_ff + FOX_HEADS
    w_big = jnp.concatenate([
        w_in[:, :o_i],
        w_in[:, o_o:o_qf],
        w_in[:, o_qf:o_ff],
        w_in[:, o_ga:],
    ], axis=1).astype(bf16)
    w_gate = jnp.concatenate([
        w_in[:, o_i:o_o], w_in[:, o_ff:o_ga],
        jnp.zeros((D_MODEL, GATE_W - N_GATES), f32)], axis=1).astype(bf16)
    b_gate = jnp.concatenate([
        p["b_ml_i"][l], p["b_ml_f"][l], p["b_fox_f"][l], jnp.zeros((GATE_W - N_GATES,), f32)])[None, :]
    tri = jnp.tril(jnp.ones((ML_CHUNK, ML_CHUNK), f32)).astype(bf16)
    row = lambda v: v[None, :].astype(f32)

    proj, gcol, grow = _inproj(x2, row(p["norm_mix_pre"][l]), w_big, w_gate, b_gate, tri, seq=seq)
    h_a = _mlstm(proj, gcol, grow, row(p["ml_head_norm"][l]), batch=batch, seq=seq)
    h_b = _fox(proj, gcol, grow, batch=batch, seq=seq)
    x1, h2 = _merge(x2, h_a, h_b, proj,
                    p["w_branch_a"][l].astype(bf16), p["w_branch_b"][l].astype(bf16),
                    p["w_out"][l].astype(bf16), row(p["b_gate_a"][l]), row(p["b_gate_b"][l]),
                    row(p["norm_mix_post"][l]), row(p["norm_ffn_pre"][l]))
    return _ffn(x1, h2, p["w_up"][l].astype(bf16), p["conv_w"][l], row(p["conv_b"][l]),
                p["w_down"][l].astype(bf16), row(p["norm_ffn_post"][l]), seq=seq)


def kernel(x, norm_mix_pre, w_in, b_ml_i, b_ml_f, ml_head_norm, b_fox_f, b_gate_a, b_gate_b,
           w_branch_a, w_branch_b, w_out, norm_mix_post, norm_ffn_pre, w_up, conv_w, conv_b,
           w_down, norm_ffn_post):
    batch, seq, _ = x.shape
    p = dict(norm_mix_pre=norm_mix_pre, w_in=w_in, b_ml_i=b_ml_i, b_ml_f=b_ml_f,
             ml_head_norm=ml_head_norm, b_fox_f=b_fox_f, b_gate_a=b_gate_a, b_gate_b=b_gate_b,
             w_branch_a=w_branch_a, w_branch_b=w_branch_b, w_out=w_out,
             norm_mix_post=norm_mix_post, norm_ffn_pre=norm_ffn_pre, w_up=w_up, conv_w=conv_w,
             conv_b=conv_b, w_down=w_down, norm_ffn_post=norm_ffn_post)
    x2 = x.reshape(batch * seq, D_MODEL)
    for l in range(w_in.shape[0]):
        x2 = _layer(x2, p, l, batch=batch, seq=seq)
    return x2.reshape(batch, seq, D_MODEL)
```

```python
import functools

import jax
import jax.numpy as jnp
from jax import lax
from jax.experimental import pallas as pl
from jax.experimental.pallas import tpu as pltpu

D_MODEL = 1024
ML_HEADS = 4
ML_DQK = 128
ML_DV = 256
ML_QK = ML_HEADS * ML_DQK
ML_V = ML_HEADS * ML_DV
FOX_HEADS = 8
FOX_DH = 128
FOX_W = FOX_HEADS * FOX_DH
D_FF = 2816
CONV_W = 3
GATE_CAP = 15.0
EPS = 1e-6
LOG2E = 1.4426950408889634

LANES = 128
SUBLANES = 8
GATE_W = LANES
N_GATES = 2 * ML_HEADS + FOX_HEADS
ML_CHUNK = 128
VMEM_LIMIT = 56 * 1024 * 1024

PROJ_W = 2 * ML_QK + 2 * ML_V + 3 * FOX_W + 2 * D_MODEL
COL_QM, COL_KM, COL_VM, COL_OM = 0, ML_QK, 2 * ML_QK, 2 * ML_QK + ML_V
COL_QF = COL_OM + ML_V
COL_KF = COL_QF + FOX_W
COL_VF = COL_KF + FOX_W
COL_GA = COL_VF + FOX_W
COL_GB = COL_GA + D_MODEL

f32 = jnp.float32
bf16 = jnp.bfloat16


def _log_sigmoid(z):
    return jnp.minimum(z, 0.0) - jnp.log1p(jnp.exp(-jnp.abs(z)))


def _rms(v):
    return v * lax.rsqrt(jnp.mean(v * v, axis=-1, keepdims=True) + EPS)


def _dot(a, b):
    return jnp.dot(a, b, preferred_element_type=f32)


def _dot_nt(a, b):
    return lax.dot_general(a, b, (((1,), (1,)), ((), ())), preferred_element_type=f32)


def _dot_tn(a, b):
    return lax.dot_general(a, b, (((0,), (0,)), ((), ())), preferred_element_type=f32)


def _inproj_kernel(x_ref, nw_ref, w_ref, wg_ref, bg_ref, tri_ref,
                   proj_ref, gcol_ref, grow_ref, h_sc, carry_sc, *, blocks_per_seq):
    i = pl.program_id(0)
    j = pl.program_id(1)
    tm = x_ref.shape[0]

    @pl.when(j == 0)
    def _():
        hb = (_rms(x_ref[...]) * nw_ref[...]).astype(bf16)
        h_sc[...] = hb
        g = _dot(hb, wg_ref[...]) + bg_ref[...]
        lane = lax.broadcasted_iota(jnp.int32, (tm, GATE_W), 1)
        cap = GATE_CAP * jnp.tanh(g / GATE_CAP)
        act = jnp.where(lane < ML_HEADS, cap,
                        jnp.where(lane < 2 * ML_HEADS, _log_sigmoid(cap), _log_sigmoid(g)))
        act = jnp.where(lane < N_GATES, act, 0.0)

        @pl.when(i % blocks_per_seq == 0)
        def _():
            carry_sc[...] = jnp.zeros_like(carry_sc)

        tri = tri_ref[...]
        sub_lane = lax.broadcasted_iota(jnp.int32, (ML_CHUNK, GATE_W), 1)
        for r in range(tm // ML_CHUNK):
            a = act[r * ML_CHUNK:(r + 1) * ML_CHUNK]
            a_hi = a.astype(bf16)
            r1 = a - a_hi.astype(f32)
            a_mid = r1.astype(bf16)
            a_lo = (r1 - a_mid.astype(f32)).astype(bf16)
            cs = _dot(tri, a_hi) + _dot(tri, a_mid) + _dot(tri, a_lo)
            glob = cs + carry_sc[...]
            carry_sc[...] = glob[ML_CHUNK - 1:ML_CHUNK, :]
            out = jnp.where(sub_lane < ML_HEADS, a, jnp.where(sub_lane < 2 * ML_HEADS, cs, glob))
            gcol_ref[r * ML_CHUNK:(r + 1) * ML_CHUNK, :] = out
            grow_ref[:, r * ML_CHUNK:(r + 1) * ML_CHUNK] = out.T[:N_GATES, :]

    factor = jnp.where(j == COL_QF // w_ref.shape[1], LOG2E * FOX_DH ** -0.5, 1.0)
    proj_ref[...] = (_dot(h_sc[...], w_ref[...]) * factor).astype(bf16)


def _inproj(x2, nw, w_big, w_gate, b_gate, tri, *, seq, tm=1024, tn=1024):
    assert tn == FOX_W and COL_QF % tn == 0
    t = x2.shape[0]
    kern = functools.partial(_inproj_kernel, blocks_per_seq=seq // tm)
    return pl.pallas_call(
        kern,
        grid=(t // tm, PROJ_W // tn),
        in_specs=[
            pl.BlockSpec((tm, D_MODEL), lambda i, j: (i, 0)),
            pl.BlockSpec((1, D_MODEL), lambda i, j: (0, 0)),
            pl.BlockSpec((D_MODEL, tn), lambda i, j: (0, j)),
            pl.BlockSpec((D_MODEL, GATE_W), lambda i, j: (0, 0)),
            pl.BlockSpec((1, GATE_W), lambda i, j: (0, 0)),
            pl.BlockSpec((ML_CHUNK, ML_CHUNK), lambda i, j: (0, 0)),
        ],
        out_specs=[
            pl.BlockSpec((tm, tn), lambda i, j: (i, j)),
            pl.BlockSpec((tm, GATE_W), lambda i, j: (i, 0)),
            pl.BlockSpec((N_GATES, tm), lambda i, j: (0, i)),
        ],
        out_shape=[
            jax.ShapeDtypeStruct((t, PROJ_W), bf16),
            jax.ShapeDtypeStruct((t, GATE_W), f32),
            jax.ShapeDtypeStruct((N_GATES, t), f32),
        ],
        scratch_shapes=[pltpu.VMEM((tm, D_MODEL), bf16), pltpu.VMEM((1, GATE_W), f32)],
        compiler_params=pltpu.CompilerParams(
            dimension_semantics=("arbitrary", "arbitrary"), vmem_limit_bytes=VMEM_LIMIT),
        name="inproj",
    )(x2, nw, w_big, w_gate, b_gate, tri)


def _mlstm_kernel(q_ref, k_ref, v_ref, o_ref, gcol_ref, grow_ref, hnw_ref, out_ref, s_sc, m_sc):
    c = pl.program_id(1)
    L = ML_CHUNK

    @pl.when(c == 0)
    def _():
        s_sc[...] = jnp.zeros_like(s_sc)
        m_sc[...] = jnp.zeros_like(m_sc)

    row = lax.broadcasted_iota(jnp.int32, (L, L), 0)
    col = lax.broadcasted_iota(jnp.int32, (L, L), 1)
    causal = col <= row
    gc = gcol_ref[...]
    gr = grow_ref[...]
    ones_col = (lax.broadcasted_iota(jnp.int32, (L, LANES), 1) == 0).astype(bf16)
    scale = ML_DQK ** -0.5

    for h in range(ML_HEADS):
        q = q_ref[:, h * ML_DQK:(h + 1) * ML_DQK]
        k = k_ref[:, h * ML_DQK:(h + 1) * ML_DQK]
        v = v_ref[:, h * ML_DV:(h + 1) * ML_DV]
        v_aug = jnp.concatenate([v, ones_col], axis=1)
        li_c = gc[:, h:h + 1]
        b_c = gc[:, ML_HEADS + h:ML_HEADS + h + 1]
        li_r = gr[h:h + 1, :]
        b_r = gr[ML_HEADS + h:ML_HEADS + h + 1, :]
        m_prev = m_sc[h:h + 1, 0:1]
        state = s_sc[h]

        dmat = jnp.where(causal, b_c - b_r + li_r, -jnp.inf)
        inter = b_c + m_prev
        m_t = jnp.maximum(inter, jnp.max(dmat, axis=-1, keepdims=True))
        w_intra = jnp.exp(dmat - m_t)
        w_inter = jnp.exp(inter - m_t)
        s = _dot_nt(q, k) * scale * w_intra
        num = _dot(s.astype(bf16), v_aug) + w_inter * (_dot(q, state.astype(bf16)) * scale)
        den = num[:, ML_DV:ML_DV + 1]
        hh = num[:, :ML_DV] * (1.0 / jnp.maximum(jnp.abs(den), jnp.exp(-m_t)))

        b_last = b_r[:, L - 1:L]
        g_r = b_last - b_r + li_r
        m_new = jnp.maximum(b_last + m_prev, jnp.max(g_r, axis=-1, keepdims=True))
        a = jnp.exp(b_last + m_prev - m_new)
        wk_c = jnp.exp(b_last - b_c + li_c - m_new)
        upd = _dot_tn(k, (wk_c * v_aug.astype(f32)).astype(bf16))
        s_sc[h] = a * state + upd
        m_sc[h:h + 1, :] = jnp.broadcast_to(m_new, (1, LANES))

        hn = _rms(hh) * hnw_ref[:, h * ML_DV:(h + 1) * ML_DV]
        gate = jax.nn.sigmoid(o_ref[:, h * ML_DV:(h + 1) * ML_DV].astype(f32))
        out_ref[:, h * ML_DV:(h + 1) * ML_DV] = (hn * gate).astype(bf16)


def _mlstm(proj, gcol, grow, hnw, *, batch, seq):
    t = proj.shape[0]
    nc = seq // ML_CHUNK
    L = ML_CHUNK
    return pl.pallas_call(
        _mlstm_kernel,
        grid=(batch, nc),
        in_specs=[
            pl.BlockSpec((L, ML_QK), lambda b, c: (b * nc + c, COL_QM // ML_QK)),
            pl.BlockSpec((L, ML_QK), lambda b, c: (b * nc + c, COL_KM // ML_QK)),
            pl.BlockSpec((L, ML_V), lambda b, c: (b * nc + c, COL_VM // ML_V)),
            pl.BlockSpec((L, ML_V), lambda b, c: (b * nc + c, COL_OM // ML_V)),
            pl.BlockSpec((L, GATE_W), lambda b, c: (b * nc + c, 0)),
            pl.BlockSpec((N_GATES, L), lambda b, c: (0, b * nc + c)),
            pl.BlockSpec((1, ML_V), lambda b, c: (0, 0)),
        ],
        out_specs=pl.BlockSpec((L, ML_V), lambda b, c: (b * nc + c, 0)),
        out_shape=jax.ShapeDtypeStruct((t, ML_V), bf16),
        scratch_shapes=[pltpu.VMEM((ML_HEADS, ML_DQK, ML_DV + LANES), f32),
                        pltpu.VMEM((SUBLANES, LANES), f32)],
        compiler_params=pltpu.CompilerParams(
            dimension_semantics=("arbitrary", "arbitrary"), vmem_limit_bytes=VMEM_LIMIT),
        name="mlstm",
    )(proj, proj, proj, proj, gcol, grow, hnw)


def _fox_kernel(q_ref, k_ref, v_ref, gcol_ref, out_ref, kaug_sc, st_sc, mx_sc, m_sc, l_sc, acc_sc,
                *, blk, setup_rows):
    hg = pl.program_id(1)
    qi = pl.program_id(2)
    seq = k_ref.shape[0]
    nh = kaug_sc.shape[0]
    head_cols = [slice(i * FOX_DH, (i + 1) * FOX_DH) for i in range(nh)]

    @pl.when(qi == 0)
    def _():
        r = lax.broadcasted_iota(jnp.int32, (GATE_W, LANES), 0)
        c = lax.broadcasted_iota(jnp.int32, (GATE_W, LANES), 1)
        for n in range(seq // setup_rows):
            rows = pl.ds(n * setup_rows, setup_rows)
            a = gcol_ref[rows, :] * (-LOG2E)
            a_hi = a.astype(bf16)
            r1 = a - a_hi.astype(f32)
            a_mid = r1.astype(bf16)
            a_lo = (r1 - a_mid.astype(f32)).astype(bf16)
            for i in range(nh):
                src = 2 * ML_HEADS + hg * nh + i
                pick = [((r == src) & (c == j)).astype(bf16) for j in range(3)]
                aug = _dot(a_hi, pick[0]) + _dot(a_mid, pick[1]) + _dot(a_lo, pick[2])
                kaug_sc[i, rows, :FOX_DH] = k_ref[rows, head_cols[i]]
                kaug_sc[i, rows, FOX_DH:] = aug.astype(bf16)

    ones3 = (lax.broadcasted_iota(jnp.int32, (blk, LANES), 1) < 3).astype(bf16)
    q_aug = [jnp.concatenate([q_ref[:, head_cols[i]], ones3], axis=1) for i in range(nh)]

    m_sc[...] = jnp.full(m_sc.shape, -jnp.inf, f32)
    l_sc[...] = jnp.zeros(l_sc.shape, f32)
    acc_sc[...] = jnp.zeros(acc_sc.shape, f32)

    def logits_stage(i, j, masked):
        start = pl.multiple_of(j * blk, blk)
        st = _dot_nt(kaug_sc[i, pl.ds(start, blk), :], q_aug[i])
        if masked:
            row = lax.broadcasted_iota(jnp.int32, (blk, blk), 0)
            col = lax.broadcasted_iota(jnp.int32, (blk, blk), 1)
            st = jnp.where(row <= col, st, -jnp.inf)
        st_sc[i] = st
        mx_sc[i] = jnp.max(st, axis=0, keepdims=True)

    def softmax_stage(i, j):
        start = pl.multiple_of(j * blk, blk)
        m = m_sc[i]
        m_new = jnp.maximum(m, mx_sc[i])
        alpha = jnp.exp2(m - m_new)
        p = jnp.exp2(st_sc[i] - m_new)
        l_sc[i] = alpha * l_sc[i] + jnp.sum(p, axis=0, keepdims=True)
        acc_sc[i] = alpha * acc_sc[i] + _dot_tn(v_ref[pl.ds(start, blk), head_cols[i]], p.astype(bf16))
        m_sc[i] = m_new

    def round_robin(cur, nxt, masked):
        for i in range(nh):
            if i + 1 < nh:
                logits_stage(i + 1, cur, masked)
            else:
                logits_stage(0, nxt, False)
            softmax_stage(i, cur)

    logits_stage(0, qi, True)
    round_robin(qi, 0, True)

    def body(k, _):
        round_robin(k - 1, jnp.minimum(k, qi - 1), False)
        return 0

    lax.fori_loop(1, qi + 1, body, 0)
    for i in range(nh):
        out_ref[:, head_cols[i]] = (acc_sc[i] * (1.0 / l_sc[i])).T.astype(bf16)


def _fox(proj, gcol, *, batch, seq, blk=512, nh=2):
    assert nh >= 2
    t = proj.shape[0]
    nq = seq // blk
    w = nh * FOX_DH
    kern = functools.partial(_fox_kernel, blk=blk, setup_rows=min(seq, 512))
    return pl.pallas_call(
        kern,
        grid=(batch, FOX_HEADS // nh, nq),
        in_specs=[
            pl.BlockSpec((blk, w), lambda b, h, i: (b * nq + i, COL_QF // w + h)),
            pl.BlockSpec((seq, w), lambda b, h, i: (b, COL_KF // w + h)),
            pl.BlockSpec((seq, w), lambda b, h, i: (b, COL_VF // w + h)),
            pl.BlockSpec((seq, GATE_W), lambda b, h, i: (b, 0)),
        ],
        out_specs=pl.BlockSpec((blk, w), lambda b, h, i: (b * nq + i, h)),
        out_shape=jax.ShapeDtypeStruct((t, FOX_W), bf16),
        scratch_shapes=[pltpu.VMEM((nh, seq, FOX_DH + LANES), bf16),
                        pltpu.VMEM((nh, blk, blk), f32),
                        pltpu.VMEM((nh, 1, blk), f32),
                        pltpu.VMEM((nh, 1, blk), f32),
                        pltpu.VMEM((nh, 1, blk), f32),
                        pltpu.VMEM((nh, FOX_DH, blk), f32)],
        compiler_params=pltpu.CompilerParams(
            dimension_semantics=("arbitrary", "arbitrary", "arbitrary"), vmem_limit_bytes=VMEM_LIMIT),
        name="fox",
    )(proj, proj, proj, gcol)


def _merge_kernel(x_ref, ha_ref, hb_ref, ga_ref, gb_ref, wa_ref, wb_ref, wo_ref,
                  bga_ref, bgb_ref, npost_ref, npre_ref, x1_ref, h2_ref):
    ya = _dot(ha_ref[...], wa_ref[...])
    yb = _dot(hb_ref[...], wb_ref[...])
    merged = (jax.nn.sigmoid(ga_ref[...].astype(f32) + bga_ref[...]) * ya
              + jax.nn.sigmoid(gb_ref[...].astype(f32) + bgb_ref[...]) * yb)
    z = _dot(merged.astype(bf16), wo_ref[...])
    x1 = x_ref[...] + _rms(z) * npost_ref[...]
    x1_ref[...] = x1
    h2_ref[...] = (_rms(x1) * npre_ref[...]).astype(bf16)


def _merge(x2, h_a, h_b, proj, wa, wb, wo, bga, bgb, npost, npre, *, tm=512):
    t = x2.shape[0]
    tok = lambda i: (i, 0)
    const = lambda i: (0, 0)
    wspec = pl.BlockSpec((D_MODEL, D_MODEL), const, pipeline_mode=pl.Buffered(1))
    vspec = pl.BlockSpec((1, D_MODEL), const)
    return pl.pallas_call(
        _merge_kernel,
        grid=(t // tm,),
        in_specs=[
            pl.BlockSpec((tm, D_MODEL), tok),
            pl.BlockSpec((tm, ML_V), tok),
            pl.BlockSpec((tm, FOX_W), tok),
            pl.BlockSpec((tm, D_MODEL), lambda i: (i, COL_GA // D_MODEL)),
            pl.BlockSpec((tm, D_MODEL), lambda i: (i, COL_GB // D_MODEL)),
            wspec, wspec, wspec, vspec, vspec, vspec, vspec,
        ],
        out_specs=[pl.BlockSpec((tm, D_MODEL), tok), pl.BlockSpec((tm, D_MODEL), tok)],
        out_shape=[jax.ShapeDtypeStruct((t, D_MODEL), f32), jax.ShapeDtypeStruct((t, D_MODEL), bf16)],
        compiler_params=pltpu.CompilerParams(
            dimension_semantics=("arbitrary",), vmem_limit_bytes=VMEM_LIMIT),
        name="merge",
    )(x2, h_a, h_b, proj, proj, wa, wb, wo, bga, bgb, npost, npre)


def _ffn_kernel(x1_ref, h2_ref, wup_ref, cw_ref, cb_ref, wdn_ref, npost_ref, out_ref,
                halo_sc, ubuf_sc, act_sc, *, blocks_per_seq, tf):
    i = pl.program_id(0)
    tm = x1_ref.shape[0]

    @pl.when(i % blocks_per_seq == 0)
    def _():
        halo_sc[...] = jnp.zeros_like(halo_sc)

    h2 = h2_ref[...]

    def conv_cols(slot, col0):
        u = _dot(h2, wup_ref[:, col0:col0 + tf])
        ubuf_sc[slot, 0:SUBLANES, :] = halo_sc[:, col0:col0 + tf]
        ubuf_sc[slot, SUBLANES:SUBLANES + tm, :] = u
        halo_sc[:, col0:col0 + tf] = u[tm - SUBLANES:tm, :]
        w = cw_ref[:, col0:col0 + tf]
        return (cb_ref[:, col0:col0 + tf]
                + w[0:1] * ubuf_sc[slot, SUBLANES - 2:SUBLANES - 2 + tm, :]
                + w[1:2] * ubuf_sc[slot, SUBLANES - 1:SUBLANES - 1 + tm, :]
                + w[2:3] * u)

    for c in range(D_FF // tf):
        a = conv_cols(0, c * tf)
        g = conv_cols(1, D_FF + c * tf)
        act_sc[:, c * tf:(c + 1) * tf] = (jax.nn.gelu(g) * a).astype(bf16)

    y = _dot(act_sc[...], wdn_ref[...])
    out_ref[...] = x1_ref[...] + _rms(y) * npost_ref[...]


def _ffn(x1, h2, wup, cw, cb, wdn, npost, *, seq, tm=512, tf=256):
    t = x1.shape[0]
    tok = lambda i: (i, 0)
    const = lambda i: (0, 0)
    kern = functools.partial(_ffn_kernel, blocks_per_seq=seq // tm, tf=tf)
    return pl.pallas_call(
        kern,
        grid=(t // tm,),
        in_specs=[
            pl.BlockSpec((tm, D_MODEL), tok),
            pl.BlockSpec((tm, D_MODEL), tok),
            pl.BlockSpec((D_MODEL, 2 * D_FF), const, pipeline_mode=pl.Buffered(1)),
            pl.BlockSpec((CONV_W, 2 * D_FF), const),
            pl.BlockSpec((1, 2 * D_FF), const),
            pl.BlockSpec((D_FF, D_MODEL), const, pipeline_mode=pl.Buffered(1)),
            pl.BlockSpec((1, D_MODEL), const),
        ],
        out_specs=pl.BlockSpec((tm, D_MODEL), tok),
        out_shape=jax.ShapeDtypeStruct((t, D_MODEL), f32),
        scratch_shapes=[pltpu.VMEM((SUBLANES, 2 * D_FF), f32),
                        pltpu.VMEM((2, tm + SUBLANES, tf), f32),
                        pltpu.VMEM((tm, D_FF), bf16)],
        compiler_params=pltpu.CompilerParams(
            dimension_semantics=("arbitrary",), vmem_limit_bytes=VMEM_LIMIT),
        name="ffn",
    )(x1, h2, wup, cw, cb, wdn, npost)


def _layer(x2, p, l, *, batch, seq):
    w_in = p["w_in"][l]
    o_i = 2 * ML_QK + ML_V
    o_o = o_i + 2 * ML_HEADS
    o_qf = o_o + ML_V
    o_ff = o_qf + 3 * FOX_W
    o_ga = o_ff + FOX_HEADS
    w_big = jnp.concatenate([
        w_in[:, :o_i],
        w_in[:, o_o:o_qf],
        w_in[:, o_qf:o_ff],
        w_in[:, o_ga:],
    ], axis=1).astype(bf16)
    w_gate = jnp.concatenate([
        w_in[:, o_i:o_o], w_in[:, o_ff:o_ga],
        jnp.zeros((D_MODEL, GATE_W - N_GATES), f32)], axis=1).astype(bf16)
    b_gate = jnp.concatenate([
        p["b_ml_i"][l], p["b_ml_f"][l], p["b_fox_f"][l], jnp.zeros((GATE_W - N_GATES,), f32)])[None, :]
    tri = jnp.tril(jnp.ones((ML_CHUNK, ML_CHUNK), f32)).astype(bf16)
    row = lambda v: v[None, :].astype(f32)

    proj, gcol, grow = _inproj(x2, row(p["norm_mix_pre"][l]), w_big, w_gate, b_gate, tri, seq=seq)
    h_a = _mlstm(proj, gcol, grow, row(p["ml_head_norm"][l]), batch=batch, seq=seq)
    h_b = _fox(proj, gcol, batch=batch, seq=seq)
    x1, h2 = _merge(x2, h_a, h_b, proj,
                    p["w_branch_a"][l].astype(bf16), p["w_branch_b"][l].astype(bf16),
                    p["w_out"][l].astype(bf16), row(p["b_gate_a"][l]), row(p["b_gate_b"][l]),
                    row(p["norm_mix_post"][l]), row(p["norm_ffn_pre"][l]))
    return _ffn(x1, h2, p["w_up"][l].astype(bf16), p["conv_w"][l], row(p["conv_b"][l]),
                p["w_down"][l].astype(bf16), row(p["norm_ffn_post"][l]), seq=seq)


def kernel(x, norm_mix_pre, w_in, b_ml_i, b_ml_f, ml_head_norm, b_fox_f, b_gate_a, b_gate_b,
           w_branch_a, w_branch_b, w_out, norm_mix_post, norm_ffn_pre, w_up, conv_w, conv_b,
           w_down, norm_ffn_post):
    batch, seq, _ = x.shape
    p = dict(norm_mix_pre=norm_mix_pre, w_in=w_in, b_ml_i=b_ml_i, b_ml_f=b_ml_f,
             ml_head_norm=ml_head_norm, b_fox_f=b_fox_f, b_gate_a=b_gate_a, b_gate_b=b_gate_b,
             w_branch_a=w_branch_a, w_branch_b=w_branch_b, w_out=w_out,
             norm_mix_post=norm_mix_post, norm_ffn_pre=norm_ffn_pre, w_up=w_up, conv_w=conv_w,
             conv_b=conv_b, w_down=w_down, norm_ffn_post=norm_ffn_post)
    x2 = x.reshape(batch * seq, D_MODEL)
    for l in range(w_in.shape[0]):
        x2 = _layer(x2, p, l, batch=batch, seq=seq)
    return x2.reshape(batch, seq, D_MODEL)
```

```python
import functools

import jax
import jax.numpy as jnp
from jax import lax
from jax.experimental import pallas as pl
from jax.experimental.pallas import tpu as pltpu

D_MODEL = 1024
ML_HEADS = 4
ML_DQK = 128
ML_DV = 256
ML_QK = ML_HEADS * ML_DQK
ML_V = ML_HEADS * ML_DV
FOX_HEADS = 8
FOX_DH = 128
FOX_W = FOX_HEADS * FOX_DH
D_FF = 2816
CONV_W = 3
GATE_CAP = 15.0
EPS = 1e-6
LOG2E = 1.4426950408889634

LANES = 128
SUBLANES = 8
GATE_W = LANES
N_GATES = 2 * ML_HEADS + FOX_HEADS
ML_CHUNK = 128
VMEM_LIMIT = 56 * 1024 * 1024

PROJ_W = 2 * ML_QK + 2 * ML_V + 3 * FOX_W + 2 * D_MODEL
COL_QM, COL_KM, COL_VM, COL_OM = 0, ML_QK, 2 * ML_QK, 2 * ML_QK + ML_V
COL_QF = COL_OM + ML_V
COL_KF = COL_QF + FOX_W
COL_VF = COL_KF + FOX_W
COL_GA = COL_VF + FOX_W
COL_GB = COL_GA + D_MODEL

f32 = jnp.float32
bf16 = jnp.bfloat16


def _log_sigmoid(z):
    return jnp.minimum(z, 0.0) - jnp.log1p(jnp.exp(-jnp.abs(z)))


def _rms(v):
    return v * lax.rsqrt(jnp.mean(v * v, axis=-1, keepdims=True) + EPS)


def _dot(a, b):
    return jnp.dot(a, b, preferred_element_type=f32)


def _dot_nt(a, b):
    return lax.dot_general(a, b, (((1,), (1,)), ((), ())), preferred_element_type=f32)


def _dot_tn(a, b):
    return lax.dot_general(a, b, (((0,), (0,)), ((), ())), preferred_element_type=f32)


def _inproj_kernel(x_ref, nw_ref, w_ref, wg_ref, bg_ref, tri_ref, cs_ref,
                   proj_ref, gcol_ref, grow_ref, h_sc, carry_sc, *, blocks_per_seq):
    i = pl.program_id(0)
    j = pl.program_id(1)
    tm = x_ref.shape[0]

    @pl.when(j == 0)
    def _():
        hb = (_rms(x_ref[...]) * nw_ref[...]).astype(bf16)
        h_sc[...] = hb
        g = _dot(hb, wg_ref[...]) + bg_ref[...]
        lane = lax.broadcasted_iota(jnp.int32, (tm, GATE_W), 1)
        cap = GATE_CAP * jnp.tanh(g / GATE_CAP)
        act = jnp.where(lane < ML_HEADS, cap,
                        jnp.where(lane < 2 * ML_HEADS, _log_sigmoid(cap), _log_sigmoid(g)))
        act = jnp.where(lane < N_GATES, act, 0.0)

        @pl.when(i % blocks_per_seq == 0)
        def _():
            carry_sc[...] = jnp.zeros_like(carry_sc)

        tri = tri_ref[...]
        sub_lane = lax.broadcasted_iota(jnp.int32, (ML_CHUNK, GATE_W), 1)
        for r in range(tm // ML_CHUNK):
            a = act[r * ML_CHUNK:(r + 1) * ML_CHUNK]
            a_hi = a.astype(bf16)
            r1 = a - a_hi.astype(f32)
            a_mid = r1.astype(bf16)
            a_lo = (r1 - a_mid.astype(f32)).astype(bf16)
            cs = _dot(tri, a_hi) + _dot(tri, a_mid) + _dot(tri, a_lo)
            glob = cs + carry_sc[...]
            carry_sc[...] = glob[ML_CHUNK - 1:ML_CHUNK, :]
            out = jnp.where(sub_lane < ML_HEADS, a, jnp.where(sub_lane < 2 * ML_HEADS, cs, glob))
            gcol_ref[r * ML_CHUNK:(r + 1) * ML_CHUNK, :] = out
            grow_ref[:, r * ML_CHUNK:(r + 1) * ML_CHUNK] = out.T[:N_GATES, :]

    proj_ref[...] = (_dot(h_sc[...], w_ref[...]) * cs_ref[...]).astype(bf16)


def _proj_col_scale():
    cs = jnp.ones((1, PROJ_W), f32)
    cs = cs.at[:, COL_QM:COL_QM + ML_QK].set(ML_DQK ** -0.5)
    return cs.at[:, COL_QF:COL_QF + FOX_W].set(LOG2E * FOX_DH ** -0.5)


def _inproj(x2, nw, w_big, w_gate, b_gate, tri, *, seq, tm=1024, tn=1024):
    t = x2.shape[0]
    kern = functools.partial(_inproj_kernel, blocks_per_seq=seq // tm)
    return pl.pallas_call(
        kern,
        grid=(t // tm, PROJ_W // tn),
        in_specs=[
            pl.BlockSpec((tm, D_MODEL), lambda i, j: (i, 0)),
            pl.BlockSpec((1, D_MODEL), lambda i, j: (0, 0)),
            pl.BlockSpec((D_MODEL, tn), lambda i, j: (0, j)),
            pl.BlockSpec((D_MODEL, GATE_W), lambda i, j: (0, 0)),
            pl.BlockSpec((1, GATE_W), lambda i, j: (0, 0)),
            pl.BlockSpec((ML_CHUNK, ML_CHUNK), lambda i, j: (0, 0)),
            pl.BlockSpec((1, tn), lambda i, j: (0, j)),
        ],
        out_specs=[
            pl.BlockSpec((tm, tn), lambda i, j: (i, j)),
            pl.BlockSpec((tm, GATE_W), lambda i, j: (i, 0)),
            pl.BlockSpec((N_GATES, tm), lambda i, j: (0, i)),
        ],
        out_shape=[
            jax.ShapeDtypeStruct((t, PROJ_W), bf16),
            jax.ShapeDtypeStruct((t, GATE_W), f32),
            jax.ShapeDtypeStruct((N_GATES, t), f32),
        ],
        scratch_shapes=[pltpu.VMEM((tm, D_MODEL), bf16), pltpu.VMEM((1, GATE_W), f32)],
        compiler_params=pltpu.CompilerParams(
            dimension_semantics=("arbitrary", "arbitrary"), vmem_limit_bytes=VMEM_LIMIT),
        name="inproj",
    )(x2, nw, w_big, w_gate, b_gate, tri, _proj_col_scale())


def _mlstm_kernel(q_ref, k_ref, v_ref, o_ref, gcol_ref, grow_ref, hnw_ref, out_ref, s_sc, m_sc):
    c = pl.program_id(1)
    L = ML_CHUNK

    @pl.when(c == 0)
    def _():
        s_sc[...] = jnp.zeros_like(s_sc)
        m_sc[...] = jnp.zeros_like(m_sc)

    row = lax.broadcasted_iota(jnp.int32, (L, L), 0)
    col = lax.broadcasted_iota(jnp.int32, (L, L), 1)
    causal = col <= row
    gc = gcol_ref[...]
    gr = grow_ref[...]
    ones_col = (lax.broadcasted_iota(jnp.int32, (L, LANES), 1) == 0).astype(bf16)

    H = range(ML_HEADS)
    qk_cols = [slice(h * ML_DQK, (h + 1) * ML_DQK) for h in H]
    v_cols = [slice(h * ML_DV, (h + 1) * ML_DV) for h in H]
    q = [q_ref[:, qk_cols[h]] for h in H]
    k = [k_ref[:, qk_cols[h]] for h in H]
    v_aug = [jnp.concatenate([v_ref[:, v_cols[h]], ones_col], axis=1) for h in H]
    li_c = [gc[:, h:h + 1] for h in H]
    b_c = [gc[:, ML_HEADS + h:ML_HEADS + h + 1] for h in H]
    li_r = [gr[h:h + 1, :] for h in H]
    b_r = [gr[ML_HEADS + h:ML_HEADS + h + 1, :] for h in H]
    m_prev = [m_sc[h:h + 1, 0:1] for h in H]
    state = [s_sc[h] for h in H]

    qk = [_dot_nt(q[h], k[h]) for h in H]
    inter_mm = [_dot(q[h], state[h].astype(bf16)) for h in H]
    dmat = [jnp.where(causal, b_c[h] - b_r[h] + li_r[h], -jnp.inf) for h in H]
    inter = [b_c[h] + m_prev[h] for h in H]
    m_t = [jnp.maximum(inter[h], jnp.max(dmat[h], axis=-1, keepdims=True)) for h in H]
    w_intra = [jnp.exp(dmat[h] - m_t[h]) for h in H]
    w_inter = [jnp.exp(inter[h] - m_t[h]) for h in H]
    s = [(qk[h] * w_intra[h]).astype(bf16) for h in H]
    num = [_dot(s[h], v_aug[h]) + w_inter[h] * inter_mm[h] for h in H]
    inv = [1.0 / jnp.maximum(jnp.abs(num[h][:, ML_DV:ML_DV + 1]), jnp.exp(-m_t[h])) for h in H]
    hh = [num[h][:, :ML_DV] * inv[h] for h in H]

    b_last = [b_r[h][:, L - 1:L] for h in H]
    g_r = [b_last[h] - b_r[h] + li_r[h] for h in H]
    m_new = [jnp.maximum(b_last[h] + m_prev[h], jnp.max(g_r[h], axis=-1, keepdims=True)) for h in H]
    a = [jnp.exp(b_last[h] + m_prev[h] - m_new[h]) for h in H]
    wk_c = [jnp.exp(b_last[h] - b_c[h] + li_c[h] - m_new[h]) for h in H]
    upd = [_dot_tn((wk_c[h] * k[h].astype(f32)).astype(bf16), v_aug[h]) for h in H]
    for h in H:
        s_sc[h] = a[h] * state[h] + upd[h]
        m_sc[h:h + 1, :] = jnp.broadcast_to(m_new[h], (1, LANES))

    hn = [_rms(hh[h]) * hnw_ref[:, v_cols[h]] for h in H]
    for h in H:
        gate = jax.nn.sigmoid(o_ref[:, v_cols[h]].astype(f32))
        out_ref[:, v_cols[h]] = (hn[h] * gate).astype(bf16)


def _mlstm(proj, gcol, grow, hnw, *, batch, seq):
    t = proj.shape[0]
    nc = seq // ML_CHUNK
    L = ML_CHUNK
    return pl.pallas_call(
        _mlstm_kernel,
        grid=(batch, nc),
        in_specs=[
            pl.BlockSpec((L, ML_QK), lambda b, c: (b * nc + c, COL_QM // ML_QK)),
            pl.BlockSpec((L, ML_QK), lambda b, c: (b * nc + c, COL_KM // ML_QK)),
            pl.BlockSpec((L, ML_V), lambda b, c: (b * nc + c, COL_VM // ML_V)),
            pl.BlockSpec((L, ML_V), lambda b, c: (b * nc + c, COL_OM // ML_V)),
            pl.BlockSpec((L, GATE_W), lambda b, c: (b * nc + c, 0)),
            pl.BlockSpec((N_GATES, L), lambda b, c: (0, b * nc + c)),
            pl.BlockSpec((1, ML_V), lambda b, c: (0, 0)),
        ],
        out_specs=pl.BlockSpec((L, ML_V), lambda b, c: (b * nc + c, 0)),
        out_shape=jax.ShapeDtypeStruct((t, ML_V), bf16),
        scratch_shapes=[pltpu.VMEM((ML_HEADS, ML_DQK, ML_DV + LANES), f32),
                        pltpu.VMEM((SUBLANES, LANES), f32)],
        compiler_params=pltpu.CompilerParams(
            dimension_semantics=("arbitrary", "arbitrary"), vmem_limit_bytes=VMEM_LIMIT),
        name="mlstm",
    )(proj, proj, proj, proj, gcol, grow, hnw)


def _fox_kernel(q_ref, k_ref, v_ref, gcol_ref, out_ref, kaug_sc, st_sc, mx_sc, m_sc, l_sc, acc_sc,
                *, blk, setup_rows):
    hg = pl.program_id(1)
    qi = pl.program_id(2)
    seq = k_ref.shape[0]
    nh = kaug_sc.shape[0]
    head_cols = [slice(i * FOX_DH, (i + 1) * FOX_DH) for i in range(nh)]

    @pl.when(qi == 0)
    def _():
        r = lax.broadcasted_iota(jnp.int32, (GATE_W, LANES), 0)
        c = lax.broadcasted_iota(jnp.int32, (GATE_W, LANES), 1)
        for n in range(seq // setup_rows):
            rows = pl.ds(n * setup_rows, setup_rows)
            a = gcol_ref[rows, :] * (-LOG2E)
            a_hi = a.astype(bf16)
            r1 = a - a_hi.astype(f32)
            a_mid = r1.astype(bf16)
            a_lo = (r1 - a_mid.astype(f32)).astype(bf16)
            for i in range(nh):
                src = 2 * ML_HEADS + hg * nh + i
                pick = [((r == src) & (c == j)).astype(bf16) for j in range(3)]
                aug = _dot(a_hi, pick[0]) + _dot(a_mid, pick[1]) + _dot(a_lo, pick[2])
                kaug_sc[i, rows, :FOX_DH] = k_ref[rows, head_cols[i]]
                kaug_sc[i, rows, FOX_DH:] = aug.astype(bf16)

    ones3 = (lax.broadcasted_iota(jnp.int32, (blk, LANES), 1) < 3).astype(bf16)
    q_aug = [jnp.concatenate([q_ref[:, head_cols[i]], ones3], axis=1) for i in range(nh)]

    m_sc[...] = jnp.full(m_sc.shape, -jnp.inf, f32)
    l_sc[...] = jnp.zeros(l_sc.shape, f32)
    acc_sc[...] = jnp.zeros(acc_sc.shape, f32)

    def logits_stage(i, j, masked):
        start = pl.multiple_of(j * blk, blk)
        st = _dot_nt(kaug_sc[i, pl.ds(start, blk), :], q_aug[i])
        if masked:
            row = lax.broadcasted_iota(jnp.int32, (blk, blk), 0)
            col = lax.broadcasted_iota(jnp.int32, (blk, blk), 1)
            st = jnp.where(row <= col, st, -jnp.inf)
        st_sc[i] = st
        mx_sc[i] = jnp.max(st, axis=0, keepdims=True)

    def softmax_stage(i, j):
        start = pl.multiple_of(j * blk, blk)
        m = m_sc[i]
        m_new = jnp.maximum(m, mx_sc[i])
        alpha = jnp.exp2(m - m_new)
        p = jnp.exp2(st_sc[i] - m_new)
        l_sc[i] = alpha * l_sc[i] + jnp.sum(p, axis=0, keepdims=True)
        acc_sc[i] = alpha * acc_sc[i] + _dot_tn(v_ref[pl.ds(start, blk), head_cols[i]], p.astype(bf16))
        m_sc[i] = m_new

    def round_robin(cur, nxt, masked):
        for i in range(nh):
            if i + 1 < nh:
                logits_stage(i + 1, cur, masked)
            else:
                logits_stage(0, nxt, False)
            softmax_stage(i, cur)

    logits_stage(0, qi, True)
    round_robin(qi, 0, True)

    def body(k, _):
        round_robin(k - 1, jnp.minimum(k, qi - 1), False)
        return 0

    lax.fori_loop(1, qi + 1, body, 0)
    for i in range(nh):
        out_ref[:, head_cols[i]] = (acc_sc[i] * (1.0 / l_sc[i])).T.astype(bf16)


def _fox(proj, gcol, *, batch, seq, blk=512, nh=4):
    assert nh >= 2
    t = proj.shape[0]
    nq = seq // blk
    w = nh * FOX_DH
    kern = functools.partial(_fox_kernel, blk=blk, setup_rows=min(seq, 512))
    return pl.pallas_call(
        kern,
        grid=(batch, FOX_HEADS // nh, nq),
        in_specs=[
            pl.BlockSpec((blk, w), lambda b, h, i: (b * nq + i, COL_QF // w + h)),
            pl.BlockSpec((seq, w), lambda b, h, i: (b, COL_KF // w + h)),
            pl.BlockSpec((seq, w), lambda b, h, i: (b, COL_VF // w + h)),
            pl.BlockSpec((seq, GATE_W), lambda b, h, i: (b, 0)),
        ],
        out_specs=pl.BlockSpec((blk, w), lambda b, h, i: (b * nq + i, h)),
        out_shape=jax.ShapeDtypeStruct((t, FOX_W), bf16),
        scratch_shapes=[pltpu.VMEM((nh, seq, FOX_DH + LANES), bf16),
                        pltpu.VMEM((nh, blk, blk), f32),
                        pltpu.VMEM((nh, 1, blk), f32),
                        pltpu.VMEM((nh, 1, blk), f32),
                        pltpu.VMEM((nh, 1, blk), f32),
                        pltpu.VMEM((nh, FOX_DH, blk), f32)],
        compiler_params=pltpu.CompilerParams(
            dimension_semantics=("arbitrary", "arbitrary", "arbitrary"), vmem_limit_bytes=VMEM_LIMIT),
        name="fox",
    )(proj, proj, proj, gcol)


def _merge_kernel(x_ref, ha_ref, hb_ref, ga_ref, gb_ref, wa_ref, wb_ref, wo_ref,
                  bga_ref, bgb_ref, npost_ref, npre_ref, x1_ref, h2_ref):
    ya = _dot(ha_ref[...], wa_ref[...])
    yb = _dot(hb_ref[...], wb_ref[...])
    merged = (jax.nn.sigmoid(ga_ref[...].astype(f32) + bga_ref[...]) * ya
              + jax.nn.sigmoid(gb_ref[...].astype(f32) + bgb_ref[...]) * yb)
    z = _dot(merged.astype(bf16), wo_ref[...])
    x1 = x_ref[...] + _rms(z) * npost_ref[...]
    x1_ref[...] = x1
    h2_ref[...] = (_rms(x1) * npre_ref[...]).astype(bf16)


def _merge(x2, h_a, h_b, proj, wa, wb, wo, bga, bgb, npost, npre, *, tm=512):
    t = x2.shape[0]
    tok = lambda i: (i, 0)
    const = lambda i: (0, 0)
    wspec = pl.BlockSpec((D_MODEL, D_MODEL), const, pipeline_mode=pl.Buffered(1))
    vspec = pl.BlockSpec((1, D_MODEL), const)
    return pl.pallas_call(
        _merge_kernel,
        grid=(t // tm,),
        in_specs=[
            pl.BlockSpec((tm, D_MODEL), tok),
            pl.BlockSpec((tm, ML_V), tok),
            pl.BlockSpec((tm, FOX_W), tok),
            pl.BlockSpec((tm, D_MODEL), lambda i: (i, COL_GA // D_MODEL)),
            pl.BlockSpec((tm, D_MODEL), lambda i: (i, COL_GB // D_MODEL)),
            wspec, wspec, wspec, vspec, vspec, vspec, vspec,
        ],
        out_specs=[pl.BlockSpec((tm, D_MODEL), tok), pl.BlockSpec((tm, D_MODEL), tok)],
        out_shape=[jax.ShapeDtypeStruct((t, D_MODEL), f32), jax.ShapeDtypeStruct((t, D_MODEL), bf16)],
        compiler_params=pltpu.CompilerParams(
            dimension_semantics=("arbitrary",), vmem_limit_bytes=VMEM_LIMIT),
        name="merge",
    )(x2, h_a, h_b, proj, proj, wa, wb, wo, bga, bgb, npost, npre)


def _ffn_kernel(x1_ref, h2_ref, wup_ref, cw_ref, cb_ref, wdn_ref, npost_ref, out_ref,
                halo_sc, ubuf_sc, act_sc, *, blocks_per_seq, tf):
    i = pl.program_id(0)
    tm = x1_ref.shape[0]

    @pl.when(i % blocks_per_seq == 0)
    def _():
        halo_sc[...] = jnp.zeros_like(halo_sc)

    h2 = h2_ref[...]

    def conv_cols(slot, col0):
        u = _dot(h2, wup_ref[:, col0:col0 + tf])
        ubuf_sc[slot, 0:SUBLANES, :] = halo_sc[:, col0:col0 + tf]
        ubuf_sc[slot, SUBLANES:SUBLANES + tm, :] = u
        halo_sc[:, col0:col0 + tf] = u[tm - SUBLANES:tm, :]
        w = cw_ref[:, col0:col0 + tf]
        return (cb_ref[:, col0:col0 + tf]
                + w[0:1] * ubuf_sc[slot, SUBLANES - 2:SUBLANES - 2 + tm, :]
                + w[1:2] * ubuf_sc[slot, SUBLANES - 1:SUBLANES - 1 + tm, :]
                + w[2:3] * u)

    for c in range(D_FF // tf):
        a = conv_cols(0, c * tf)
        g = conv_cols(1, D_FF + c * tf)
        act_sc[:, c * tf:(c + 1) * tf] = (jax.nn.gelu(g) * a).astype(bf16)

    y = _dot(act_sc[...], wdn_ref[...])
    out_ref[...] = x1_ref[...] + _rms(y) * npost_ref[...]


def _ffn(x1, h2, wup, cw, cb, wdn, npost, *, seq, tm=512, tf=256):
    t = x1.shape[0]
    tok = lambda i: (i, 0)
    const = lambda i: (0, 0)
    kern = functools.partial(_ffn_kernel, blocks_per_seq=seq // tm, tf=tf)
    return pl.pallas_call(
        kern,
        grid=(t // tm,),
        in_specs=[
            pl.BlockSpec((tm, D_MODEL), tok),
            pl.BlockSpec((tm, D_MODEL), tok),
            pl.BlockSpec((D_MODEL, 2 * D_FF), const, pipeline_mode=pl.Buffered(1)),
            pl.BlockSpec((CONV_W, 2 * D_FF), const),
            pl.BlockSpec((1, 2 * D_FF), const),
            pl.BlockSpec((D_FF, D_MODEL), const, pipeline_mode=pl.Buffered(1)),
            pl.BlockSpec((1, D_MODEL), const),
        ],
        out_specs=pl.BlockSpec((tm, D_MODEL), tok),
        out_shape=jax.ShapeDtypeStruct((t, D_MODEL), f32),
        scratch_shapes=[pltpu.VMEM((SUBLANES, 2 * D_FF), f32),
                        pltpu.VMEM((2, tm + SUBLANES, tf), f32),
                        pltpu.VMEM((tm, D_FF), bf16)],
        compiler_params=pltpu.CompilerParams(
            dimension_semantics=("arbitrary",), vmem_limit_bytes=VMEM_LIMIT),
        name="ffn",
    )(x1, h2, wup, cw, cb, wdn, npost)


def _layer(x2, p, l, *, batch, seq):
    w_in = p["w_in"][l]
    o_i = 2 * ML_QK + ML_V
    o_o = o_i + 2 * ML_HEADS
    o_qf = o_o + ML_V
    o_ff = o_qf + 3 * FOX_W
    o_ga = o_ff + FOX_HEADS
    w_big = jnp.concatenate([
        w_in[:, :o_i],
        w_in[:, o_o:o_qf],
        w_in[:, o_qf:o_ff],
        w_in[:, o_ga:],
    ], axis=1).astype(bf16)
    w_gate = jnp.concatenate([
        w_in[:, o_i:o_o], w_in[:, o_ff:o_ga],
        jnp.zeros((D_MODEL, GATE_W - N_GATES), f32)], axis=1).astype(bf16)
    b_gate = jnp.concatenate([
        p["b_ml_i"][l], p["b_ml_f"][l], p["b_fox_f"][l], jnp.zeros((GATE_W - N_GATES,), f32)])[None, :]
    tri = jnp.tril(jnp.ones((ML_CHUNK, ML_CHUNK), f32)).astype(bf16)
    row = lambda v: v[None, :].astype(f32)

    proj, gcol, grow = _inproj(x2, row(p["norm_mix_pre"][l]), w_big, w_gate, b_gate, tri, seq=seq)
    h_a = _mlstm(proj, gcol, grow, row(p["ml_head_norm"][l]), batch=batch, seq=seq)
    h_b = _fox(proj, gcol, batch=batch, seq=seq)
    x1, h2 = _merge(x2, h_a, h_b, proj,
                    p["w_branch_a"][l].astype(bf16), p["w_branch_b"][l].astype(bf16),
                    p["w_out"][l].astype(bf16), row(p["b_gate_a"][l]), row(p["b_gate_b"][l]),
                    row(p["norm_mix_post"][l]), row(p["norm_ffn_pre"][l]))
    return _ffn(x1, h2, p["w_up"][l].astype(bf16), p["conv_w"][l], row(p["conv_b"][l]),
                p["w_down"][l].astype(bf16), row(p["norm_ffn_post"][l]), seq=seq)


def kernel(x, norm_mix_pre, w_in, b_ml_i, b_ml_f, ml_head_norm, b_fox_f, b_gate_a, b_gate_b,
           w_branch_a, w_branch_b, w_out, norm_mix_post, norm_ffn_pre, w_up, conv_w, conv_b,
           w_down, norm_ffn_post):
    batch, seq, _ = x.shape
    p = dict(norm_mix_pre=norm_mix_pre, w_in=w_in, b_ml_i=b_ml_i, b_ml_f=b_ml_f,
             ml_head_norm=ml_head_norm, b_fox_f=b_fox_f, b_gate_a=b_gate_a, b_gate_b=b_gate_b,
             w_branch_a=w_branch_a, w_branch_b=w_branch_b, w_out=w_out,
             norm_mix_post=norm_mix_post, norm_ffn_pre=norm_ffn_pre, w_up=w_up, conv_w=conv_w,
             conv_b=conv_b, w_down=w_down, norm_ffn_post=norm_ffn_post)
    x2 = x.reshape(batch * seq, D_MODEL)
    for l in range(w_in.shape[0]):
        x2 = _layer(x2, p, l, batch=batch, seq=seq)
    return x2.reshape(batch, seq, D_MODEL)
```

```python
import functools

import jax
import jax.numpy as jnp
from jax import lax
from jax.experimental import pallas as pl
from jax.experimental.pallas import tpu as pltpu

D_MODEL = 1024
ML_HEADS = 4
ML_DQK = 128
ML_DV = 256
ML_QK = ML_HEADS * ML_DQK
ML_V = ML_HEADS * ML_DV
FOX_HEADS = 8
FOX_DH = 128
FOX_W = FOX_HEADS * FOX_DH
D_FF = 2816
CONV_W = 3
GATE_CAP = 15.0
EPS = 1e-6
LOG2E = 1.4426950408889634
GELU_CUBIC = 0.044715
GELU_K1 = -2.0 * (2.0 / 3.141592653589793) ** 0.5 * LOG2E

LANES = 128
SUBLANES = 8
GATE_W = LANES
N_GATES = 2 * ML_HEADS + FOX_HEADS
ML_CHUNK = 128
VMEM_LIMIT = 56 * 1024 * 1024

PROJ_W = 2 * ML_QK + 2 * ML_V + 3 * FOX_W + 2 * D_MODEL
COL_QM, COL_KM, COL_VM, COL_OM = 0, ML_QK, 2 * ML_QK, 2 * ML_QK + ML_V
COL_QF = COL_OM + ML_V
COL_KF = COL_QF + FOX_W
COL_VF = COL_KF + FOX_W
COL_GA = COL_VF + FOX_W
COL_GB = COL_GA + D_MODEL

f32 = jnp.float32
bf16 = jnp.bfloat16


def _log_sigmoid(z):
    return jnp.minimum(z, 0.0) - jnp.log1p(jnp.exp(-jnp.abs(z)))


def _rms(v):
    return v * lax.rsqrt(jnp.mean(v * v, axis=-1, keepdims=True) + EPS)


def _dot(a, b):
    return jnp.dot(a, b, preferred_element_type=f32)


def _dot_nt(a, b):
    return lax.dot_general(a, b, (((1,), (1,)), ((), ())), preferred_element_type=f32)


def _dot_tn(a, b):
    return lax.dot_general(a, b, (((0,), (0,)), ((), ())), preferred_element_type=f32)


def _inproj_kernel(x_ref, nw_ref, w_ref, wg_ref, bg_ref, tri_ref, cs_ref,
                   proj_ref, gcol_ref, grow_ref, h_sc, carry_sc, *, blocks_per_seq, piece):
    i = pl.program_id(0)
    j = pl.program_id(1)
    tm = x_ref.shape[0]

    def project(hb):
        return (_dot(hb, w_ref[...]) * cs_ref[...]).astype(bf16)

    @pl.when(j == 0)
    def _():
        @pl.when(i % blocks_per_seq == 0)
        def _():
            carry_sc[...] = jnp.zeros_like(carry_sc)

        tri = tri_ref[...]
        lane = lax.broadcasted_iota(jnp.int32, (ML_CHUNK, GATE_W), 1)
        per_piece = piece // ML_CHUNK
        for r in range(tm // ML_CHUNK):
            rows = slice(r * ML_CHUNK, (r + 1) * ML_CHUNK)
            if r % per_piece == 0:
                prow = slice(r * ML_CHUNK, r * ML_CHUNK + piece)
                hb = (_rms(x_ref[prow, :]) * nw_ref[...]).astype(bf16)
                h_sc[prow, :] = hb
                proj_ref[prow, :] = project(hb)
                g_piece = _dot(hb, wg_ref[...]) + bg_ref[...]
            g = g_piece[(r % per_piece) * ML_CHUNK:(r % per_piece + 1) * ML_CHUNK]
            cap = GATE_CAP * jnp.tanh(g / GATE_CAP)
            a = jnp.where(lane < ML_HEADS, cap,
                          jnp.where(lane < 2 * ML_HEADS, _log_sigmoid(cap), _log_sigmoid(g)))
            a = jnp.where(lane < N_GATES, a, 0.0)
            a_hi = a.astype(bf16)
            r1 = a - a_hi.astype(f32)
            a_mid = r1.astype(bf16)
            a_lo = (r1 - a_mid.astype(f32)).astype(bf16)
            cs = _dot(tri, a_hi) + _dot(tri, a_mid) + _dot(tri, a_lo)
            glob = cs + carry_sc[...]
            carry_sc[...] = glob[ML_CHUNK - 1:ML_CHUNK, :]
            out = jnp.where(lane < ML_HEADS, a, jnp.where(lane < 2 * ML_HEADS, cs, glob))
            gcol_ref[rows, :] = out
            grow_ref[:, rows] = out.T[:N_GATES, :]

    @pl.when(j != 0)
    def _():
        proj_ref[...] = project(h_sc[...])


def _proj_col_scale():
    cs = jnp.ones((1, PROJ_W), f32)
    cs = cs.at[:, COL_QM:COL_QM + ML_QK].set(ML_DQK ** -0.5)
    return cs.at[:, COL_QF:COL_QF + FOX_W].set(LOG2E * FOX_DH ** -0.5)


def _inproj(x2, nw, w_big, w_gate, b_gate, tri, *, seq, tm=1024, tn=1024, piece=256):
    t = x2.shape[0]
    kern = functools.partial(_inproj_kernel, blocks_per_seq=seq // tm, piece=piece)
    return pl.pallas_call(
        kern,
        grid=(t // tm, PROJ_W // tn),
        in_specs=[
            pl.BlockSpec((tm, D_MODEL), lambda i, j: (i, 0)),
            pl.BlockSpec((1, D_MODEL), lambda i, j: (0, 0)),
            pl.BlockSpec((D_MODEL, tn), lambda i, j: (0, j)),
            pl.BlockSpec((D_MODEL, GATE_W), lambda i, j: (0, 0)),
            pl.BlockSpec((1, GATE_W), lambda i, j: (0, 0)),
            pl.BlockSpec((ML_CHUNK, ML_CHUNK), lambda i, j: (0, 0)),
            pl.BlockSpec((1, tn), lambda i, j: (0, j)),
        ],
        out_specs=[
            pl.BlockSpec((tm, tn), lambda i, j: (i, j)),
            pl.BlockSpec((tm, GATE_W), lambda i, j: (i, 0)),
            pl.BlockSpec((N_GATES, tm), lambda i, j: (0, i)),
        ],
        out_shape=[
            jax.ShapeDtypeStruct((t, PROJ_W), bf16),
            jax.ShapeDtypeStruct((t, GATE_W), f32),
            jax.ShapeDtypeStruct((N_GATES, t), f32),
        ],
        scratch_shapes=[pltpu.VMEM((tm, D_MODEL), bf16), pltpu.VMEM((1, GATE_W), f32)],
        compiler_params=pltpu.CompilerParams(
            dimension_semantics=("arbitrary", "arbitrary"), vmem_limit_bytes=VMEM_LIMIT),
        name="inproj",
    )(x2, nw, w_big, w_gate, b_gate, tri, _proj_col_scale())


def _mlstm_kernel(q_ref, k_ref, v_ref, o_ref, gcol_ref, grow_ref, hnw_ref, out_ref, s_sc, m_sc):
    c = pl.program_id(1)
    L = ML_CHUNK

    @pl.when(c == 0)
    def _():
        s_sc[...] = jnp.zeros_like(s_sc)
        m_sc[...] = jnp.zeros_like(m_sc)

    row = lax.broadcasted_iota(jnp.int32, (L, L), 0)
    col = lax.broadcasted_iota(jnp.int32, (L, L), 1)
    causal = col <= row
    gc = gcol_ref[...]
    gr = grow_ref[...]
    ones_col = (lax.broadcasted_iota(jnp.int32, (L, LANES), 1) == 0).astype(bf16)

    H = range(ML_HEADS)
    qk_cols = [slice(h * ML_DQK, (h + 1) * ML_DQK) for h in H]
    v_cols = [slice(h * ML_DV, (h + 1) * ML_DV) for h in H]
    q = [q_ref[:, qk_cols[h]] for h in H]
    k = [k_ref[:, qk_cols[h]] for h in H]
    v_aug = [jnp.concatenate([v_ref[:, v_cols[h]], ones_col], axis=1) for h in H]
    li_c = [gc[:, h:h + 1] for h in H]
    b_c = [gc[:, ML_HEADS + h:ML_HEADS + h + 1] for h in H]
    li_r = [gr[h:h + 1, :] for h in H]
    b_r = [gr[ML_HEADS + h:ML_HEADS + h + 1, :] for h in H]
    m_prev = [m_sc[h:h + 1, 0:1] for h in H]
    state = [s_sc[h] for h in H]

    qk = [_dot_nt(q[h], k[h]) for h in H]
    inter_mm = [_dot(q[h], state[h].astype(bf16)) for h in H]
    dmat = [jnp.where(causal, b_c[h] - b_r[h] + li_r[h], -jnp.inf) for h in H]
    inter = [b_c[h] + m_prev[h] for h in H]
    m_t = [jnp.maximum(inter[h], jnp.max(dmat[h], axis=-1, keepdims=True)) for h in H]
    w_intra = [jnp.exp(dmat[h] - m_t[h]) for h in H]
    w_inter = [jnp.exp(inter[h] - m_t[h]) for h in H]
    s = [(qk[h] * w_intra[h]).astype(bf16) for h in H]
    num = [_dot(s[h], v_aug[h]) + w_inter[h] * inter_mm[h] for h in H]
    inv = [1.0 / jnp.maximum(jnp.abs(num[h][:, ML_DV:ML_DV + 1]), jnp.exp(-m_t[h])) for h in H]
    hh = [num[h][:, :ML_DV] * inv[h] for h in H]

    b_last = [b_r[h][:, L - 1:L] for h in H]
    g_r = [b_last[h] - b_r[h] + li_r[h] for h in H]
    m_new = [jnp.maximum(b_last[h] + m_prev[h], jnp.max(g_r[h], axis=-1, keepdims=True)) for h in H]
    a = [jnp.exp(b_last[h] + m_prev[h] - m_new[h]) for h in H]
    wk_c = [jnp.exp(b_last[h] - b_c[h] + li_c[h] - m_new[h]) for h in H]
    upd = [_dot_tn((wk_c[h] * k[h].astype(f32)).astype(bf16), v_aug[h]) for h in H]
    for h in H:
        s_sc[h] = a[h] * state[h] + upd[h]
        m_sc[h:h + 1, :] = jnp.broadcast_to(m_new[h], (1, LANES))

    hn = [_rms(hh[h]) * hnw_ref[:, v_cols[h]] for h in H]
    for h in H:
        gate = jax.nn.sigmoid(o_ref[:, v_cols[h]].astype(f32))
        out_ref[:, v_cols[h]] = (hn[h] * gate).astype(bf16)


def _mlstm(proj, gcol, grow, hnw, *, batch, seq):
    t = proj.shape[0]
    nc = seq // ML_CHUNK
    L = ML_CHUNK
    return pl.pallas_call(
        _mlstm_kernel,
        grid=(batch, nc),
        in_specs=[
            pl.BlockSpec((L, ML_QK), lambda b, c: (b * nc + c, COL_QM // ML_QK)),
            pl.BlockSpec((L, ML_QK), lambda b, c: (b * nc + c, COL_KM // ML_QK)),
            pl.BlockSpec((L, ML_V), lambda b, c: (b * nc + c, COL_VM // ML_V)),
            pl.BlockSpec((L, ML_V), lambda b, c: (b * nc + c, COL_OM // ML_V)),
            pl.BlockSpec((L, GATE_W), lambda b, c: (b * nc + c, 0)),
            pl.BlockSpec((N_GATES, L), lambda b, c: (0, b * nc + c)),
            pl.BlockSpec((1, ML_V), lambda b, c: (0, 0)),
        ],
        out_specs=pl.BlockSpec((L, ML_V), lambda b, c: (b * nc + c, 0)),
        out_shape=jax.ShapeDtypeStruct((t, ML_V), bf16),
        scratch_shapes=[pltpu.VMEM((ML_HEADS, ML_DQK, ML_DV + LANES), f32),
                        pltpu.VMEM((SUBLANES, LANES), f32)],
        compiler_params=pltpu.CompilerParams(
            dimension_semantics=("arbitrary", "arbitrary"), vmem_limit_bytes=VMEM_LIMIT),
        name="mlstm",
    )(proj, proj, proj, proj, gcol, grow, hnw)


def _fox_kernel(q_ref, k_ref, v_ref, gcol_ref, out_ref, kaug_sc, st_sc, mx_sc, m_sc, l_sc, acc_sc,
                *, blk, setup_rows, ahead):
    hg = pl.program_id(1)
    qi = pl.program_id(2)
    seq = k_ref.shape[0]
    nh = kaug_sc.shape[0]
    head_cols = [slice(i * FOX_DH, (i + 1) * FOX_DH) for i in range(nh)]

    @pl.when(qi == 0)
    def _():
        r = lax.broadcasted_iota(jnp.int32, (GATE_W, LANES), 0)
        c = lax.broadcasted_iota(jnp.int32, (GATE_W, LANES), 1)
        for n in range(seq // setup_rows):
            rows = pl.ds(n * setup_rows, setup_rows)
            a = gcol_ref[rows, :] * (-LOG2E)
            a_hi = a.astype(bf16)
            r1 = a - a_hi.astype(f32)
            a_mid = r1.astype(bf16)
            a_lo = (r1 - a_mid.astype(f32)).astype(bf16)
            for i in range(nh):
                src = 2 * ML_HEADS + hg * nh + i
                pick = [((r == src) & (c == j)).astype(bf16) for j in range(3)]
                aug = _dot(a_hi, pick[0]) + _dot(a_mid, pick[1]) + _dot(a_lo, pick[2])
                kaug_sc[i, rows, :FOX_DH] = k_ref[rows, head_cols[i]]
                kaug_sc[i, rows, FOX_DH:] = aug.astype(bf16)

    ones3 = (lax.broadcasted_iota(jnp.int32, (blk, LANES), 1) < 3).astype(bf16)
    q_aug = [jnp.concatenate([q_ref[:, head_cols[i]], ones3], axis=1) for i in range(nh)]

    m_sc[...] = jnp.full(m_sc.shape, -jnp.inf, f32)
    l_sc[...] = jnp.zeros(l_sc.shape, f32)
    acc_sc[...] = jnp.zeros(acc_sc.shape, f32)

    def logits_stage(i, j, masked):
        start = pl.multiple_of(j * blk, blk)
        st = _dot_nt(kaug_sc[i, pl.ds(start, blk), :], q_aug[i])
        if masked:
            row = lax.broadcasted_iota(jnp.int32, (blk, blk), 0)
            col = lax.broadcasted_iota(jnp.int32, (blk, blk), 1)
            st = jnp.where(row <= col, st, -jnp.inf)
        st_sc[i] = st
        mx_sc[i] = jnp.max(st, axis=0, keepdims=True)

    def softmax_stage(i, j):
        start = pl.multiple_of(j * blk, blk)
        m = m_sc[i]
        m_new = jnp.maximum(m, mx_sc[i])
        alpha = jnp.exp2(m - m_new)
        p = jnp.exp2(st_sc[i] - m_new)
        l_sc[i] = alpha * l_sc[i] + jnp.sum(p, axis=0, keepdims=True)
        acc_sc[i] = alpha * acc_sc[i] + _dot_tn(v_ref[pl.ds(start, blk), head_cols[i]], p.astype(bf16))
        m_sc[i] = m_new

    def round_robin(cur, nxt, masked):
        for i in range(nh):
            if i + ahead < nh:
                logits_stage(i + ahead, cur, masked)
            elif nxt is not None:
                logits_stage(i + ahead - nh, nxt, False)
            softmax_stage(i, cur)

    for i in range(ahead):
        logits_stage(i, qi, True)

    @pl.when(qi == 0)
    def _():
        round_robin(qi, None, True)

    @pl.when(qi > 0)
    def _():
        round_robin(qi, 0, True)

        def body(k, _):
            round_robin(k - 1, k, False)
            return 0

        lax.fori_loop(1, qi, body, 0)
        round_robin(qi - 1, None, False)

    for i in range(nh):
        out_ref[:, head_cols[i]] = (acc_sc[i] * (1.0 / l_sc[i])).T.astype(bf16)


def _fox(proj, gcol, *, batch, seq, blk=512, nh=4, ahead=2):
    assert 1 <= ahead < nh
    t = proj.shape[0]
    nq = seq // blk
    w = nh * FOX_DH
    kern = functools.partial(_fox_kernel, blk=blk, setup_rows=min(seq, 512), ahead=ahead)
    return pl.pallas_call(
        kern,
        grid=(batch, FOX_HEADS // nh, nq),
        in_specs=[
            pl.BlockSpec((blk, w), lambda b, h, i: (b * nq + i, COL_QF // w + h)),
            pl.BlockSpec((seq, w), lambda b, h, i: (b, COL_KF // w + h)),
            pl.BlockSpec((seq, w), lambda b, h, i: (b, COL_VF // w + h)),
            pl.BlockSpec((seq, GATE_W), lambda b, h, i: (b, 0)),
        ],
        out_specs=pl.BlockSpec((blk, w), lambda b, h, i: (b * nq + i, h)),
        out_shape=jax.ShapeDtypeStruct((t, FOX_W), bf16),
        scratch_shapes=[pltpu.VMEM((nh, seq, FOX_DH + LANES), bf16),
                        pltpu.VMEM((nh, blk, blk), f32),
                        pltpu.VMEM((nh, 1, blk), f32),
                        pltpu.VMEM((nh, 1, blk), f32),
                        pltpu.VMEM((nh, 1, blk), f32),
                        pltpu.VMEM((nh, FOX_DH, blk), f32)],
        compiler_params=pltpu.CompilerParams(
            dimension_semantics=("arbitrary", "arbitrary", "arbitrary"), vmem_limit_bytes=VMEM_LIMIT),
        name="fox",
    )(proj, proj, proj, gcol)


def _merge_kernel(x_ref, ha_ref, hb_ref, ga_ref, gb_ref, wa_ref, wb_ref, wo_ref,
                  bga_ref, bgb_ref, npost_ref, npre_ref, x1_ref, h2_ref, *, sub):
    n_sub = x_ref.shape[0] // sub
    rows = [slice(r * sub, (r + 1) * sub) for r in range(n_sub)]

    def branches(r):
        return _dot(ha_ref[rows[r], :], wa_ref[...]), _dot(hb_ref[rows[r], :], wb_ref[...])

    def finish(r, z):
        x1 = x_ref[rows[r], :] + _rms(z) * npost_ref[...]
        x1_ref[rows[r], :] = x1
        h2_ref[rows[r], :] = (_rms(x1) * npre_ref[...]).astype(bf16)

    y_next = branches(0)
    z_prev = None
    for r in range(n_sub):
        ya, yb = y_next
        if r + 1 < n_sub:
            y_next = branches(r + 1)
        merged = (jax.nn.sigmoid(ga_ref[rows[r], :].astype(f32) + bga_ref[...]) * ya
                  + jax.nn.sigmoid(gb_ref[rows[r], :].astype(f32) + bgb_ref[...]) * yb)
        z = _dot(merged.astype(bf16), wo_ref[...])
        if z_prev is not None:
            finish(r - 1, z_prev)
        z_prev = z
    finish(n_sub - 1, z_prev)


def _merge(x2, h_a, h_b, proj, wa, wb, wo, bga, bgb, npost, npre, *, tm=1024, sub=256):
    t = x2.shape[0]
    tm = min(tm, t)
    tok = lambda i: (i, 0)
    const = lambda i: (0, 0)
    wspec = pl.BlockSpec((D_MODEL, D_MODEL), const, pipeline_mode=pl.Buffered(1))
    vspec = pl.BlockSpec((1, D_MODEL), const)
    return pl.pallas_call(
        functools.partial(_merge_kernel, sub=sub),
        grid=(t // tm,),
        in_specs=[
            pl.BlockSpec((tm, D_MODEL), tok),
            pl.BlockSpec((tm, ML_V), tok),
            pl.BlockSpec((tm, FOX_W), tok),
            pl.BlockSpec((tm, D_MODEL), lambda i: (i, COL_GA // D_MODEL)),
            pl.BlockSpec((tm, D_MODEL), lambda i: (i, COL_GB // D_MODEL)),
            wspec, wspec, wspec, vspec, vspec, vspec, vspec,
        ],
        out_specs=[pl.BlockSpec((tm, D_MODEL), tok), pl.BlockSpec((tm, D_MODEL), tok)],
        out_shape=[jax.ShapeDtypeStruct((t, D_MODEL), f32), jax.ShapeDtypeStruct((t, D_MODEL), bf16)],
        compiler_params=pltpu.CompilerParams(
            dimension_semantics=("arbitrary",), vmem_limit_bytes=VMEM_LIMIT),
        name="merge",
    )(x2, h_a, h_b, proj, proj, wa, wb, wo, bga, bgb, npost, npre)


def _ffn_kernel(x1_ref, h2_ref, wup_ref, cw_ref, cb_ref, wdn_ref, npost_ref, out_ref,
                halo_sc, ubuf_sc, act_sc, acc_sc, *, blocks_per_seq, tf, down_group):
    i = pl.program_id(0)
    tm = x1_ref.shape[0]

    @pl.when(i % blocks_per_seq == 0)
    def _():
        halo_sc[...] = jnp.zeros_like(halo_sc)

    h2 = h2_ref[...]
    n_chunks = D_FF // tf

    def up(c):
        for half in range(2):
            col0 = half * D_FF + c * tf
            slot = 2 * (c % 2) + half
            u = _dot(h2, wup_ref[:, col0:col0 + tf])
            ubuf_sc[slot, 0:SUBLANES, :] = halo_sc[:, col0:col0 + tf]
            ubuf_sc[slot, SUBLANES:SUBLANES + tm, :] = u
            halo_sc[:, col0:col0 + tf] = u[tm - SUBLANES:tm, :]

    def conv(c, half):
        col0 = half * D_FF + c * tf
        slot = 2 * (c % 2) + half
        w = cw_ref[:, col0:col0 + tf]
        y = cb_ref[:, col0:col0 + tf]
        for j in range(CONV_W):
            first = SUBLANES - (CONV_W - 1 - j)
            y = y + w[j:j + 1] * ubuf_sc[slot, first:first + tm, :]
        return y

    def gate(g, a):
        inner = g * (GELU_K1 + (GELU_K1 * GELU_CUBIC) * (g * g))
        return (g * a) * (1.0 / (1.0 + jnp.exp2(inner)))

    def down(c0, c1):
        return _dot(act_sc[:, c0 * tf:c1 * tf], wdn_ref[c0 * tf:c1 * tf, :])

    up(0)
    pending, done = None, 0
    for c in range(n_chunks):
        if c + 1 < n_chunks:
            up(c + 1)
        if pending is not None:
            if pending[0] == 0:
                acc_sc[...] = down(*pending)
            else:
                acc_sc[...] += down(*pending)
            pending = None
        act_sc[:, c * tf:(c + 1) * tf] = gate(conv(c, 1), conv(c, 0)).astype(bf16)
        if (c + 1) % down_group == 0 or c + 1 == n_chunks:
            pending, done = (done, c + 1), c + 1

    y = down(*pending)
    if pending[0] != 0:
        y = y + acc_sc[...]
    out_ref[...] = x1_ref[...] + _rms(y) * npost_ref[...]


def _ffn(x1, h2, wup, cw, cb, wdn, npost, *, seq, tm=512, tf=256, down_group=2):
    t = x1.shape[0]
    tok = lambda i: (i, 0)
    const = lambda i: (0, 0)
    kern = functools.partial(_ffn_kernel, blocks_per_seq=seq // tm, tf=tf, down_group=down_group)
    return pl.pallas_call(
        kern,
        grid=(t // tm,),
        in_specs=[
            pl.BlockSpec((tm, D_MODEL), tok),
            pl.BlockSpec((tm, D_MODEL), tok),
            pl.BlockSpec((D_MODEL, 2 * D_FF), const, pipeline_mode=pl.Buffered(1)),
            pl.BlockSpec((CONV_W, 2 * D_FF), const),
            pl.BlockSpec((1, 2 * D_FF), const),
            pl.BlockSpec((D_FF, D_MODEL), const, pipeline_mode=pl.Buffered(1)),
            pl.BlockSpec((1, D_MODEL), const),
        ],
        out_specs=pl.BlockSpec((tm, D_MODEL), tok),
        out_shape=jax.ShapeDtypeStruct((t, D_MODEL), f32),
        scratch_shapes=[pltpu.VMEM((SUBLANES, 2 * D_FF), f32),
                        pltpu.VMEM((4, tm + SUBLANES, tf), f32),
                        pltpu.VMEM((tm, D_FF), bf16),
                        pltpu.VMEM((tm, D_MODEL), f32)],
        compiler_params=pltpu.CompilerParams(
            dimension_semantics=("arbitrary",), vmem_limit_bytes=VMEM_LIMIT),
        name="ffn",
    )(x1, h2, wup, cw, cb, wdn, npost)


def _layer(x2, p, l, *, batch, seq):
    w_in = p["w_in"][l]
    o_i = 2 * ML_QK + ML_V
    o_o = o_i + 2 * ML_HEADS
    o_qf = o_o + ML_V
    o_ff = o_qf + 3 * FOX_W
    o_ga = o_ff + FOX_HEADS
    w_big = jnp.concatenate([
        w_in[:, :o_i],
        w_in[:, o_o:o_qf],
        w_in[:, o_qf:o_ff],
        w_in[:, o_ga:],
    ], axis=1).astype(bf16)
    w_gate = jnp.concatenate([
        w_in[:, o_i:o_o], w_in[:, o_ff:o_ga],
        jnp.zeros((D_MODEL, GATE_W - N_GATES), f32)], axis=1).astype(bf16)
    b_gate = jnp.concatenate([
        p["b_ml_i"][l], p["b_ml_f"][l], p["b_fox_f"][l], jnp.zeros((GATE_W - N_GATES,), f32)])[None, :]
    tri = jnp.tril(jnp.ones((ML_CHUNK, ML_CHUNK), f32)).astype(bf16)
    row = lambda v: v[None, :].astype(f32)

    proj, gcol, grow = _inproj(x2, row(p["norm_mix_pre"][l]), w_big, w_gate, b_gate, tri, seq=seq)
    h_a = _mlstm(proj, gcol, grow, row(p["ml_head_norm"][l]), batch=batch, seq=seq)
    h_b = _fox(proj, gcol, batch=batch, seq=seq)
    x1, h2 = _merge(x2, h_a, h_b, proj,
                    p["w_branch_a"][l].astype(bf16), p["w_branch_b"][l].astype(bf16),
                    p["w_out"][l].astype(bf16), row(p["b_gate_a"][l]), row(p["b_gate_b"][l]),
                    row(p["norm_mix_post"][l]), row(p["norm_ffn_pre"][l]))
    return _ffn(x1, h2, p["w_up"][l].astype(bf16), p["conv_w"][l], row(p["conv_b"][l]),
                p["w_down"][l].astype(bf16), row(p["norm_ffn_post"][l]), seq=seq)


def kernel(x, norm_mix_pre, w_in, b_ml_i, b_ml_f, ml_head_norm, b_fox_f, b_gate_a, b_gate_b,
           w_branch_a, w_branch_b, w_out, norm_mix_post, norm_ffn_pre, w_up, conv_w, conv_b,
           w_down, norm_ffn_post):
    batch, seq, _ = x.shape
    p = dict(norm_mix_pre=norm_mix_pre, w_in=w_in, b_ml_i=b_ml_i, b_ml_f=b_ml_f,
             ml_head_norm=ml_head_norm, b_fox_f=b_fox_f, b_gate_a=b_gate_a, b_gate_b=b_gate_b,
             w_branch_a=w_branch_a, w_branch_b=w_branch_b, w_out=w_out,
             norm_mix_post=norm_mix_post, norm_ffn_pre=norm_ffn_pre, w_up=w_up, conv_w=conv_w,
             conv_b=conv_b, w_down=w_down, norm_ffn_post=norm_ffn_post)
    x2 = x.reshape(batch * seq, D_MODEL)
    for l in range(w_in.shape[0]):
        x2 = _layer(x2, p, l, batch=batch, seq=seq)
    return x2.reshape(batch, seq, D_MODEL)
```

```python
import functools

import jax
import jax.numpy as jnp
from jax import lax
from jax.experimental import pallas as pl
from jax.experimental.pallas import tpu as pltpu

D_MODEL = 1024
ML_HEADS = 4
ML_DQK = 128
ML_DV = 256
ML_QK = ML_HEADS * ML_DQK
ML_V = ML_HEADS * ML_DV
FOX_HEADS = 8
FOX_DH = 128
FOX_W = FOX_HEADS * FOX_DH
D_FF = 2816
CONV_W = 3
GATE_CAP = 15.0
EPS = 1e-6
LOG2E = 1.4426950408889634
GELU_CUBIC = 0.044715
GELU_K1 = -2.0 * (2.0 / 3.141592653589793) ** 0.5 * LOG2E

LANES = 128
SUBLANES = 8
GATE_W = LANES
N_GATES = 2 * ML_HEADS + FOX_HEADS
ML_CHUNK = 128
VMEM_LIMIT = 56 * 1024 * 1024

PROJ_W = 2 * ML_QK + 2 * ML_V + 3 * FOX_W + 2 * D_MODEL
COL_QM, COL_KM, COL_VM, COL_OM = 0, ML_QK, 2 * ML_QK, 2 * ML_QK + ML_V
COL_QF = COL_OM + ML_V
COL_KF = COL_QF + FOX_W
COL_VF = COL_KF + FOX_W
COL_GA = COL_VF + FOX_W
COL_GB = COL_GA + D_MODEL

f32 = jnp.float32
bf16 = jnp.bfloat16


def _log_sigmoid(z):
    return jnp.minimum(z, 0.0) - jnp.log1p(jnp.exp(-jnp.abs(z)))


def _rms(v):
    return v * lax.rsqrt(jnp.mean(v * v, axis=-1, keepdims=True) + EPS)


def _dot(a, b):
    return jnp.dot(a, b, preferred_element_type=f32)


def _dot_nt(a, b):
    return lax.dot_general(a, b, (((1,), (1,)), ((), ())), preferred_element_type=f32)


def _dot_tn(a, b):
    return lax.dot_general(a, b, (((0,), (0,)), ((), ())), preferred_element_type=f32)


def _inproj_kernel(x_ref, nw_ref, w_ref, wg_ref, bg_ref, tri_ref, cs_ref,
                   proj_ref, gcol_ref, grow_ref, h_sc, carry_sc, *, blocks_per_seq, piece):
    i = pl.program_id(0)
    j = pl.program_id(1)
    tm = x_ref.shape[0]

    def project(hb):
        return (_dot(hb, w_ref[...]) * cs_ref[...]).astype(bf16)

    @pl.when(j == 0)
    def _():
        @pl.when(i % blocks_per_seq == 0)
        def _():
            carry_sc[...] = jnp.zeros_like(carry_sc)

        tri = tri_ref[...]
        lane = lax.broadcasted_iota(jnp.int32, (ML_CHUNK, GATE_W), 1)
        per_piece = piece // ML_CHUNK
        for r in range(tm // ML_CHUNK):
            rows = slice(r * ML_CHUNK, (r + 1) * ML_CHUNK)
            if r % per_piece == 0:
                prow = slice(r * ML_CHUNK, r * ML_CHUNK + piece)
                hb = (_rms(x_ref[prow, :]) * nw_ref[...]).astype(bf16)
                h_sc[prow, :] = hb
                proj_ref[prow, :] = project(hb)
                g_piece = _dot(hb, wg_ref[...]) + bg_ref[...]
            g = g_piece[(r % per_piece) * ML_CHUNK:(r % per_piece + 1) * ML_CHUNK]
            cap = GATE_CAP * jnp.tanh(g / GATE_CAP)
            a = jnp.where(lane < ML_HEADS, cap,
                          jnp.where(lane < 2 * ML_HEADS, _log_sigmoid(cap), _log_sigmoid(g)))
            a = jnp.where(lane < N_GATES, a, 0.0)
            a_hi = a.astype(bf16)
            r1 = a - a_hi.astype(f32)
            a_mid = r1.astype(bf16)
            a_lo = (r1 - a_mid.astype(f32)).astype(bf16)
            cs = _dot(tri, a_hi) + _dot(tri, a_mid) + _dot(tri, a_lo)
            glob = cs + carry_sc[...]
            carry_sc[...] = glob[ML_CHUNK - 1:ML_CHUNK, :]
            out = jnp.where(lane < ML_HEADS, a, jnp.where(lane < 2 * ML_HEADS, cs, glob))
            gcol_ref[rows, :] = out
            grow_ref[:, rows] = out.T[:N_GATES, :]

    @pl.when(j != 0)
    def _():
        proj_ref[...] = project(h_sc[...])


def _proj_col_scale():
    cs = jnp.ones((1, PROJ_W), f32)
    cs = cs.at[:, COL_QM:COL_QM + ML_QK].set(ML_DQK ** -0.5)
    return cs.at[:, COL_QF:COL_QF + FOX_W].set(LOG2E * FOX_DH ** -0.5)


def _inproj(x2, nw, w_big, w_gate, b_gate, tri, *, seq, tm=1024, tn=1024, piece=256):
    t = x2.shape[0]
    piece = max(piece, ML_CHUNK)
    kern = functools.partial(_inproj_kernel, blocks_per_seq=seq // tm, piece=piece)
    return pl.pallas_call(
        kern,
        grid=(t // tm, PROJ_W // tn),
        in_specs=[
            pl.BlockSpec((tm, D_MODEL), lambda i, j: (i, 0)),
            pl.BlockSpec((1, D_MODEL), lambda i, j: (0, 0)),
            pl.BlockSpec((D_MODEL, tn), lambda i, j: (0, j)),
            pl.BlockSpec((D_MODEL, GATE_W), lambda i, j: (0, 0)),
            pl.BlockSpec((1, GATE_W), lambda i, j: (0, 0)),
            pl.BlockSpec((ML_CHUNK, ML_CHUNK), lambda i, j: (0, 0)),
            pl.BlockSpec((1, tn), lambda i, j: (0, j)),
        ],
        out_specs=[
            pl.BlockSpec((tm, tn), lambda i, j: (i, j)),
            pl.BlockSpec((tm, GATE_W), lambda i, j: (i, 0)),
            pl.BlockSpec((N_GATES, tm), lambda i, j: (0, i)),
        ],
        out_shape=[
            jax.ShapeDtypeStruct((t, PROJ_W), bf16),
            jax.ShapeDtypeStruct((t, GATE_W), f32),
            jax.ShapeDtypeStruct((N_GATES, t), f32),
        ],
        scratch_shapes=[pltpu.VMEM((tm, D_MODEL), bf16), pltpu.VMEM((1, GATE_W), f32)],
        compiler_params=pltpu.CompilerParams(
            dimension_semantics=("arbitrary", "arbitrary"), vmem_limit_bytes=VMEM_LIMIT),
        name="inproj",
    )(x2, nw, w_big, w_gate, b_gate, tri, _proj_col_scale())


def _mlstm_kernel(q_ref, k_ref, v_ref, o_ref, gcol_ref, grow_ref, hnw_ref, spread_ref, out_ref, s_sc, m_sc):
    c = pl.program_id(1)
    L = ML_CHUNK
    nb = q_ref.shape[0]
    assert L == LANES

    @pl.when(c == 0)
    def _():
        s_sc[...] = jnp.zeros_like(s_sc)
        m_sc[...] = jnp.zeros_like(m_sc)

    row = lax.broadcasted_iota(jnp.int32, (L, L), 0)
    col = lax.broadcasted_iota(jnp.int32, (L, L), 1)
    causal = col <= row
    ones_blk = jnp.ones((L, LANES), bf16)

    def wide(tile, n):
        return jnp.concatenate([tile] * n, axis=1)

    P = [(n, h) for n in range(nb) for h in range(ML_HEADS)]
    C = range(len(P))
    qk_cols = lambda h: slice(h * ML_DQK, (h + 1) * ML_DQK)
    v_cols = lambda h: slice(h * ML_DV, (h + 1) * ML_DV)
    spread = []
    for n in range(nb):
        g = gcol_ref[n]
        g_hi = g.astype(bf16)
        r1 = g - g_hi.astype(f32)
        g_mid = r1.astype(bf16)
        g_lo = (r1 - g_mid.astype(f32)).astype(bf16)
        spread.append(_dot(g_hi, spread_ref[...]) + _dot(g_mid, spread_ref[...]) + _dot(g_lo, spread_ref[...]))
    tile = lambda n, j: spread[n][:, j * LANES:(j + 1) * LANES]
    b_t = [tile(n, h) for n, h in P]
    a_t = [tile(n, ML_HEADS + h) for n, h in P]
    gr = [grow_ref[n] for n in range(nb)]
    a_r = [gr[n][h:h + 1, :] - gr[n][ML_HEADS + h:ML_HEADS + h + 1, :] for n, h in P]
    q = [q_ref[n, :, qk_cols(h)] for n, h in P]
    k = [k_ref[n, :, qk_cols(h)] for n, h in P]
    v_aug = [jnp.concatenate([v_ref[n, :, v_cols(h)], ones_blk], axis=1) for n, h in P]
    m_prev = [m_sc[i:i + 1, :] for i in C]
    state = [s_sc[i] for i in C]

    qk = [_dot_nt(q[i], k[i]) for i in C]
    inter_mm = [_dot(q[i], state[i].astype(bf16)) for i in C]
    amat = [jnp.where(causal, a_r[i], -jnp.inf) for i in C]
    mx = [jnp.maximum(m_prev[i], jnp.broadcast_to(jnp.max(amat[i], axis=-1, keepdims=True), (L, LANES)))
          for i in C]

    mx_last = [mx[i][L - 1:L, :] for i in C]
    b_last = [b_t[i][L - 1:L, :] for i in C]
    wk = [jnp.exp(a_t[i] - mx_last[i]) for i in C]
    upd = [_dot_tn((wk[i] * k[i].astype(f32)).astype(bf16), v_aug[i]) for i in C]

    s = [(qk[i] * jnp.exp(amat[i] - mx[i])).astype(bf16) for i in C]
    w_inter = [jnp.exp(m_prev[i] - mx[i]) for i in C]
    num = [_dot(s[i], v_aug[i]) + wide(w_inter[i], 3) * inter_mm[i] for i in C]
    for i in C:
        s_sc[i] = wide(jnp.exp(m_prev[i] - mx_last[i]), 3) * state[i] + upd[i]
        m_sc[i:i + 1, :] = b_last[i] + mx_last[i]

    inv = [1.0 / jnp.maximum(jnp.abs(num[i][:, ML_DV:]), jnp.exp(-b_t[i] - mx[i])) for i in C]
    msq = [jnp.broadcast_to(jnp.mean(num[i][:, :ML_DV] * num[i][:, :ML_DV], axis=-1, keepdims=True),
                            (L, LANES)) for i in C]
    fac = [inv[i] * lax.rsqrt(inv[i] * inv[i] * msq[i] + EPS) for i in C]
    for i, (n, h) in enumerate(P):
        gate = jax.nn.sigmoid(o_ref[n, :, v_cols(h)].astype(f32)) * hnw_ref[:, v_cols(h)]
        out_ref[n, :, v_cols(h)] = (num[i][:, :ML_DV] * wide(fac[i], ML_DV // LANES) * gate).astype(bf16)


def _mlstm(proj, gcol, grow, hnw, *, batch, seq, nb=2):
    nc = seq // ML_CHUNK
    L = ML_CHUNK
    proj4 = proj.reshape(batch // nb, nb, seq, PROJ_W)
    gcol4 = gcol.reshape(batch // nb, nb, seq, GATE_W)
    grow4 = grow.reshape(N_GATES, batch // nb, nb, seq).transpose(1, 2, 0, 3)
    src = lax.broadcasted_iota(jnp.int32, (GATE_W, 2 * ML_HEADS * LANES), 0)
    dst = lax.broadcasted_iota(jnp.int32, (GATE_W, 2 * ML_HEADS * LANES), 1) // LANES
    spread = (jnp.where(src == ML_HEADS + dst % ML_HEADS, jnp.where(dst < ML_HEADS, 1.0, -1.0), 0.0)
              + jnp.where((dst >= ML_HEADS) & (src == dst - ML_HEADS), 1.0, 0.0)).astype(bf16)
    out = pl.pallas_call(
        _mlstm_kernel,
        grid=(batch // nb, nc),
        in_specs=[
            pl.BlockSpec((None, nb, L, ML_QK), lambda b, c: (b, 0, c, COL_QM // ML_QK)),
            pl.BlockSpec((None, nb, L, ML_QK), lambda b, c: (b, 0, c, COL_KM // ML_QK)),
            pl.BlockSpec((None, nb, L, ML_V), lambda b, c: (b, 0, c, COL_VM // ML_V)),
            pl.BlockSpec((None, nb, L, ML_V), lambda b, c: (b, 0, c, COL_OM // ML_V)),
            pl.BlockSpec((None, nb, L, GATE_W), lambda b, c: (b, 0, c, 0)),
            pl.BlockSpec((None, nb, N_GATES, L), lambda b, c: (b, 0, 0, c)),
            pl.BlockSpec((1, ML_V), lambda b, c: (0, 0)),
            pl.BlockSpec((GATE_W, 2 * ML_HEADS * LANES), lambda b, c: (0, 0)),
        ],
        out_specs=pl.BlockSpec((None, nb, L, ML_V), lambda b, c: (b, 0, c, 0)),
        out_shape=jax.ShapeDtypeStruct((batch // nb, nb, seq, ML_V), bf16),
        scratch_shapes=[pltpu.VMEM((nb * ML_HEADS, ML_DQK, ML_DV + LANES), f32),
                        pltpu.VMEM((nb * ML_HEADS, LANES), f32)],
        compiler_params=pltpu.CompilerParams(
            dimension_semantics=("arbitrary", "arbitrary"), vmem_limit_bytes=VMEM_LIMIT),
        name="mlstm",
    )(proj4, proj4, proj4, proj4, gcol4, grow4, hnw, spread)
    return out.reshape(batch * seq, ML_V)


def _fox_kernel(q_ref, k_ref, v_ref, gcol_ref, out_ref, kaug_sc, st_sc, mx_sc, m_sc, l_sc, acc_sc,
                *, blk, setup_rows, ahead):
    hg = pl.program_id(1)
    qi = pl.program_id(2)
    seq = k_ref.shape[0]
    nh = kaug_sc.shape[0]
    head_cols = [slice(i * FOX_DH, (i + 1) * FOX_DH) for i in range(nh)]

    @pl.when(qi == 0)
    def _():
        r = lax.broadcasted_iota(jnp.int32, (GATE_W, LANES), 0)
        c = lax.broadcasted_iota(jnp.int32, (GATE_W, LANES), 1)
        for n in range(seq // setup_rows):
            rows = pl.ds(n * setup_rows, setup_rows)
            a = gcol_ref[rows, :] * (-LOG2E)
            a_hi = a.astype(bf16)
            r1 = a - a_hi.astype(f32)
            a_mid = r1.astype(bf16)
            a_lo = (r1 - a_mid.astype(f32)).astype(bf16)
            for i in range(nh):
                src = 2 * ML_HEADS + hg * nh + i
                pick = [((r == src) & (c == j)).astype(bf16) for j in range(3)]
                aug = _dot(a_hi, pick[0]) + _dot(a_mid, pick[1]) + _dot(a_lo, pick[2])
                kaug_sc[i, rows, :FOX_DH] = k_ref[rows, head_cols[i]]
                kaug_sc[i, rows, FOX_DH:] = aug.astype(bf16)

    ones3 = (lax.broadcasted_iota(jnp.int32, (blk, LANES), 1) < 3).astype(bf16)
    q_aug = [jnp.concatenate([q_ref[:, head_cols[i]], ones3], axis=1) for i in range(nh)]

    m_sc[...] = jnp.full(m_sc.shape, -jnp.inf, f32)
    l_sc[...] = jnp.zeros(l_sc.shape, f32)
    acc_sc[...] = jnp.zeros(acc_sc.shape, f32)

    def logits_stage(i, j, masked):
        start = pl.multiple_of(j * blk, blk)
        st = _dot_nt(kaug_sc[i, pl.ds(start, blk), :], q_aug[i])
        if masked:
            row = lax.broadcasted_iota(jnp.int32, (blk, blk), 0)
            col = lax.broadcasted_iota(jnp.int32, (blk, blk), 1)
            st = jnp.where(row <= col, st, -jnp.inf)
        st_sc[i] = st
        mx_sc[i] = jnp.max(st, axis=0, keepdims=True)

    def softmax_stage(i, j):
        start = pl.multiple_of(j * blk, blk)
        m = m_sc[i]
        m_new = jnp.maximum(m, mx_sc[i])
        alpha = jnp.exp2(m - m_new)
        p = jnp.exp2(st_sc[i] - m_new)
        l_sc[i] = alpha * l_sc[i] + jnp.sum(p, axis=0, keepdims=True)
        acc_sc[i] = alpha * acc_sc[i] + _dot_tn(v_ref[pl.ds(start, blk), head_cols[i]], p.astype(bf16))
        m_sc[i] = m_new

    def round_robin(cur, nxt, masked):
        for i in range(nh):
            if i + ahead < nh:
                logits_stage(i + ahead, cur, masked)
            elif nxt is not None:
                logits_stage(i + ahead - nh, nxt, False)
            softmax_stage(i, cur)

    for i in range(ahead):
        logits_stage(i, qi, True)

    @pl.when(qi == 0)
    def _():
        round_robin(qi, None, True)

    @pl.when(qi > 0)
    def _():
        round_robin(qi, 0, True)

        def body(k, _):
            round_robin(k - 1, k, False)
            return 0

        lax.fori_loop(1, qi, body, 0)
        round_robin(qi - 1, None, False)

    for i in range(nh):
        out_ref[:, head_cols[i]] = (acc_sc[i] * (1.0 / l_sc[i])).T.astype(bf16)


def _fox(proj, gcol, *, batch, seq, blk=512, nh=4, ahead=1):
    assert 1 <= ahead < nh
    t = proj.shape[0]
    nq = seq // blk
    w = nh * FOX_DH
    kern = functools.partial(_fox_kernel, blk=blk, setup_rows=min(seq, 512), ahead=ahead)
    return pl.pallas_call(
        kern,
        grid=(batch, FOX_HEADS // nh, nq),
        in_specs=[
            pl.BlockSpec((blk, w), lambda b, h, i: (b * nq + i, COL_QF // w + h)),
            pl.BlockSpec((seq, w), lambda b, h, i: (b, COL_KF // w + h)),
            pl.BlockSpec((seq, w), lambda b, h, i: (b, COL_VF // w + h)),
            pl.BlockSpec((seq, GATE_W), lambda b, h, i: (b, 0)),
        ],
        out_specs=pl.BlockSpec((blk, w), lambda b, h, i: (b * nq + i, h)),
        out_shape=jax.ShapeDtypeStruct((t, FOX_W), bf16),
        scratch_shapes=[pltpu.VMEM((nh, seq, FOX_DH + LANES), bf16),
                        pltpu.VMEM((nh, blk, blk), f32),
                        pltpu.VMEM((nh, 1, blk), f32),
                        pltpu.VMEM((nh, 1, blk), f32),
                        pltpu.VMEM((nh, 1, blk), f32),
                        pltpu.VMEM((nh, FOX_DH, blk), f32)],
        compiler_params=pltpu.CompilerParams(
            dimension_semantics=("arbitrary", "arbitrary", "arbitrary"), vmem_limit_bytes=VMEM_LIMIT),
        name="fox",
    )(proj, proj, proj, gcol)


def _merge_kernel(x_ref, ha_ref, hb_ref, ga_ref, gb_ref, wa_ref, wb_ref, wo_ref,
                  bga_ref, bgb_ref, npost_ref, npre_ref, x1_ref, h2_ref, *, sub):
    n_sub = x_ref.shape[0] // sub
    rows = [slice(r * sub, (r + 1) * sub) for r in range(n_sub)]

    def branches(r):
        return _dot(ha_ref[rows[r], :], wa_ref[...]), _dot(hb_ref[rows[r], :], wb_ref[...])

    def finish(r, z):
        x1 = x_ref[rows[r], :] + _rms(z) * npost_ref[...]
        x1_ref[rows[r], :] = x1
        h2_ref[rows[r], :] = (_rms(x1) * npre_ref[...]).astype(bf16)

    y_next = branches(0)
    z_prev = None
    for r in range(n_sub):
        ya, yb = y_next
        if r + 1 < n_sub:
            y_next = branches(r + 1)
        merged = (jax.nn.sigmoid(ga_ref[rows[r], :].astype(f32) + bga_ref[...]) * ya
                  + jax.nn.sigmoid(gb_ref[rows[r], :].astype(f32) + bgb_ref[...]) * yb)
        z = _dot(merged.astype(bf16), wo_ref[...])
        if z_prev is not None:
            finish(r - 1, z_prev)
        z_prev = z
    finish(n_sub - 1, z_prev)


def _merge(x2, h_a, h_b, proj, wa, wb, wo, bga, bgb, npost, npre, *, tm=1024, sub=256):
    t = x2.shape[0]
    tm = min(tm, t)
    tok = lambda i: (i, 0)
    const = lambda i: (0, 0)
    wspec = pl.BlockSpec((D_MODEL, D_MODEL), const, pipeline_mode=pl.Buffered(1))
    vspec = pl.BlockSpec((1, D_MODEL), const)
    return pl.pallas_call(
        functools.partial(_merge_kernel, sub=sub),
        grid=(t // tm,),
        in_specs=[
            pl.BlockSpec((tm, D_MODEL), tok),
            pl.BlockSpec((tm, ML_V), tok),
            pl.BlockSpec((tm, FOX_W), tok),
            pl.BlockSpec((tm, D_MODEL), lambda i: (i, COL_GA // D_MODEL)),
            pl.BlockSpec((tm, D_MODEL), lambda i: (i, COL_GB // D_MODEL)),
            wspec, wspec, wspec, vspec, vspec, vspec, vspec,
        ],
        out_specs=[pl.BlockSpec((tm, D_MODEL), tok), pl.BlockSpec((tm, D_MODEL), tok)],
        out_shape=[jax.ShapeDtypeStruct((t, D_MODEL), f32), jax.ShapeDtypeStruct((t, D_MODEL), bf16)],
        compiler_params=pltpu.CompilerParams(
            dimension_semantics=("arbitrary",), vmem_limit_bytes=VMEM_LIMIT),
        name="merge",
    )(x2, h_a, h_b, proj, proj, wa, wb, wo, bga, bgb, npost, npre)


def _ffn_kernel(x1_ref, h2_ref, wup_ref, cw_ref, cb_ref, wdn_ref, npost_ref, out_ref,
                halo_sc, ubuf_sc, act_sc, acc_sc, *, blocks_per_seq, tf, down_group):
    i = pl.program_id(0)
    tm = x1_ref.shape[0]

    @pl.when(i % blocks_per_seq == 0)
    def _():
        halo_sc[...] = jnp.zeros_like(halo_sc)

    h2 = h2_ref[...]
    n_chunks = D_FF // tf

    def up(c):
        for half in range(2):
            col0 = half * D_FF + c * tf
            slot = 2 * (c % 2) + half
            u = _dot(h2, wup_ref[:, col0:col0 + tf])
            ubuf_sc[slot, 0:SUBLANES, :] = halo_sc[:, col0:col0 + tf]
            ubuf_sc[slot, SUBLANES:SUBLANES + tm, :] = u
            halo_sc[:, col0:col0 + tf] = u[tm - SUBLANES:tm, :]

    def conv(c, half):
        col0 = half * D_FF + c * tf
        slot = 2 * (c % 2) + half
        w = cw_ref[:, col0:col0 + tf]
        y = cb_ref[:, col0:col0 + tf]
        for j in range(CONV_W):
            first = SUBLANES - (CONV_W - 1 - j)
            y = y + w[j:j + 1] * ubuf_sc[slot, first:first + tm, :]
        return y

    def gate(g, a):
        inner = g * (GELU_K1 + (GELU_K1 * GELU_CUBIC) * (g * g))
        return (g * a) * (1.0 / (1.0 + jnp.exp2(inner)))

    def down(c0, c1):
        return _dot(act_sc[:, c0 * tf:c1 * tf], wdn_ref[c0 * tf:c1 * tf, :])

    up(0)
    pending, done = None, 0
    for c in range(n_chunks):
        if c + 1 < n_chunks:
            up(c + 1)
        if pending is not None:
            if pending[0] == 0:
                acc_sc[...] = down(*pending)
            else:
                acc_sc[...] += down(*pending)
            pending = None
        act_sc[:, c * tf:(c + 1) * tf] = gate(conv(c, 1), conv(c, 0)).astype(bf16)
        if (c + 1) % down_group == 0 or c + 1 == n_chunks:
            pending, done = (done, c + 1), c + 1

    y = down(*pending)
    if pending[0] != 0:
        y = y + acc_sc[...]
    out_ref[...] = x1_ref[...] + _rms(y) * npost_ref[...]


def _ffn(x1, h2, wup, cw, cb, wdn, npost, *, seq, tm=512, tf=256, down_group=2):
    t = x1.shape[0]
    tok = lambda i: (i, 0)
    const = lambda i: (0, 0)
    kern = functools.partial(_ffn_kernel, blocks_per_seq=seq // tm, tf=tf, down_group=down_group)
    return pl.pallas_call(
        kern,
        grid=(t // tm,),
        in_specs=[
            pl.BlockSpec((tm, D_MODEL), tok),
            pl.BlockSpec((tm, D_MODEL), tok),
            pl.BlockSpec((D_MODEL, 2 * D_FF), const, pipeline_mode=pl.Buffered(1)),
            pl.BlockSpec((CONV_W, 2 * D_FF), const),
            pl.BlockSpec((1, 2 * D_FF), const),
            pl.BlockSpec((D_FF, D_MODEL), const, pipeline_mode=pl.Buffered(1)),
            pl.BlockSpec((1, D_MODEL), const),
        ],
        out_specs=pl.BlockSpec((tm, D_MODEL), tok),
        out_shape=jax.ShapeDtypeStruct((t, D_MODEL), f32),
        scratch_shapes=[pltpu.VMEM((SUBLANES, 2 * D_FF), f32),
                        pltpu.VMEM((4, tm + SUBLANES, tf), f32),
                        pltpu.VMEM((tm, D_FF), bf16),
                        pltpu.VMEM((tm, D_MODEL), f32)],
        compiler_params=pltpu.CompilerParams(
            dimension_semantics=("arbitrary",), vmem_limit_bytes=VMEM_LIMIT),
        name="ffn",
    )(x1, h2, wup, cw, cb, wdn, npost)


def _layer(x2, p, l, *, batch, seq):
    w_in = p["w_in"][l]
    o_i = 2 * ML_QK + ML_V
    o_o = o_i + 2 * ML_HEADS
    o_qf = o_o + ML_V
    o_ff = o_qf + 3 * FOX_W
    o_ga = o_ff + FOX_HEADS
    w_big = jnp.concatenate([
        w_in[:, :o_i],
        w_in[:, o_o:o_qf],
        w_in[:, o_qf:o_ff],
        w_in[:, o_ga:],
    ], axis=1).astype(bf16)
    w_gate = jnp.concatenate([
        w_in[:, o_i:o_o], w_in[:, o_ff:o_ga],
        jnp.zeros((D_MODEL, GATE_W - N_GATES), f32)], axis=1).astype(bf16)
    b_gate = jnp.concatenate([
        p["b_ml_i"][l], p["b_ml_f"][l], p["b_fox_f"][l], jnp.zeros((GATE_W - N_GATES,), f32)])[None, :]
    tri = jnp.tril(jnp.ones((ML_CHUNK, ML_CHUNK), f32)).astype(bf16)
    row = lambda v: v[None, :].astype(f32)

    proj, gcol, grow = _inproj(x2, row(p["norm_mix_pre"][l]), w_big, w_gate, b_gate, tri, seq=seq)
    h_a = _mlstm(proj, gcol, grow, row(p["ml_head_norm"][l]), batch=batch, seq=seq)
    h_b = _fox(proj, gcol, batch=batch, seq=seq)
    x1, h2 = _merge(x2, h_a, h_b, proj,
                    p["w_branch_a"][l].astype(bf16), p["w_branch_b"][l].astype(bf16),
                    p["w_out"][l].astype(bf16), row(p["b_gate_a"][l]), row(p["b_gate_b"][l]),
                    row(p["norm_mix_post"][l]), row(p["norm_ffn_pre"][l]))
    return _ffn(x1, h2, p["w_up"][l].astype(bf16), p["conv_w"][l], row(p["conv_b"][l]),
                p["w_down"][l].astype(bf16), row(p["norm_ffn_post"][l]), seq=seq)


def kernel(x, norm_mix_pre, w_in, b_ml_i, b_ml_f, ml_head_norm, b_fox_f, b_gate_a, b_gate_b,
           w_branch_a, w_branch_b, w_out, norm_mix_post, norm_ffn_pre, w_up, conv_w, conv_b,
           w_down, norm_ffn_post):
    batch, seq, _ = x.shape
    p = dict(norm_mix_pre=norm_mix_pre, w_in=w_in, b_ml_i=b_ml_i, b_ml_f=b_ml_f,
             ml_head_norm=ml_head_norm, b_fox_f=b_fox_f, b_gate_a=b_gate_a, b_gate_b=b_gate_b,
             w_branch_a=w_branch_a, w_branch_b=w_branch_b, w_out=w_out,
             norm_mix_post=norm_mix_post, norm_ffn_pre=norm_ffn_pre, w_up=w_up, conv_w=conv_w,
             conv_b=conv_b, w_down=w_down, norm_ffn_post=norm_ffn_post)
    x2 = x.reshape(batch * seq, D_MODEL)
    for l in range(w_in.shape[0]):
        x2 = _layer(x2, p, l, batch=batch, seq=seq)
    return x2.reshape(batch, seq, D_MODEL)
```

```python
import functools

import jax
import jax.numpy as jnp
from jax import lax
from jax.experimental import pallas as pl
from jax.experimental.pallas import tpu as pltpu

D_MODEL = 1024
ML_HEADS = 4
ML_DQK = 128
ML_DV = 256
ML_QK = ML_HEADS * ML_DQK
ML_V = ML_HEADS * ML_DV
FOX_HEADS = 8
FOX_DH = 128
FOX_W = FOX_HEADS * FOX_DH
D_FF = 2816
CONV_W = 3
GATE_CAP = 15.0
EPS = 1e-6
LOG2E = 1.4426950408889634
GELU_CUBIC = 0.044715
GELU_K1 = -2.0 * (2.0 / 3.141592653589793) ** 0.5 * LOG2E

LANES = 128
SUBLANES = 8
GATE_W = LANES
N_GATES = 2 * ML_HEADS + FOX_HEADS
ML_CHUNK = 128
VMEM_LIMIT = 56 * 1024 * 1024

PROJ_W = 2 * ML_QK + 2 * ML_V + 3 * FOX_W + 2 * D_MODEL
COL_QM, COL_KM, COL_VM, COL_OM = 0, ML_QK, 2 * ML_QK, 2 * ML_QK + ML_V
COL_QF = COL_OM + ML_V
COL_KF = COL_QF + FOX_W
COL_VF = COL_KF + FOX_W
COL_GA = COL_VF + FOX_W
COL_GB = COL_GA + D_MODEL

f32 = jnp.float32
bf16 = jnp.bfloat16


def _log_sigmoid(z):
    return jnp.minimum(z, 0.0) - jnp.log1p(jnp.exp(-jnp.abs(z)))


def _rms(v):
    return v * lax.rsqrt(jnp.mean(v * v, axis=-1, keepdims=True) + EPS)


def _dot(a, b):
    return jnp.dot(a, b, preferred_element_type=f32)


def _dot_nt(a, b):
    return lax.dot_general(a, b, (((1,), (1,)), ((), ())), preferred_element_type=f32)


def _dot_tn(a, b):
    return lax.dot_general(a, b, (((0,), (0,)), ((), ())), preferred_element_type=f32)


def _inproj_kernel(*refs, blocks_per_seq, piece, w_blocks):
    x_ref, nw_ref, wg_ref, bg_ref, tri_ref, cs_ref = refs[:6]
    w_refs = refs[6:6 + len(w_blocks)]
    proj_ref, gcol_ref, grow_ref, h_sc, carry_sc = refs[6 + len(w_blocks):]
    assert w_blocks[0][0] == 0
    i = pl.program_id(0)
    j = pl.program_id(1)
    tm = x_ref.shape[0]

    def project(hb, w_ref=w_refs[0]):
        return (_dot(hb, w_ref[...]) * cs_ref[...]).astype(bf16)

    @pl.when(j == 0)
    def _():
        @pl.when(i % blocks_per_seq == 0)
        def _():
            carry_sc[...] = jnp.zeros_like(carry_sc)

        tri = tri_ref[...]
        lane = lax.broadcasted_iota(jnp.int32, (ML_CHUNK, GATE_W), 1)
        per_piece = piece // ML_CHUNK

        def normed(p):
            prow = slice(p * piece, (p + 1) * piece)
            hb = (_rms(x_ref[prow, :]) * nw_ref[...]).astype(bf16)
            h_sc[prow, :] = hb
            return hb

        acts = []
        hb = normed(0)
        for p in range(tm // piece):
            proj_ref[p * piece:(p + 1) * piece, :] = project(hb)
            g_piece = _dot(hb, wg_ref[...]) + bg_ref[...]
            if (p + 1) * piece < tm:
                hb = normed(p + 1)
            for q in range(per_piece):
                g = g_piece[q * ML_CHUNK:(q + 1) * ML_CHUNK]
                cap = GATE_CAP * jnp.tanh(g / GATE_CAP)
                a = jnp.where(lane < ML_HEADS, cap,
                              jnp.where(lane < 2 * ML_HEADS, _log_sigmoid(cap), _log_sigmoid(g)))
                acts.append(jnp.where(lane < N_GATES, a, 0.0))

        for r in range(tm // ML_CHUNK):
            rows = slice(r * ML_CHUNK, (r + 1) * ML_CHUNK)
            a = acts[r]
            a_hi = a.astype(bf16)
            r1 = a - a_hi.astype(f32)
            a_mid = r1.astype(bf16)
            a_lo = (r1 - a_mid.astype(f32)).astype(bf16)
            cs = _dot(tri, a_hi) + _dot(tri, a_mid) + _dot(tri, a_lo)
            glob = cs + carry_sc[...]
            carry_sc[...] = glob[ML_CHUNK - 1:ML_CHUNK, :]
            out = jnp.where(lane < ML_HEADS, a, jnp.where(lane < 2 * ML_HEADS, cs, glob))
            gcol_ref[rows, :] = out
            grow_ref[:, rows] = out.T[:N_GATES, :]

    for w_ref, (first, count) in zip(w_refs, w_blocks):
        @pl.when((j >= max(first, 1)) & (j < first + count))
        def _(w_ref=w_ref):
            proj_ref[...] = project(h_sc[...], w_ref)


def _proj_col_scale():
    cs = jnp.ones((1, PROJ_W), f32)
    cs = cs.at[:, COL_QM:COL_QM + ML_QK].set(ML_DQK ** -0.5)
    return cs.at[:, COL_QF:COL_QF + FOX_W].set(LOG2E * FOX_DH ** -0.5)


def _inproj(x2, nw, w_pieces, w_gate, b_gate, tri, *, seq, tm=1024, tn=1024, piece=256):
    t = x2.shape[0]
    piece = max(piece, ML_CHUNK)
    w_blocks, first = [], 0
    for w in w_pieces:
        w_blocks.append((first, w.shape[1] // tn))
        first += w.shape[1] // tn
    assert first == PROJ_W // tn
    kern = functools.partial(_inproj_kernel, blocks_per_seq=seq // tm, piece=piece, w_blocks=tuple(w_blocks))

    def w_spec(first, count):
        return pl.BlockSpec((D_MODEL, tn), lambda i, j: (0, jnp.clip(j - first, 0, count - 1)))

    return pl.pallas_call(
        kern,
        grid=(t // tm, PROJ_W // tn),
        in_specs=[
            pl.BlockSpec((tm, D_MODEL), lambda i, j: (i, 0)),
            pl.BlockSpec((1, D_MODEL), lambda i, j: (0, 0)),
            pl.BlockSpec((D_MODEL, GATE_W), lambda i, j: (0, 0)),
            pl.BlockSpec((1, GATE_W), lambda i, j: (0, 0)),
            pl.BlockSpec((ML_CHUNK, ML_CHUNK), lambda i, j: (0, 0)),
            pl.BlockSpec((1, tn), lambda i, j: (0, j)),
        ] + [w_spec(f, n) for f, n in w_blocks],
        out_specs=[
            pl.BlockSpec((tm, tn), lambda i, j: (i, j)),
            pl.BlockSpec((tm, GATE_W), lambda i, j: (i, 0)),
            pl.BlockSpec((N_GATES, tm), lambda i, j: (0, i)),
        ],
        out_shape=[
            jax.ShapeDtypeStruct((t, PROJ_W), bf16),
            jax.ShapeDtypeStruct((t, GATE_W), f32),
            jax.ShapeDtypeStruct((N_GATES, t), f32),
        ],
        scratch_shapes=[pltpu.VMEM((tm, D_MODEL), bf16), pltpu.VMEM((1, GATE_W), f32)],
        compiler_params=pltpu.CompilerParams(
            dimension_semantics=("arbitrary", "arbitrary"), vmem_limit_bytes=VMEM_LIMIT),
        name="inproj",
    )(x2, nw, w_gate, b_gate, tri, _proj_col_scale(), *w_pieces)


def _mlstm_kernel(q_ref, k_ref, v_ref, o_ref, gcol_ref, grow_ref, hnw_ref, spread_ref, out_ref, s_sc, m_sc):
    c = pl.program_id(1)
    L = ML_CHUNK
    nb = q_ref.shape[0]
    assert L == LANES

    @pl.when(c == 0)
    def _():
        s_sc[...] = jnp.zeros_like(s_sc)
        m_sc[...] = jnp.zeros_like(m_sc)

    row = lax.broadcasted_iota(jnp.int32, (L, L), 0)
    col = lax.broadcasted_iota(jnp.int32, (L, L), 1)
    causal = col <= row
    ones_blk = jnp.ones((L, LANES), bf16)

    def wide(tile, n):
        return jnp.concatenate([tile] * n, axis=1)

    P = [(n, h) for n in range(nb) for h in range(ML_HEADS)]
    C = range(len(P))
    qk_cols = lambda h: slice(h * ML_DQK, (h + 1) * ML_DQK)
    v_cols = lambda h: slice(h * ML_DV, (h + 1) * ML_DV)
    spread = []
    for n in range(nb):
        g = gcol_ref[n]
        g_hi = g.astype(bf16)
        r1 = g - g_hi.astype(f32)
        g_mid = r1.astype(bf16)
        g_lo = (r1 - g_mid.astype(f32)).astype(bf16)
        spread.append(_dot(g_hi, spread_ref[...]) + _dot(g_mid, spread_ref[...]) + _dot(g_lo, spread_ref[...]))
    tile = lambda n, j: spread[n][:, j * LANES:(j + 1) * LANES]
    b_t = [tile(n, h) for n, h in P]
    a_t = [tile(n, ML_HEADS + h) for n, h in P]
    gr = [grow_ref[n] for n in range(nb)]
    a_r = [gr[n][h:h + 1, :] - gr[n][ML_HEADS + h:ML_HEADS + h + 1, :] for n, h in P]
    q = [q_ref[n, :, qk_cols(h)] for n, h in P]
    k = [k_ref[n, :, qk_cols(h)] for n, h in P]
    v_aug = [jnp.concatenate([v_ref[n, :, v_cols(h)], ones_blk], axis=1) for n, h in P]
    m_prev = [m_sc[i:i + 1, :] for i in C]
    state = [s_sc[i] for i in C]

    qk = [_dot_nt(q[i], k[i]) for i in C]
    inter_mm = [_dot(q[i], state[i].astype(bf16)) for i in C]
    amat = [jnp.where(causal, a_r[i], -jnp.inf) for i in C]
    mx = [jnp.maximum(m_prev[i], jnp.broadcast_to(jnp.max(amat[i], axis=-1, keepdims=True), (L, LANES)))
          for i in C]

    mx_last = [mx[i][L - 1:L, :] for i in C]
    b_last = [b_t[i][L - 1:L, :] for i in C]
    wk = [jnp.exp(a_t[i] - mx_last[i]) for i in C]
    upd = [_dot_tn((wk[i] * k[i].astype(f32)).astype(bf16), v_aug[i]) for i in C]

    s = [(qk[i] * jnp.exp(amat[i] - mx[i])).astype(bf16) for i in C]
    w_inter = [jnp.exp(m_prev[i] - mx[i]) for i in C]
    num = [_dot(s[i], v_aug[i]) + wide(w_inter[i], 3) * inter_mm[i] for i in C]
    for i in C:
        s_sc[i] = wide(jnp.exp(m_prev[i] - mx_last[i]), 3) * state[i] + upd[i]
        m_sc[i:i + 1, :] = b_last[i] + mx_last[i]

    inv = [1.0 / jnp.maximum(jnp.abs(num[i][:, ML_DV:]), jnp.exp(-b_t[i] - mx[i])) for i in C]
    msq = [jnp.broadcast_to(jnp.mean(num[i][:, :ML_DV] * num[i][:, :ML_DV], axis=-1, keepdims=True),
                            (L, LANES)) for i in C]
    fac = [inv[i] * lax.rsqrt(inv[i] * inv[i] * msq[i] + EPS) for i in C]
    for i, (n, h) in enumerate(P):
        gate = jax.nn.sigmoid(o_ref[n, :, v_cols(h)].astype(f32)) * hnw_ref[:, v_cols(h)]
        out_ref[n, :, v_cols(h)] = (num[i][:, :ML_DV] * wide(fac[i], ML_DV // LANES) * gate).astype(bf16)


def _mlstm(proj, gcol, grow, hnw, *, batch, seq, nb=2):
    nc = seq // ML_CHUNK
    L = ML_CHUNK
    proj4 = proj.reshape(batch // nb, nb, seq, PROJ_W)
    gcol4 = gcol.reshape(batch // nb, nb, seq, GATE_W)
    grow4 = grow.reshape(N_GATES, batch // nb, nb, seq).transpose(1, 2, 0, 3)
    src = lax.broadcasted_iota(jnp.int32, (GATE_W, 2 * ML_HEADS * LANES), 0)
    dst = lax.broadcasted_iota(jnp.int32, (GATE_W, 2 * ML_HEADS * LANES), 1) // LANES
    spread = (jnp.where(src == ML_HEADS + dst % ML_HEADS, jnp.where(dst < ML_HEADS, 1.0, -1.0), 0.0)
              + jnp.where((dst >= ML_HEADS) & (src == dst - ML_HEADS), 1.0, 0.0)).astype(bf16)
    out = pl.pallas_call(
        _mlstm_kernel,
        grid=(batch // nb, nc),
        in_specs=[
            pl.BlockSpec((None, nb, L, ML_QK), lambda b, c: (b, 0, c, COL_QM // ML_QK)),
            pl.BlockSpec((None, nb, L, ML_QK), lambda b, c: (b, 0, c, COL_KM // ML_QK)),
            pl.BlockSpec((None, nb, L, ML_V), lambda b, c: (b, 0, c, COL_VM // ML_V)),
            pl.BlockSpec((None, nb, L, ML_V), lambda b, c: (b, 0, c, COL_OM // ML_V)),
            pl.BlockSpec((None, nb, L, GATE_W), lambda b, c: (b, 0, c, 0)),
            pl.BlockSpec((None, nb, N_GATES, L), lambda b, c: (b, 0, 0, c)),
            pl.BlockSpec((1, ML_V), lambda b, c: (0, 0)),
            pl.BlockSpec((GATE_W, 2 * ML_HEADS * LANES), lambda b, c: (0, 0)),
        ],
        out_specs=pl.BlockSpec((None, nb, L, ML_V), lambda b, c: (b, 0, c, 0)),
        out_shape=jax.ShapeDtypeStruct((batch // nb, nb, seq, ML_V), bf16),
        scratch_shapes=[pltpu.VMEM((nb * ML_HEADS, ML_DQK, ML_DV + LANES), f32),
                        pltpu.VMEM((nb * ML_HEADS, LANES), f32)],
        compiler_params=pltpu.CompilerParams(
            dimension_semantics=("arbitrary", "arbitrary"), vmem_limit_bytes=VMEM_LIMIT),
        name="mlstm",
    )(proj4, proj4, proj4, proj4, gcol4, grow4, hnw, spread)
    return out.reshape(batch * seq, ML_V)


def _fox_kernel(q_ref, k_ref, v_ref, gcol_ref, out_ref, kaug_sc, st_sc, mx_sc, m_sc, l_sc, acc_sc,
                *, blk, setup_rows, ahead):
    hg = pl.program_id(1)
    qi = pl.program_id(2)
    seq = k_ref.shape[0]
    nh = kaug_sc.shape[0]
    head_cols = [slice(i * FOX_DH, (i + 1) * FOX_DH) for i in range(nh)]

    @pl.when(qi == 0)
    def _():
        r = lax.broadcasted_iota(jnp.int32, (GATE_W, LANES), 0)
        c = lax.broadcasted_iota(jnp.int32, (GATE_W, LANES), 1)
        for n in range(seq // setup_rows):
            rows = pl.ds(n * setup_rows, setup_rows)
            a = gcol_ref[rows, :] * (-LOG2E)
            a_hi = a.astype(bf16)
            r1 = a - a_hi.astype(f32)
            a_mid = r1.astype(bf16)
            a_lo = (r1 - a_mid.astype(f32)).astype(bf16)
            for i in range(nh):
                src = 2 * ML_HEADS + hg * nh + i
                pick = [((r == src) & (c == j)).astype(bf16) for j in range(3)]
                aug = _dot(a_hi, pick[0]) + _dot(a_mid, pick[1]) + _dot(a_lo, pick[2])
                kaug_sc[i, rows, :FOX_DH] = k_ref[rows, head_cols[i]]
                kaug_sc[i, rows, FOX_DH:] = aug.astype(bf16)

    ones3 = (lax.broadcasted_iota(jnp.int32, (blk, LANES), 1) < 3).astype(bf16)
    q_aug = [jnp.concatenate([q_ref[:, head_cols[i]], ones3], axis=1) for i in range(nh)]

    m_sc[...] = jnp.full(m_sc.shape, -jnp.inf, f32)
    l_sc[...] = jnp.zeros(l_sc.shape, f32)
    acc_sc[...] = jnp.zeros(acc_sc.shape, f32)

    def logits_stage(i, j, masked):
        start = pl.multiple_of(j * blk, blk)
        st = _dot_nt(kaug_sc[i, pl.ds(start, blk), :], q_aug[i])
        if masked:
            row = lax.broadcasted_iota(jnp.int32, (blk, blk), 0)
            col = lax.broadcasted_iota(jnp.int32, (blk, blk), 1)
            st = jnp.where(row <= col, st, -jnp.inf)
        st_sc[i] = st
        mx_sc[i] = jnp.max(st, axis=0, keepdims=True)

    def softmax_stage(i, j):
        start = pl.multiple_of(j * blk, blk)
        m = m_sc[i]
        m_new = jnp.maximum(m, mx_sc[i])
        alpha = jnp.exp2(m - m_new)
        p = jnp.exp2(st_sc[i] - m_new)
        l_sc[i] = alpha * l_sc[i] + jnp.sum(p, axis=0, keepdims=True)
        acc_sc[i] = alpha * acc_sc[i] + _dot_tn(v_ref[pl.ds(start, blk), head_cols[i]], p.astype(bf16))
        m_sc[i] = m_new

    def round_robin(cur, nxt, masked):
        for i in range(nh):
            if i + ahead < nh:
                logits_stage(i + ahead, cur, masked)
            elif nxt is not None:
                logits_stage(i + ahead - nh, nxt, False)
            softmax_stage(i, cur)

    for i in range(ahead):
        logits_stage(i, qi, True)
    round_robin(qi, 0, True)

    def body(k, _):
        round_robin(k - 1, jnp.minimum(k, qi - 1), False)
        return 0

    lax.fori_loop(1, qi + 1, body, 0)

    for i in range(nh):
        out_ref[:, head_cols[i]] = (acc_sc[i] * (1.0 / l_sc[i])).T.astype(bf16)


def _fox(proj, gcol, *, batch, seq, blk=512, nh=4, ahead=1):
    assert 1 <= ahead < nh
    t = proj.shape[0]
    nq = seq // blk
    w = nh * FOX_DH
    kern = functools.partial(_fox_kernel, blk=blk, setup_rows=min(seq, 512), ahead=ahead)
    return pl.pallas_call(
        kern,
        grid=(batch, FOX_HEADS // nh, nq),
        in_specs=[
            pl.BlockSpec((blk, w), lambda b, h, i: (b * nq + i, COL_QF // w + h)),
            pl.BlockSpec((seq, w), lambda b, h, i: (b, COL_KF // w + h)),
            pl.BlockSpec((seq, w), lambda b, h, i: (b, COL_VF // w + h)),
            pl.BlockSpec((seq, GATE_W), lambda b, h, i: (b, 0)),
        ],
        out_specs=pl.BlockSpec((blk, w), lambda b, h, i: (b * nq + i, h)),
        out_shape=jax.ShapeDtypeStruct((t, FOX_W), bf16),
        scratch_shapes=[pltpu.VMEM((nh, seq, FOX_DH + LANES), bf16),
                        pltpu.VMEM((nh, blk, blk), f32),
                        pltpu.VMEM((nh, 1, blk), f32),
                        pltpu.VMEM((nh, 1, blk), f32),
                        pltpu.VMEM((nh, 1, blk), f32),
                        pltpu.VMEM((nh, FOX_DH, blk), f32)],
        compiler_params=pltpu.CompilerParams(
            dimension_semantics=("arbitrary", "arbitrary", "arbitrary"), vmem_limit_bytes=VMEM_LIMIT),
        name="fox",
    )(proj, proj, proj, gcol)


def _merge_kernel(x_ref, ha_ref, hb_ref, ga_ref, gb_ref, wa_ref, wb_ref, wo_ref,
                  bga_ref, bgb_ref, npost_ref, npre_ref, x1_ref, h2_ref, *, sub):
    n_sub = x_ref.shape[0] // sub
    rows = [slice(r * sub, (r + 1) * sub) for r in range(n_sub)]

    def branches(r):
        return _dot(ha_ref[rows[r], :], wa_ref[...]), _dot(hb_ref[rows[r], :], wb_ref[...])

    def finish(r, z):
        x1 = x_ref[rows[r], :] + _rms(z) * npost_ref[...]
        x1_ref[rows[r], :] = x1
        h2_ref[rows[r], :] = (_rms(x1) * npre_ref[...]).astype(bf16)

    y_next = branches(0)
    z_prev = None
    for r in range(n_sub):
        ya, yb = y_next
        if r + 1 < n_sub:
            y_next = branches(r + 1)
        merged = (jax.nn.sigmoid(ga_ref[rows[r], :].astype(f32) + bga_ref[...]) * ya
                  + jax.nn.sigmoid(gb_ref[rows[r], :].astype(f32) + bgb_ref[...]) * yb)
        z = _dot(merged.astype(bf16), wo_ref[...])
        if z_prev is not None:
            finish(r - 1, z_prev)
        z_prev = z
    finish(n_sub - 1, z_prev)


def _merge(x2, h_a, h_b, proj, wa, wb, wo, bga, bgb, npost, npre, *, tm=1024, sub=256):
    t = x2.shape[0]
    tm = min(tm, t)
    tok = lambda i: (i, 0)
    const = lambda i: (0, 0)
    wspec = pl.BlockSpec((D_MODEL, D_MODEL), const, pipeline_mode=pl.Buffered(1))
    vspec = pl.BlockSpec((1, D_MODEL), const)
    return pl.pallas_call(
        functools.partial(_merge_kernel, sub=sub),
        grid=(t // tm,),
        in_specs=[
            pl.BlockSpec((tm, D_MODEL), tok),
            pl.BlockSpec((tm, ML_V), tok),
            pl.BlockSpec((tm, FOX_W), tok),
            pl.BlockSpec((tm, D_MODEL), lambda i: (i, COL_GA // D_MODEL)),
            pl.BlockSpec((tm, D_MODEL), lambda i: (i, COL_GB // D_MODEL)),
            wspec, wspec, wspec, vspec, vspec, vspec, vspec,
        ],
        out_specs=[pl.BlockSpec((tm, D_MODEL), tok), pl.BlockSpec((tm, D_MODEL), tok)],
        out_shape=[jax.ShapeDtypeStruct((t, D_MODEL), f32), jax.ShapeDtypeStruct((t, D_MODEL), bf16)],
        compiler_params=pltpu.CompilerParams(
            dimension_semantics=("arbitrary",), vmem_limit_bytes=VMEM_LIMIT),
        name="merge",
    )(x2, h_a, h_b, proj, proj, wa, wb, wo, bga, bgb, npost, npre)


def _ffn_kernel(x1_ref, h2_ref, wup_ref, cw_ref, cb_ref, wdn_ref, npost_ref, out_ref,
                halo_sc, ubuf_sc, act_sc, acc_sc, *, blocks_per_seq, tf, down_group):
    i = pl.program_id(0)
    tm = x1_ref.shape[0]

    @pl.when(i % blocks_per_seq == 0)
    def _():
        halo_sc[...] = jnp.zeros_like(halo_sc)

    h2 = h2_ref[...]
    n_chunks = D_FF // tf

    def up(c):
        for half in range(2):
            col0 = half * D_FF + c * tf
            slot = 2 * (c % 2) + half
            u = _dot(h2, wup_ref[:, col0:col0 + tf])
            ubuf_sc[slot, 0:SUBLANES, :] = halo_sc[:, col0:col0 + tf]
            ubuf_sc[slot, SUBLANES:SUBLANES + tm, :] = u
            halo_sc[:, col0:col0 + tf] = u[tm - SUBLANES:tm, :]

    def conv(c, half):
        col0 = half * D_FF + c * tf
        slot = 2 * (c % 2) + half
        w = cw_ref[:, col0:col0 + tf]
        y = cb_ref[:, col0:col0 + tf]
        for j in range(CONV_W):
            first = SUBLANES - (CONV_W - 1 - j)
            y = y + w[j:j + 1] * ubuf_sc[slot, first:first + tm, :]
        return y

    def gate(g, a):
        inner = g * (GELU_K1 + (GELU_K1 * GELU_CUBIC) * (g * g))
        return (g * a) * (1.0 / (1.0 + jnp.exp2(inner)))

    def down(c0, c1):
        return _dot(act_sc[:, c0 * tf:c1 * tf], wdn_ref[c0 * tf:c1 * tf, :])

    up(0)
    pending, done = None, 0
    for c in range(n_chunks):
        if c + 1 < n_chunks:
            up(c + 1)
        if pending is not None:
            if pending[0] == 0:
                acc_sc[...] = down(*pending)
            else:
                acc_sc[...] += down(*pending)
            pending = None
        act_sc[:, c * tf:(c + 1) * tf] = gate(conv(c, 1), conv(c, 0)).astype(bf16)
        if (c + 1) % down_group == 0 or c + 1 == n_chunks:
            pending, done = (done, c + 1), c + 1

    y = down(*pending)
    if pending[0] != 0:
        y = y + acc_sc[...]
    out_ref[...] = x1_ref[...] + _rms(y) * npost_ref[...]


def _ffn(x1, h2, wup, cw, cb, wdn, npost, *, seq, tm=512, tf=256, down_group=2):
    t = x1.shape[0]
    tok = lambda i: (i, 0)
    const = lambda i: (0, 0)
    kern = functools.partial(_ffn_kernel, blocks_per_seq=seq // tm, tf=tf, down_group=down_group)
    return pl.pallas_call(
        kern,
        grid=(t // tm,),
        in_specs=[
            pl.BlockSpec((tm, D_MODEL), tok),
            pl.BlockSpec((tm, D_MODEL), tok),
            pl.BlockSpec((D_MODEL, 2 * D_FF), const, pipeline_mode=pl.Buffered(1)),
            pl.BlockSpec((CONV_W, 2 * D_FF), const),
            pl.BlockSpec((1, 2 * D_FF), const),
            pl.BlockSpec((D_FF, D_MODEL), const, pipeline_mode=pl.Buffered(1)),
            pl.BlockSpec((1, D_MODEL), const),
        ],
        out_specs=pl.BlockSpec((tm, D_MODEL), tok),
        out_shape=jax.ShapeDtypeStruct((t, D_MODEL), f32),
        scratch_shapes=[pltpu.VMEM((SUBLANES, 2 * D_FF), f32),
                        pltpu.VMEM((4, tm + SUBLANES, tf), f32),
                        pltpu.VMEM((tm, D_FF), bf16),
                        pltpu.VMEM((tm, D_MODEL), f32)],
        compiler_params=pltpu.CompilerParams(
            dimension_semantics=("arbitrary",), vmem_limit_bytes=VMEM_LIMIT),
        name="ffn",
    )(x1, h2, wup, cw, cb, wdn, npost)


def _layer(x2, p, l, *, batch, seq):
    w_in = p["w_in"][l]
    o_i = 2 * ML_QK + ML_V
    o_o = o_i + 2 * ML_HEADS
    o_qf = o_o + ML_V
    o_ff = o_qf + 3 * FOX_W
    o_ga = o_ff + FOX_HEADS
    w_pieces = [
        w_in[:, :o_i].astype(bf16),
        w_in[:, o_o:o_qf].astype(bf16),
        w_in[:, o_qf:o_ff].astype(bf16),
        w_in[:, o_ga:].astype(bf16),
    ]
    w_gate = jnp.concatenate([
        w_in[:, o_i:o_o], w_in[:, o_ff:o_ga],
        jnp.zeros((D_MODEL, GATE_W - N_GATES), f32)], axis=1).astype(bf16)
    b_gate = jnp.concatenate([
        p["b_ml_i"][l], p["b_ml_f"][l], p["b_fox_f"][l], jnp.zeros((GATE_W - N_GATES,), f32)])[None, :]
    tri = jnp.tril(jnp.ones((ML_CHUNK, ML_CHUNK), f32)).astype(bf16)
    row = lambda v: v[None, :].astype(f32)

    proj, gcol, grow = _inproj(x2, row(p["norm_mix_pre"][l]), w_pieces, w_gate, b_gate, tri, seq=seq)
    h_a = _mlstm(proj, gcol, grow, row(p["ml_head_norm"][l]), batch=batch, seq=seq)
    h_b = _fox(proj, gcol, batch=batch, seq=seq)
    x1, h2 = _merge(x2, h_a, h_b, proj,
                    p["w_branch_a"][l].astype(bf16), p["w_branch_b"][l].astype(bf16),
                    p["w_out"][l].astype(bf16), row(p["b_gate_a"][l]), row(p["b_gate_b"][l]),
                    row(p["norm_mix_post"][l]), row(p["norm_ffn_pre"][l]))
    return _ffn(x1, h2, p["w_up"][l].astype(bf16), p["conv_w"][l], row(p["conv_b"][l]),
                p["w_down"][l].astype(bf16), row(p["norm_ffn_post"][l]), seq=seq)


def kernel(x, norm_mix_pre, w_in, b_ml_i, b_ml_f, ml_head_norm, b_fox_f, b_gate_a, b_gate_b,
           w_branch_a, w_branch_b, w_out, norm_mix_post, norm_ffn_pre, w_up, conv_w, conv_b,
           w_down, norm_ffn_post):
    batch, seq, _ = x.shape
    p = dict(norm_mix_pre=norm_mix_pre, w_in=w_in, b_ml_i=b_ml_i, b_ml_f=b_ml_f,
             ml_head_norm=ml_head_norm, b_fox_f=b_fox_f, b_gate_a=b_gate_a, b_gate_b=b_gate_b,
             w_branch_a=w_branch_a, w_branch_b=w_branch_b, w_out=w_out,
             norm_mix_post=norm_mix_post, norm_ffn_pre=norm_ffn_pre, w_up=w_up, conv_w=conv_w,
             conv_b=conv_b, w_down=w_down, norm_ffn_post=norm_ffn_post)
    x2 = x.reshape(batch * seq, D_MODEL)
    for l in range(w_in.shape[0]):
        x2 = _layer(x2, p, l, batch=batch, seq=seq)
    return x2.reshape(batch, seq, D_MODEL)
```

```python
import functools

import jax
import jax.numpy as jnp
from jax import lax
from jax.experimental import pallas as pl
from jax.experimental.pallas import tpu as pltpu

D_MODEL = 1024
ML_HEADS = 4
ML_DQK = 128
ML_DV = 256
ML_QK = ML_HEADS * ML_DQK
ML_V = ML_HEADS * ML_DV
FOX_HEADS = 8
FOX_DH = 128
FOX_W = FOX_HEADS * FOX_DH
D_FF = 2816
CONV_W = 3
GATE_CAP = 15.0
EPS = 1e-6
LOG2E = 1.4426950408889634
GELU_CUBIC = 0.044715
GELU_K1 = -2.0 * (2.0 / 3.141592653589793) ** 0.5 * LOG2E

LANES = 128
SUBLANES = 8
GATE_W = LANES
N_GATES = 2 * ML_HEADS + FOX_HEADS
ML_CHUNK = 128
VMEM_LIMIT = 56 * 1024 * 1024

PROJ_W = 2 * ML_QK + 2 * ML_V + 3 * FOX_W + 2 * D_MODEL
COL_QM, COL_KM, COL_VM, COL_OM = 0, ML_QK, 2 * ML_QK, 2 * ML_QK + ML_V
COL_QF = COL_OM + ML_V
COL_KF = COL_QF + FOX_W
COL_VF = COL_KF + FOX_W
COL_GA = COL_VF + FOX_W
COL_GB = COL_GA + D_MODEL

f32 = jnp.float32
bf16 = jnp.bfloat16


def _log_sigmoid(z):
    return jnp.minimum(z, 0.0) - jnp.log1p(jnp.exp(-jnp.abs(z)))


def _rms(v):
    return v * lax.rsqrt(jnp.mean(v * v, axis=-1, keepdims=True) + EPS)


def _dot(a, b):
    return jnp.dot(a, b, preferred_element_type=f32)


def _dot_nt(a, b):
    return lax.dot_general(a, b, (((1,), (1,)), ((), ())), preferred_element_type=f32)


def _dot_tn(a, b):
    return lax.dot_general(a, b, (((0,), (0,)), ((), ())), preferred_element_type=f32)


def _inproj_kernel(*refs, blocks_per_seq, piece, w_blocks):
    x_ref, nw_ref, wg_ref, bg_ref, tri_ref, cs_ref = refs[:6]
    w_refs = refs[6:6 + len(w_blocks)]
    proj_ref, gcol_ref, grow_ref, h_sc, carry_sc = refs[6 + len(w_blocks):]
    assert w_blocks[0][0] == 0
    i = pl.program_id(0)
    j = pl.program_id(1)
    tm = x_ref.shape[0]

    def project(hb, w_ref=w_refs[0]):
        return (_dot(hb, w_ref[...]) * cs_ref[...]).astype(bf16)

    @pl.when(j == 0)
    def _():
        @pl.when(i % blocks_per_seq == 0)
        def _():
            carry_sc[...] = jnp.zeros_like(carry_sc)

        tri = tri_ref[...]
        lane = lax.broadcasted_iota(jnp.int32, (ML_CHUNK, GATE_W), 1)
        per_piece = piece // ML_CHUNK

        def normed(p):
            prow = slice(p * piece, (p + 1) * piece)
            hb = (_rms(x_ref[prow, :]) * nw_ref[...]).astype(bf16)
            h_sc[prow, :] = hb
            return hb

        acts = []
        hb = normed(0)
        for p in range(tm // piece):
            proj_ref[p * piece:(p + 1) * piece, :] = project(hb)
            g_piece = _dot(hb, wg_ref[...]) + bg_ref[...]
            if (p + 1) * piece < tm:
                hb = normed(p + 1)
            for q in range(per_piece):
                g = g_piece[q * ML_CHUNK:(q + 1) * ML_CHUNK]
                cap = GATE_CAP * jnp.tanh(g / GATE_CAP)
                a = jnp.where(lane < ML_HEADS, cap,
                              jnp.where(lane < 2 * ML_HEADS, _log_sigmoid(cap), _log_sigmoid(g)))
                acts.append(jnp.where(lane < N_GATES, a, 0.0))

        for r in range(tm // ML_CHUNK):
            rows = slice(r * ML_CHUNK, (r + 1) * ML_CHUNK)
            a = acts[r]
            a_hi = a.astype(bf16)
            r1 = a - a_hi.astype(f32)
            a_mid = r1.astype(bf16)
            a_lo = (r1 - a_mid.astype(f32)).astype(bf16)
            cs3 = _dot(tri, jnp.concatenate([a_hi, a_mid, a_lo], axis=1))
            cs = cs3[:, :GATE_W] + cs3[:, GATE_W:2 * GATE_W] + cs3[:, 2 * GATE_W:]
            glob = cs + carry_sc[...]
            carry_sc[...] = glob[ML_CHUNK - 1:ML_CHUNK, :]
            out = jnp.where(lane < ML_HEADS, a, jnp.where(lane < 2 * ML_HEADS, cs, glob))
            gcol_ref[rows, :] = out
            grow_ref[:, rows] = out.T[:N_GATES, :]

    for w_ref, (first, count) in zip(w_refs, w_blocks):
        @pl.when((j >= max(first, 1)) & (j < first + count))
        def _(w_ref=w_ref):
            proj_ref[...] = project(h_sc[...], w_ref)


def _proj_col_scale():
    cs = jnp.ones((1, PROJ_W), f32)
    cs = cs.at[:, COL_QM:COL_QM + ML_QK].set(ML_DQK ** -0.5)
    return cs.at[:, COL_QF:COL_QF + FOX_W].set(LOG2E * FOX_DH ** -0.5)


def _inproj(x2, nw, w_pieces, w_gate, b_gate, tri, *, seq, tm=1024, tn=1024, piece=256):
    t = x2.shape[0]
    piece = max(piece, ML_CHUNK)
    w_blocks, first = [], 0
    for w in w_pieces:
        w_blocks.append((first, w.shape[1] // tn))
        first += w.shape[1] // tn
    assert first == PROJ_W // tn
    kern = functools.partial(_inproj_kernel, blocks_per_seq=seq // tm, piece=piece, w_blocks=tuple(w_blocks))

    def w_spec(first, count):
        return pl.BlockSpec((D_MODEL, tn), lambda i, j: (
            0, jnp.where(j < first - 1, count - 1, jnp.clip(j - first, 0, count - 1))))

    return pl.pallas_call(
        kern,
        grid=(t // tm, PROJ_W // tn),
        in_specs=[
            pl.BlockSpec((tm, D_MODEL), lambda i, j: (i, 0)),
            pl.BlockSpec((1, D_MODEL), lambda i, j: (0, 0)),
            pl.BlockSpec((D_MODEL, GATE_W), lambda i, j: (0, 0)),
            pl.BlockSpec((1, GATE_W), lambda i, j: (0, 0)),
            pl.BlockSpec((ML_CHUNK, ML_CHUNK), lambda i, j: (0, 0)),
            pl.BlockSpec((1, tn), lambda i, j: (0, j)),
        ] + [w_spec(f, n) for f, n in w_blocks],
        out_specs=[
            pl.BlockSpec((tm, tn), lambda i, j: (i, j)),
            pl.BlockSpec((tm, GATE_W), lambda i, j: (i, 0)),
            pl.BlockSpec((N_GATES, tm), lambda i, j: (0, i)),
        ],
        out_shape=[
            jax.ShapeDtypeStruct((t, PROJ_W), bf16),
            jax.ShapeDtypeStruct((t, GATE_W), f32),
            jax.ShapeDtypeStruct((N_GATES, t), f32),
        ],
        scratch_shapes=[pltpu.VMEM((tm, D_MODEL), bf16), pltpu.VMEM((1, GATE_W), f32)],
        compiler_params=pltpu.CompilerParams(
            dimension_semantics=("arbitrary", "arbitrary"), vmem_limit_bytes=VMEM_LIMIT),
        name="inproj",
    )(x2, nw, w_gate, b_gate, tri, _proj_col_scale(), *w_pieces)


def _mlstm_kernel(q_ref, k_ref, v_ref, o_ref, gcol_ref, grow_ref, hnw_ref, spread_ref, out_ref, s_sc, m_sc):
    c = pl.program_id(1)
    L = ML_CHUNK
    nb = q_ref.shape[0]
    assert L == LANES

    @pl.when(c == 0)
    def _():
        s_sc[...] = jnp.zeros_like(s_sc)
        m_sc[...] = jnp.zeros_like(m_sc)

    row = lax.broadcasted_iota(jnp.int32, (L, L), 0)
    col = lax.broadcasted_iota(jnp.int32, (L, L), 1)
    causal = col <= row
    ones_blk = jnp.ones((L, LANES), bf16)

    def wide(tile, n):
        return jnp.concatenate([tile] * n, axis=1)

    P = [(n, h) for n in range(nb) for h in range(ML_HEADS)]
    C = range(len(P))
    qk_cols = lambda h: slice(h * ML_DQK, (h + 1) * ML_DQK)
    v_cols = lambda h: slice(h * ML_DV, (h + 1) * ML_DV)
    spread = []
    for n in range(nb):
        g = gcol_ref[n]
        g_hi = g.astype(bf16)
        r1 = g - g_hi.astype(f32)
        g_mid = r1.astype(bf16)
        g_lo = (r1 - g_mid.astype(f32)).astype(bf16)
        spread.append(_dot(jnp.concatenate([g_hi, g_mid, g_lo], axis=1), spread_ref[...]))
    tile = lambda n, j: spread[n][:, j * LANES:(j + 1) * LANES]
    b_t = [tile(n, h) for n, h in P]
    a_t = [tile(n, ML_HEADS + h) for n, h in P]
    gr = [grow_ref[n] for n in range(nb)]
    a_r = [gr[n][h:h + 1, :] - gr[n][ML_HEADS + h:ML_HEADS + h + 1, :] for n, h in P]
    q = [q_ref[n, :, qk_cols(h)] for n, h in P]
    k = [k_ref[n, :, qk_cols(h)] for n, h in P]
    v_aug = [jnp.concatenate([v_ref[n, :, v_cols(h)], ones_blk], axis=1) for n, h in P]
    m_prev = [m_sc[i:i + 1, :] for i in C]
    state = [s_sc[i] for i in C]

    qk = [_dot_nt(q[i], k[i]) for i in C]
    inter_mm = [_dot(q[i], state[i].astype(bf16)) for i in C]
    amat = [jnp.where(causal, a_r[i], -jnp.inf) for i in C]
    mx = [jnp.maximum(m_prev[i], jnp.broadcast_to(jnp.max(amat[i], axis=-1, keepdims=True), (L, LANES)))
          for i in C]

    mx_last = [mx[i][L - 1:L, :] for i in C]
    b_last = [b_t[i][L - 1:L, :] for i in C]
    wk = [jnp.exp(a_t[i] - mx_last[i]) for i in C]
    upd = [_dot_tn((wk[i] * k[i].astype(f32)).astype(bf16), v_aug[i]) for i in C]

    s = [(qk[i] * jnp.exp(amat[i] - mx[i])).astype(bf16) for i in C]
    w_inter = [jnp.exp(m_prev[i] - mx[i]) for i in C]
    num = [_dot(s[i], v_aug[i]) + wide(w_inter[i], 3) * inter_mm[i] for i in C]
    for i in C:
        s_sc[i] = wide(jnp.exp(m_prev[i] - mx_last[i]), 3) * state[i] + upd[i]
        m_sc[i:i + 1, :] = b_last[i] + mx_last[i]

    inv = [1.0 / jnp.maximum(jnp.abs(num[i][:, ML_DV:]), jnp.exp(-b_t[i] - mx[i])) for i in C]
    msq = [jnp.broadcast_to(jnp.mean(num[i][:, :ML_DV] * num[i][:, :ML_DV], axis=-1, keepdims=True),
                            (L, LANES)) for i in C]
    fac = [inv[i] * lax.rsqrt(inv[i] * inv[i] * msq[i] + EPS) for i in C]
    for i, (n, h) in enumerate(P):
        gate = jax.nn.sigmoid(o_ref[n, :, v_cols(h)].astype(f32)) * hnw_ref[:, v_cols(h)]
        out_ref[n, :, v_cols(h)] = (num[i][:, :ML_DV] * wide(fac[i], ML_DV // LANES) * gate).astype(bf16)


def _mlstm(proj, gcol, grow, hnw, *, batch, seq, nb=2):
    nc = seq // ML_CHUNK
    L = ML_CHUNK
    proj4 = proj.reshape(batch // nb, nb, seq, PROJ_W)
    gcol4 = gcol.reshape(batch // nb, nb, seq, GATE_W)
    grow4 = grow.reshape(N_GATES, batch // nb, nb, seq).transpose(1, 2, 0, 3)
    src = lax.broadcasted_iota(jnp.int32, (GATE_W, 2 * ML_HEADS * LANES), 0)
    dst = lax.broadcasted_iota(jnp.int32, (GATE_W, 2 * ML_HEADS * LANES), 1) // LANES
    spread = (jnp.where(src == ML_HEADS + dst % ML_HEADS, jnp.where(dst < ML_HEADS, 1.0, -1.0), 0.0)
              + jnp.where((dst >= ML_HEADS) & (src == dst - ML_HEADS), 1.0, 0.0)).astype(bf16)
    spread = jnp.concatenate([spread] * 3, axis=0)
    out = pl.pallas_call(
        _mlstm_kernel,
        grid=(batch // nb, nc),
        in_specs=[
            pl.BlockSpec((None, nb, L, ML_QK), lambda b, c: (b, 0, c, COL_QM // ML_QK)),
            pl.BlockSpec((None, nb, L, ML_QK), lambda b, c: (b, 0, c, COL_KM // ML_QK)),
            pl.BlockSpec((None, nb, L, ML_V), lambda b, c: (b, 0, c, COL_VM // ML_V)),
            pl.BlockSpec((None, nb, L, ML_V), lambda b, c: (b, 0, c, COL_OM // ML_V)),
            pl.BlockSpec((None, nb, L, GATE_W), lambda b, c: (b, 0, c, 0)),
            pl.BlockSpec((None, nb, N_GATES, L), lambda b, c: (b, 0, 0, c)),
            pl.BlockSpec((1, ML_V), lambda b, c: (0, 0)),
            pl.BlockSpec((3 * GATE_W, 2 * ML_HEADS * LANES), lambda b, c: (0, 0)),
        ],
        out_specs=pl.BlockSpec((None, nb, L, ML_V), lambda b, c: (b, 0, c, 0)),
        out_shape=jax.ShapeDtypeStruct((batch // nb, nb, seq, ML_V), bf16),
        scratch_shapes=[pltpu.VMEM((nb * ML_HEADS, ML_DQK, ML_DV + LANES), f32),
                        pltpu.VMEM((nb * ML_HEADS, LANES), f32)],
        compiler_params=pltpu.CompilerParams(
            dimension_semantics=("arbitrary", "arbitrary"), vmem_limit_bytes=VMEM_LIMIT),
        name="mlstm",
    )(proj4, proj4, proj4, proj4, gcol4, grow4, hnw, spread)
    return out.reshape(batch * seq, ML_V)


def _fox_kernel(q_ref, k_ref, v_ref, gcol_ref, out_ref, kaug_sc, st_sc, mx_sc, m_sc, l_sc, acc_sc,
                *, blk, setup_rows, ahead):
    hg = pl.program_id(1)
    qi = pl.program_id(2)
    seq = k_ref.shape[0]
    nh = kaug_sc.shape[0]
    head_cols = [slice(i * FOX_DH, (i + 1) * FOX_DH) for i in range(nh)]

    @pl.when(qi == 0)
    def _():
        r = lax.broadcasted_iota(jnp.int32, (GATE_W, nh * LANES), 0)
        c = lax.broadcasted_iota(jnp.int32, (GATE_W, nh * LANES), 1)
        src = 2 * ML_HEADS + hg * nh + c // LANES
        pick = jnp.concatenate([((r == src) & (c % LANES == j)).astype(bf16) for j in range(3)], axis=0)
        for n in range(seq // setup_rows):
            rows = pl.ds(n * setup_rows, setup_rows)
            a = gcol_ref[rows, :] * (-LOG2E)
            a_hi = a.astype(bf16)
            r1 = a - a_hi.astype(f32)
            a_mid = r1.astype(bf16)
            a_lo = (r1 - a_mid.astype(f32)).astype(bf16)
            aug = _dot(jnp.concatenate([a_hi, a_mid, a_lo], axis=1), pick).astype(bf16)
            for i in range(nh):
                kaug_sc[i, rows, :FOX_DH] = k_ref[rows, head_cols[i]]
                kaug_sc[i, rows, FOX_DH:] = aug[:, i * LANES:(i + 1) * LANES]

    ones3 = (lax.broadcasted_iota(jnp.int32, (blk, LANES), 1) < 3).astype(bf16)
    q_aug = [jnp.concatenate([q_ref[:, head_cols[i]], ones3], axis=1) for i in range(nh)]

    m_sc[...] = jnp.full(m_sc.shape, -jnp.inf, f32)
    l_sc[...] = jnp.zeros(l_sc.shape, f32)
    acc_sc[...] = jnp.zeros(acc_sc.shape, f32)

    def logits_stage(i, j, masked):
        start = pl.multiple_of(j * blk, blk)
        st = _dot_nt(kaug_sc[i, pl.ds(start, blk), :], q_aug[i])
        if masked:
            row = lax.broadcasted_iota(jnp.int32, (blk, blk), 0)
            col = lax.broadcasted_iota(jnp.int32, (blk, blk), 1)
            st = jnp.where(row <= col, st, -jnp.inf)
        st_sc[i] = st
        mx_sc[i] = jnp.max(st, axis=0, keepdims=True)

    def softmax_stage(i, j):
        start = pl.multiple_of(j * blk, blk)
        m = m_sc[i]
        m_new = jnp.maximum(m, mx_sc[i])
        alpha = jnp.exp2(m - m_new)
        p = jnp.exp2(st_sc[i] - m_new)
        l_sc[i] = alpha * l_sc[i] + jnp.sum(p, axis=0, keepdims=True)
        acc_sc[i] = alpha * acc_sc[i] + _dot_tn(v_ref[pl.ds(start, blk), head_cols[i]], p.astype(bf16))
        m_sc[i] = m_new

    def round_robin(cur, nxt, masked):
        for i in range(nh):
            if i + ahead < nh:
                logits_stage(i + ahead, cur, masked)
            elif nxt is not None:
                logits_stage(i + ahead - nh, nxt, False)
            softmax_stage(i, cur)

    for i in range(ahead):
        logits_stage(i, qi, True)
    round_robin(qi, 0, True)

    def body(k, _):
        round_robin(k - 1, jnp.minimum(k, qi - 1), False)
        return 0

    lax.fori_loop(1, qi + 1, body, 0)

    for i in range(nh):
        out_ref[:, head_cols[i]] = (acc_sc[i] * (1.0 / l_sc[i])).T.astype(bf16)


def _fox(proj, gcol, *, batch, seq, blk=512, nh=4, ahead=1):
    assert 1 <= ahead < nh
    t = proj.shape[0]
    nq = seq // blk
    w = nh * FOX_DH
    kern = functools.partial(_fox_kernel, blk=blk, setup_rows=min(seq, 512), ahead=ahead)
    return pl.pallas_call(
        kern,
        grid=(batch, FOX_HEADS // nh, nq),
        in_specs=[
            pl.BlockSpec((blk, w), lambda b, h, i: (b * nq + i, COL_QF // w + h)),
            pl.BlockSpec((seq, w), lambda b, h, i: (b, COL_KF // w + h)),
            pl.BlockSpec((seq, w), lambda b, h, i: (b, COL_VF // w + h)),
            pl.BlockSpec((seq, GATE_W), lambda b, h, i: (b, 0)),
        ],
        out_specs=pl.BlockSpec((blk, w), lambda b, h, i: (b * nq + i, h)),
        out_shape=jax.ShapeDtypeStruct((t, FOX_W), bf16),
        scratch_shapes=[pltpu.VMEM((nh, seq, FOX_DH + LANES), bf16),
                        pltpu.VMEM((nh, blk, blk), f32),
                        pltpu.VMEM((nh, 1, blk), f32),
                        pltpu.VMEM((nh, 1, blk), f32),
                        pltpu.VMEM((nh, 1, blk), f32),
                        pltpu.VMEM((nh, FOX_DH, blk), f32)],
        compiler_params=pltpu.CompilerParams(
            dimension_semantics=("arbitrary", "arbitrary", "arbitrary"), vmem_limit_bytes=VMEM_LIMIT),
        name="fox",
    )(proj, proj, proj, gcol)


def _merge_kernel(x_ref, ha_ref, hb_ref, ga_ref, gb_ref, wa_ref, wb_ref, wo_ref,
                  bga_ref, bgb_ref, npost_ref, npre_ref, x1_ref, h2_ref, *, sub):
    n_sub = x_ref.shape[0] // sub
    rows = [slice(r * sub, (r + 1) * sub) for r in range(n_sub)]

    def branches(r):
        return _dot(ha_ref[rows[r], :], wa_ref[...]), _dot(hb_ref[rows[r], :], wb_ref[...])

    def finish(r, z):
        x1 = x_ref[rows[r], :] + _rms(z) * npost_ref[...]
        x1_ref[rows[r], :] = x1
        h2_ref[rows[r], :] = (_rms(x1) * npre_ref[...]).astype(bf16)

    y_next = branches(0)
    z_prev = None
    for r in range(n_sub):
        ya, yb = y_next
        if r + 1 < n_sub:
            y_next = branches(r + 1)
        merged = (jax.nn.sigmoid(ga_ref[rows[r], :].astype(f32) + bga_ref[...]) * ya
                  + jax.nn.sigmoid(gb_ref[rows[r], :].astype(f32) + bgb_ref[...]) * yb)
        z = _dot(merged.astype(bf16), wo_ref[...])
        if z_prev is not None:
            finish(r - 1, z_prev)
        z_prev = z
    finish(n_sub - 1, z_prev)


def _merge(x2, h_a, h_b, proj, wa, wb, wo, bga, bgb, npost, npre, *, tm=1024, sub=256):
    t = x2.shape[0]
    tm = min(tm, t)
    tok = lambda i: (i, 0)
    const = lambda i: (0, 0)
    wspec = pl.BlockSpec((D_MODEL, D_MODEL), const, pipeline_mode=pl.Buffered(1))
    vspec = pl.BlockSpec((1, D_MODEL), const)
    return pl.pallas_call(
        functools.partial(_merge_kernel, sub=sub),
        grid=(t // tm,),
        in_specs=[
            pl.BlockSpec((tm, D_MODEL), tok),
            pl.BlockSpec((tm, ML_V), tok),
            pl.BlockSpec((tm, FOX_W), tok),
            pl.BlockSpec((tm, D_MODEL), lambda i: (i, COL_GA // D_MODEL)),
            pl.BlockSpec((tm, D_MODEL), lambda i: (i, COL_GB // D_MODEL)),
            wspec, wspec, wspec, vspec, vspec, vspec, vspec,
        ],
        out_specs=[pl.BlockSpec((tm, D_MODEL), tok), pl.BlockSpec((tm, D_MODEL), tok)],
        out_shape=[jax.ShapeDtypeStruct((t, D_MODEL), f32), jax.ShapeDtypeStruct((t, D_MODEL), bf16)],
        compiler_params=pltpu.CompilerParams(
            dimension_semantics=("arbitrary",), vmem_limit_bytes=VMEM_LIMIT),
        name="merge",
    )(x2, h_a, h_b, proj, proj, wa, wb, wo, bga, bgb, npost, npre)


def _ffn_kernel(x1_ref, h2_ref, wup_ref, cw_ref, cb_ref, wdn_ref, npost_ref, out_ref,
                halo_sc, ubuf_sc, act_sc, acc_sc, *, blocks_per_seq, tf, down_group):
    i = pl.program_id(0)
    tm = x1_ref.shape[0]

    @pl.when(i % blocks_per_seq == 0)
    def _():
        halo_sc[...] = jnp.zeros_like(halo_sc)

    h2 = h2_ref[...]
    n_chunks = D_FF // tf

    def up(c):
        for half in range(2):
            col0 = half * D_FF + c * tf
            slot = 2 * (c % 2) + half
            u = _dot(h2, wup_ref[:, col0:col0 + tf])
            ubuf_sc[slot, 0:SUBLANES, :] = halo_sc[:, col0:col0 + tf]
            ubuf_sc[slot, SUBLANES:SUBLANES + tm, :] = u
            halo_sc[:, col0:col0 + tf] = u[tm - SUBLANES:tm, :]

    def conv(c, half):
        col0 = half * D_FF + c * tf
        slot = 2 * (c % 2) + half
        w = cw_ref[:, col0:col0 + tf]
        y = cb_ref[:, col0:col0 + tf]
        for j in range(CONV_W):
            first = SUBLANES - (CONV_W - 1 - j)
            y = y + w[j:j + 1] * ubuf_sc[slot, first:first + tm, :]
        return y

    def gate(g, a):
        inner = g * (GELU_K1 + (GELU_K1 * GELU_CUBIC) * (g * g))
        return (g * a) * (1.0 / (1.0 + jnp.exp2(inner)))

    def down(c0, c1):
        return _dot(act_sc[:, c0 * tf:c1 * tf], wdn_ref[c0 * tf:c1 * tf, :])

    up(0)
    pending, done = None, 0
    for c in range(n_chunks):
        if c + 1 < n_chunks:
            up(c + 1)
        if pending is not None:
            if pending[0] == 0:
                acc_sc[...] = down(*pending)
            else:
                acc_sc[...] += down(*pending)
            pending = None
        act_sc[:, c * tf:(c + 1) * tf] = gate(conv(c, 1), conv(c, 0)).astype(bf16)
        if (c + 1) % down_group == 0 or c + 1 == n_chunks:
            pending, done = (done, c + 1), c + 1

    y = down(*pending)
    if pending[0] != 0:
        y = y + acc_sc[...]
    out_ref[...] = x1_ref[...] + _rms(y) * npost_ref[...]


def _ffn(x1, h2, wup, cw, cb, wdn, npost, *, seq, tm=512, tf=256, down_group=2):
    t = x1.shape[0]
    tok = lambda i: (i, 0)
    const = lambda i: (0, 0)
    kern = functools.partial(_ffn_kernel, blocks_per_seq=seq // tm, tf=tf, down_group=down_group)
    return pl.pallas_call(
        kern,
        grid=(t // tm,),
        in_specs=[
            pl.BlockSpec((tm, D_MODEL), tok),
            pl.BlockSpec((tm, D_MODEL), tok),
            pl.BlockSpec((D_MODEL, 2 * D_FF), const, pipeline_mode=pl.Buffered(1)),
            pl.BlockSpec((CONV_W, 2 * D_FF), const),
            pl.BlockSpec((1, 2 * D_FF), const),
            pl.BlockSpec((D_FF, D_MODEL), const, pipeline_mode=pl.Buffered(1)),
            pl.BlockSpec((1, D_MODEL), const),
        ],
        out_specs=pl.BlockSpec((tm, D_MODEL), tok),
        out_shape=jax.ShapeDtypeStruct((t, D_MODEL), f32),
        scratch_shapes=[pltpu.VMEM((SUBLANES, 2 * D_FF), f32),
                        pltpu.VMEM((4, tm + SUBLANES, tf), f32),
                        pltpu.VMEM((tm, D_FF), bf16),
                        pltpu.VMEM((tm, D_MODEL), f32)],
        compiler_params=pltpu.CompilerParams(
            dimension_semantics=("arbitrary",), vmem_limit_bytes=VMEM_LIMIT),
        name="ffn",
    )(x1, h2, wup, cw, cb, wdn, npost)


def _layer(x2, p, l, *, batch, seq):
    w_in = p["w_in"][l]
    o_i = 2 * ML_QK + ML_V
    o_o = o_i + 2 * ML_HEADS
    o_qf = o_o + ML_V
    o_ff = o_qf + 3 * FOX_W
    o_ga = o_ff + FOX_HEADS
    w_pieces = [
        w_in[:, :o_i].astype(bf16),
        w_in[:, o_o:o_qf].astype(bf16),
        w_in[:, o_qf:o_ff].astype(bf16),
        w_in[:, o_ga:].astype(bf16),
    ]
    w_gate = jnp.concatenate([
        w_in[:, o_i:o_o], w_in[:, o_ff:o_ga],
        jnp.zeros((D_MODEL, GATE_W - N_GATES), f32)], axis=1).astype(bf16)
    b_gate = jnp.concatenate([
        p["b_ml_i"][l], p["b_ml_f"][l], p["b_fox_f"][l], jnp.zeros((GATE_W - N_GATES,), f32)])[None, :]
    tri = jnp.tril(jnp.ones((ML_CHUNK, ML_CHUNK), f32)).astype(bf16)
    row = lambda v: v[None, :].astype(f32)

    proj, gcol, grow = _inproj(x2, row(p["norm_mix_pre"][l]), w_pieces, w_gate, b_gate, tri, seq=seq)
    h_a = _mlstm(proj, gcol, grow, row(p["ml_head_norm"][l]), batch=batch, seq=seq)
    h_b = _fox(proj, gcol, batch=batch, seq=seq)
    x1, h2 = _merge(x2, h_a, h_b, proj,
                    p["w_branch_a"][l].astype(bf16), p["w_branch_b"][l].astype(bf16),
                    p["w_out"][l].astype(bf16), row(p["b_gate_a"][l]), row(p["b_gate_b"][l]),
                    row(p["norm_mix_post"][l]), row(p["norm_ffn_pre"][l]))
    return _ffn(x1, h2, p["w_up"][l].astype(bf16), p["conv_w"][l], row(p["conv_b"][l]),
                p["w_down"][l].astype(bf16), row(p["norm_ffn_post"][l]), seq=seq)


def kernel(x, norm_mix_pre, w_in, b_ml_i, b_ml_f, ml_head_norm, b_fox_f, b_gate_a, b_gate_b,
           w_branch_a, w_branch_b, w_out, norm_mix_post, norm_ffn_pre, w_up, conv_w, conv_b,
           w_down, norm_ffn_post):
    batch, seq, _ = x.shape
    p = dict(norm_mix_pre=norm_mix_pre, w_in=w_in, b_ml_i=b_ml_i, b_ml_f=b_ml_f,
             ml_head_norm=ml_head_norm, b_fox_f=b_fox_f, b_gate_a=b_gate_a, b_gate_b=b_gate_b,
             w_branch_a=w_branch_a, w_branch_b=w_branch_b, w_out=w_out,
             norm_mix_post=norm_mix_post, norm_ffn_pre=norm_ffn_pre, w_up=w_up, conv_w=conv_w,
             conv_b=conv_b, w_down=w_down, norm_ffn_post=norm_ffn_post)
    x2 = x.reshape(batch * seq, D_MODEL)
    for l in range(w_in.shape[0]):
        x2 = _layer(x2, p, l, batch=batch, seq=seq)
    return x2.reshape(batch, seq, D_MODEL)
```

```python
import functools

import jax
import jax.numpy as jnp
from jax import lax
from jax.experimental import pallas as pl
from jax.experimental.pallas import tpu as pltpu

D_MODEL = 1024
ML_HEADS = 4
ML_DQK = 128
ML_DV = 256
ML_QK = ML_HEADS * ML_DQK
ML_V = ML_HEADS * ML_DV
FOX_HEADS = 8
FOX_DH = 128
FOX_W = FOX_HEADS * FOX_DH
D_FF = 2816
CONV_W = 3
GATE_CAP = 15.0
EPS = 1e-6
LOG2E = 1.4426950408889634
GELU_CUBIC = 0.044715
GELU_K1 = -2.0 * (2.0 / 3.141592653589793) ** 0.5 * LOG2E

LANES = 128
SUBLANES = 8
GATE_W = LANES
N_GATES = 2 * ML_HEADS + FOX_HEADS
ML_CHUNK = 128
VMEM_LIMIT = 56 * 1024 * 1024

PROJ_W = 2 * ML_QK + 2 * ML_V + 3 * FOX_W + 2 * D_MODEL
COL_QM, COL_KM, COL_VM, COL_OM = 0, ML_QK, 2 * ML_QK, 2 * ML_QK + ML_V
COL_QF = COL_OM + ML_V
COL_KF = COL_QF + FOX_W
COL_VF = COL_KF + FOX_W
COL_GA = COL_VF + FOX_W
COL_GB = COL_GA + D_MODEL

f32 = jnp.float32
bf16 = jnp.bfloat16


def _log_sigmoid(z):
    return jnp.minimum(z, 0.0) - jnp.log1p(jnp.exp(-jnp.abs(z)))


def _rms(v):
    return v * lax.rsqrt(jnp.mean(v * v, axis=-1, keepdims=True) + EPS)


def _dot(a, b):
    return jnp.dot(a, b, preferred_element_type=f32)


def _dot_nt(a, b):
    return lax.dot_general(a, b, (((1,), (1,)), ((), ())), preferred_element_type=f32)


def _dot_tn(a, b):
    return lax.dot_general(a, b, (((0,), (0,)), ((), ())), preferred_element_type=f32)


def _inproj_kernel(*refs, blocks_per_seq, piece, w_blocks):
    x_ref, nw_ref, wg_ref, bg_ref, tri_ref, cs_ref = refs[:6]
    w_refs = refs[6:6 + len(w_blocks)]
    proj_ref, gcol_ref, grow_ref, h_sc, carry_sc = refs[6 + len(w_blocks):]
    assert w_blocks[0][0] == 0
    i = pl.program_id(0)
    j = pl.program_id(1)
    tm = x_ref.shape[0]

    def project(hb, w_ref=w_refs[0]):
        return (_dot(hb, w_ref[...]) * cs_ref[...]).astype(bf16)

    @pl.when(j == 0)
    def _():
        @pl.when(i % blocks_per_seq == 0)
        def _():
            carry_sc[...] = jnp.zeros_like(carry_sc)

        tri = tri_ref[...]
        lane = lax.broadcasted_iota(jnp.int32, (ML_CHUNK, GATE_W), 1)
        per_piece = piece // ML_CHUNK

        def normed(p):
            prow = slice(p * piece, (p + 1) * piece)
            hb = (_rms(x_ref[prow, :]) * nw_ref[...]).astype(bf16)
            h_sc[prow, :] = hb
            return hb

        acts = []
        hb = normed(0)
        for p in range(tm // piece):
            proj_ref[p * piece:(p + 1) * piece, :] = project(hb)
            g_piece = _dot(hb, wg_ref[...]) + bg_ref[...]
            if (p + 1) * piece < tm:
                hb = normed(p + 1)
            for q in range(per_piece):
                g = g_piece[q * ML_CHUNK:(q + 1) * ML_CHUNK]
                cap = GATE_CAP * jnp.tanh(g / GATE_CAP)
                a = jnp.where(lane < ML_HEADS, cap,
                              jnp.where(lane < 2 * ML_HEADS, _log_sigmoid(cap), _log_sigmoid(g)))
                acts.append(jnp.where(lane < N_GATES, a, 0.0))

        for r in range(tm // ML_CHUNK):
            rows = slice(r * ML_CHUNK, (r + 1) * ML_CHUNK)
            a = acts[r]
            a_hi = a.astype(bf16)
            r1 = a - a_hi.astype(f32)
            a_mid = r1.astype(bf16)
            a_lo = (r1 - a_mid.astype(f32)).astype(bf16)
            cs3 = _dot(tri, jnp.concatenate([a_hi, a_mid, a_lo], axis=1))
            cs = cs3[:, :GATE_W] + cs3[:, GATE_W:2 * GATE_W] + cs3[:, 2 * GATE_W:]
            glob = cs + carry_sc[...]
            carry_sc[...] = glob[ML_CHUNK - 1:ML_CHUNK, :]
            out = jnp.where(lane < ML_HEADS, a, jnp.where(lane < 2 * ML_HEADS, cs, glob))
            gcol_ref[rows, :] = out
            grow_ref[:, rows] = out.T[:N_GATES, :]

    for w_ref, (first, count) in zip(w_refs, w_blocks):
        @pl.when((j >= max(first, 1)) & (j < first + count))
        def _(w_ref=w_ref):
            proj_ref[...] = project(h_sc[...], w_ref)


def _proj_col_scale():
    cs = jnp.ones((1, PROJ_W), f32)
    cs = cs.at[:, COL_QM:COL_QM + ML_QK].set(ML_DQK ** -0.5)
    return cs.at[:, COL_QF:COL_QF + FOX_W].set(LOG2E * FOX_DH ** -0.5)


def _inproj(x2, nw, w_pieces, w_gate, b_gate, tri, *, seq, tm=2048, tn=1024, piece=256):
    t = x2.shape[0]
    piece = max(piece, ML_CHUNK)
    w_blocks, first = [], 0
    for w in w_pieces:
        w_blocks.append((first, w.shape[1] // tn))
        first += w.shape[1] // tn
    assert first == PROJ_W // tn
    kern = functools.partial(_inproj_kernel, blocks_per_seq=seq // tm, piece=piece, w_blocks=tuple(w_blocks))

    def w_spec(first, count):
        return pl.BlockSpec((D_MODEL, tn), lambda i, j: (
            0, jnp.where(j < first - 1, count - 1, jnp.clip(j - first, 0, count - 1))))

    return pl.pallas_call(
        kern,
        grid=(t // tm, PROJ_W // tn),
        in_specs=[
            pl.BlockSpec((tm, D_MODEL), lambda i, j: (i, 0)),
            pl.BlockSpec((1, D_MODEL), lambda i, j: (0, 0)),
            pl.BlockSpec((D_MODEL, GATE_W), lambda i, j: (0, 0)),
            pl.BlockSpec((1, GATE_W), lambda i, j: (0, 0)),
            pl.BlockSpec((ML_CHUNK, ML_CHUNK), lambda i, j: (0, 0)),
            pl.BlockSpec((1, tn), lambda i, j: (0, j)),
        ] + [w_spec(f, n) for f, n in w_blocks],
        out_specs=[
            pl.BlockSpec((tm, tn), lambda i, j: (i, j)),
            pl.BlockSpec((tm, GATE_W), lambda i, j: (i, 0)),
            pl.BlockSpec((N_GATES, tm), lambda i, j: (0, i)),
        ],
        out_shape=[
            jax.ShapeDtypeStruct((t, PROJ_W), bf16),
            jax.ShapeDtypeStruct((t, GATE_W), f32),
            jax.ShapeDtypeStruct((N_GATES, t), f32),
        ],
        scratch_shapes=[pltpu.VMEM((tm, D_MODEL), bf16), pltpu.VMEM((1, GATE_W), f32)],
        compiler_params=pltpu.CompilerParams(
            dimension_semantics=("arbitrary", "arbitrary"), vmem_limit_bytes=VMEM_LIMIT),
        name="inproj",
    )(x2, nw, w_gate, b_gate, tri, _proj_col_scale(), *w_pieces)


def _mlstm_kernel(q_ref, k_ref, v_ref, o_ref, gcol_ref, grow_ref, hnw_ref, spread_ref, out_ref, s_sc, m_sc):
    c = pl.program_id(1)
    L = ML_CHUNK
    nb = q_ref.shape[0]
    assert L == LANES

    @pl.when(c == 0)
    def _():
        s_sc[...] = jnp.zeros_like(s_sc)
        m_sc[...] = jnp.zeros_like(m_sc)

    row = lax.broadcasted_iota(jnp.int32, (L, L), 0)
    col = lax.broadcasted_iota(jnp.int32, (L, L), 1)
    causal = col <= row
    ones_blk = jnp.ones((L, LANES), bf16)

    def wide(tile, n):
        return jnp.concatenate([tile] * n, axis=1)

    P = [(n, h) for n in range(nb) for h in range(ML_HEADS)]
    C = range(len(P))
    qk_cols = lambda h: slice(h * ML_DQK, (h + 1) * ML_DQK)
    v_cols = lambda h: slice(h * ML_DV, (h + 1) * ML_DV)
    spread = []
    for n in range(nb):
        g = gcol_ref[n]
        g_hi = g.astype(bf16)
        r1 = g - g_hi.astype(f32)
        g_mid = r1.astype(bf16)
        g_lo = (r1 - g_mid.astype(f32)).astype(bf16)
        spread.append(_dot(jnp.concatenate([g_hi, g_mid, g_lo], axis=1), spread_ref[...]))
    tile = lambda n, j: spread[n][:, j * LANES:(j + 1) * LANES]
    b_t = [tile(n, h) for n, h in P]
    a_t = [tile(n, ML_HEADS + h) for n, h in P]
    gr = [grow_ref[n] for n in range(nb)]
    a_r = [gr[n][h:h + 1, :] - gr[n][ML_HEADS + h:ML_HEADS + h + 1, :] for n, h in P]
    q = [q_ref[n, :, qk_cols(h)] for n, h in P]
    k = [k_ref[n, :, qk_cols(h)] for n, h in P]
    v_aug = [jnp.concatenate([v_ref[n, :, v_cols(h)], ones_blk], axis=1) for n, h in P]
    m_prev = [m_sc[i:i + 1, :] for i in C]
    state = [s_sc[i] for i in C]

    qk = [_dot_nt(q[i], k[i]) for i in C]
    inter_mm = [_dot(q[i], state[i].astype(bf16)) for i in C]
    amat = [jnp.where(causal, a_r[i], -jnp.inf) for i in C]
    mx = [jnp.maximum(m_prev[i], jnp.broadcast_to(jnp.max(amat[i], axis=-1, keepdims=True), (L, LANES)))
          for i in C]

    mx_last = [mx[i][L - 1:L, :] for i in C]
    b_last = [b_t[i][L - 1:L, :] for i in C]
    wk = [jnp.exp(a_t[i] - mx_last[i]) for i in C]
    upd = [_dot_tn((wk[i] * k[i].astype(f32)).astype(bf16), v_aug[i]) for i in C]

    s = [(qk[i] * jnp.exp(amat[i] - mx[i])).astype(bf16) for i in C]
    w_inter = [jnp.exp(m_prev[i] - mx[i]) for i in C]
    num = [_dot(s[i], v_aug[i]) + wide(w_inter[i], 3) * inter_mm[i] for i in C]
    for i in C:
        s_sc[i] = wide(jnp.exp(m_prev[i] - mx_last[i]), 3) * state[i] + upd[i]
        m_sc[i:i + 1, :] = b_last[i] + mx_last[i]

    inv = [1.0 / jnp.maximum(jnp.abs(num[i][:, ML_DV:]), jnp.exp(-b_t[i] - mx[i])) for i in C]
    msq = [jnp.broadcast_to(jnp.mean(num[i][:, :ML_DV] * num[i][:, :ML_DV], axis=-1, keepdims=True),
                            (L, LANES)) for i in C]
    fac = [inv[i] * lax.rsqrt(inv[i] * inv[i] * msq[i] + EPS) for i in C]
    for i, (n, h) in enumerate(P):
        gate = jax.nn.sigmoid(o_ref[n, :, v_cols(h)].astype(f32)) * hnw_ref[:, v_cols(h)]
        out_ref[n, :, v_cols(h)] = (num[i][:, :ML_DV] * wide(fac[i], ML_DV // LANES) * gate).astype(bf16)


def _mlstm(proj, gcol, grow, hnw, *, batch, seq, nb=2):
    nc = seq // ML_CHUNK
    L = ML_CHUNK
    proj4 = proj.reshape(batch // nb, nb, seq, PROJ_W)
    gcol4 = gcol.reshape(batch // nb, nb, seq, GATE_W)
    grow4 = grow.reshape(N_GATES, batch // nb, nb, seq).transpose(1, 2, 0, 3)
    src = lax.broadcasted_iota(jnp.int32, (GATE_W, 2 * ML_HEADS * LANES), 0)
    dst = lax.broadcasted_iota(jnp.int32, (GATE_W, 2 * ML_HEADS * LANES), 1) // LANES
    spread = (jnp.where(src == ML_HEADS + dst % ML_HEADS, jnp.where(dst < ML_HEADS, 1.0, -1.0), 0.0)
              + jnp.where((dst >= ML_HEADS) & (src == dst - ML_HEADS), 1.0, 0.0)).astype(bf16)
    spread = jnp.concatenate([spread] * 3, axis=0)
    out = pl.pallas_call(
        _mlstm_kernel,
        grid=(batch // nb, nc),
        in_specs=[
            pl.BlockSpec((None, nb, L, ML_QK), lambda b, c: (b, 0, c, COL_QM // ML_QK)),
            pl.BlockSpec((None, nb, L, ML_QK), lambda b, c: (b, 0, c, COL_KM // ML_QK)),
            pl.BlockSpec((None, nb, L, ML_V), lambda b, c: (b, 0, c, COL_VM // ML_V)),
            pl.BlockSpec((None, nb, L, ML_V), lambda b, c: (b, 0, c, COL_OM // ML_V)),
            pl.BlockSpec((None, nb, L, GATE_W), lambda b, c: (b, 0, c, 0)),
            pl.BlockSpec((None, nb, N_GATES, L), lambda b, c: (b, 0, 0, c)),
            pl.BlockSpec((1, ML_V), lambda b, c: (0, 0)),
            pl.BlockSpec((3 * GATE_W, 2 * ML_HEADS * LANES), lambda b, c: (0, 0)),
        ],
        out_specs=pl.BlockSpec((None, nb, L, ML_V), lambda b, c: (b, 0, c, 0)),
        out_shape=jax.ShapeDtypeStruct((batch // nb, nb, seq, ML_V), bf16),
        scratch_shapes=[pltpu.VMEM((nb * ML_HEADS, ML_DQK, ML_DV + LANES), f32),
                        pltpu.VMEM((nb * ML_HEADS, LANES), f32)],
        compiler_params=pltpu.CompilerParams(
            dimension_semantics=("arbitrary", "arbitrary"), vmem_limit_bytes=VMEM_LIMIT),
        name="mlstm",
    )(proj4, proj4, proj4, proj4, gcol4, grow4, hnw, spread)
    return out.reshape(batch * seq, ML_V)


def _fox_kernel(q_ref, k_ref, v_ref, gcol_ref, out_ref, kaug_sc, st_sc, mx_sc, m_sc, l_sc, acc_sc,
                *, blk, setup_rows, ahead):
    hg = pl.program_id(1)
    qi = pl.program_id(2)
    seq = k_ref.shape[0]
    nh = kaug_sc.shape[0]
    head_cols = [slice(i * FOX_DH, (i + 1) * FOX_DH) for i in range(nh)]

    @pl.when(qi == 0)
    def _():
        r = lax.broadcasted_iota(jnp.int32, (GATE_W, nh * LANES), 0)
        c = lax.broadcasted_iota(jnp.int32, (GATE_W, nh * LANES), 1)
        src = 2 * ML_HEADS + hg * nh + c // LANES
        pick = jnp.concatenate([((r == src) & (c % LANES == j)).astype(bf16) for j in range(3)], axis=0)
        for n in range(seq // setup_rows):
            rows = pl.ds(n * setup_rows, setup_rows)
            a = gcol_ref[rows, :] * (-LOG2E)
            a_hi = a.astype(bf16)
            r1 = a - a_hi.astype(f32)
            a_mid = r1.astype(bf16)
            a_lo = (r1 - a_mid.astype(f32)).astype(bf16)
            aug = _dot(jnp.concatenate([a_hi, a_mid, a_lo], axis=1), pick).astype(bf16)
            for i in range(nh):
                kaug_sc[i, rows, :FOX_DH] = k_ref[rows, head_cols[i]]
                kaug_sc[i, rows, FOX_DH:] = aug[:, i * LANES:(i + 1) * LANES]

    ones3 = (lax.broadcasted_iota(jnp.int32, (blk, LANES), 1) < 3).astype(bf16)
    q_aug = [jnp.concatenate([q_ref[:, head_cols[i]], ones3], axis=1) for i in range(nh)]

    m_sc[...] = jnp.full(m_sc.shape, -jnp.inf, f32)
    l_sc[...] = jnp.zeros(l_sc.shape, f32)
    acc_sc[...] = jnp.zeros(acc_sc.shape, f32)

    def logits_stage(i, j, masked):
        start = pl.multiple_of(j * blk, blk)
        st = _dot_nt(kaug_sc[i, pl.ds(start, blk), :], q_aug[i])
        if masked:
            row = lax.broadcasted_iota(jnp.int32, (blk, blk), 0)
            col = lax.broadcasted_iota(jnp.int32, (blk, blk), 1)
            st = jnp.where(row <= col, st, -jnp.inf)
        st_sc[i] = st
        mx_sc[i] = jnp.max(st, axis=0, keepdims=True)

    def softmax_stage(i, j):
        start = pl.multiple_of(j * blk, blk)
        m = m_sc[i]
        m_new = jnp.maximum(m, mx_sc[i])
        alpha = jnp.exp2(m - m_new)
        p = jnp.exp2(st_sc[i] - m_new)
        l_sc[i] = alpha * l_sc[i] + jnp.sum(p, axis=0, keepdims=True)
        acc_sc[i] = alpha * acc_sc[i] + _dot_tn(v_ref[pl.ds(start, blk), head_cols[i]], p.astype(bf16))
        m_sc[i] = m_new

    def round_robin(cur, nxt, masked):
        for i in range(nh):
            if i + ahead < nh:
                logits_stage(i + ahead, cur, masked)
            elif nxt is not None:
                logits_stage(i + ahead - nh, nxt, False)
            softmax_stage(i, cur)

    for i in range(ahead):
        logits_stage(i, qi, True)
    round_robin(qi, 0, True)

    def body(k, _):
        round_robin(k - 1, jnp.minimum(k, qi - 1), False)
        return 0

    lax.fori_loop(1, qi + 1, body, 0)

    for i in range(nh):
        out_ref[:, head_cols[i]] = (acc_sc[i] * (1.0 / l_sc[i])).T.astype(bf16)


def _fox(proj, gcol, *, batch, seq, blk=512, nh=4, ahead=1):
    assert 1 <= ahead < nh
    t = proj.shape[0]
    nq = seq // blk
    w = nh * FOX_DH
    kern = functools.partial(_fox_kernel, blk=blk, setup_rows=min(seq, 512), ahead=ahead)
    return pl.pallas_call(
        kern,
        grid=(batch, FOX_HEADS // nh, nq),
        in_specs=[
            pl.BlockSpec((blk, w), lambda b, h, i: (b * nq + i, COL_QF // w + h)),
            pl.BlockSpec((seq, w), lambda b, h, i: (b, COL_KF // w + h)),
            pl.BlockSpec((seq, w), lambda b, h, i: (b, COL_VF // w + h)),
            pl.BlockSpec((seq, GATE_W), lambda b, h, i: (b, 0)),
        ],
        out_specs=pl.BlockSpec((blk, w), lambda b, h, i: (b * nq + i, h)),
        out_shape=jax.ShapeDtypeStruct((t, FOX_W), bf16),
        scratch_shapes=[pltpu.VMEM((nh, seq, FOX_DH + LANES), bf16),
                        pltpu.VMEM((nh, blk, blk), f32),
                        pltpu.VMEM((nh, 1, blk), f32),
                        pltpu.VMEM((nh, 1, blk), f32),
                        pltpu.VMEM((nh, 1, blk), f32),
                        pltpu.VMEM((nh, FOX_DH, blk), f32)],
        compiler_params=pltpu.CompilerParams(
            dimension_semantics=("arbitrary", "arbitrary", "arbitrary"), vmem_limit_bytes=VMEM_LIMIT),
        name="fox",
    )(proj, proj, proj, gcol)


def _merge_kernel(x_ref, ha_ref, hb_ref, ga_ref, gb_ref, wa_ref, wb_ref, wo_ref,
                  bga_ref, bgb_ref, npost_ref, x1_ref, *, sub):
    n_sub = x_ref.shape[0] // sub
    rows = [slice(r * sub, (r + 1) * sub) for r in range(n_sub)]

    def branches(r):
        return _dot(ha_ref[rows[r], :], wa_ref[...]), _dot(hb_ref[rows[r], :], wb_ref[...])

    def finish(r, z):
        x1_ref[rows[r], :] = x_ref[rows[r], :] + _rms(z) * npost_ref[...]

    y_next = branches(0)
    z_prev = None
    for r in range(n_sub):
        ya, yb = y_next
        if r + 1 < n_sub:
            y_next = branches(r + 1)
        merged = (jax.nn.sigmoid(ga_ref[rows[r], :].astype(f32) + bga_ref[...]) * ya
                  + jax.nn.sigmoid(gb_ref[rows[r], :].astype(f32) + bgb_ref[...]) * yb)
        z = _dot(merged.astype(bf16), wo_ref[...])
        if z_prev is not None:
            finish(r - 1, z_prev)
        z_prev = z
    finish(n_sub - 1, z_prev)


def _merge(x2, h_a, h_b, proj, wa, wb, wo, bga, bgb, npost, *, tm=1024, sub=256):
    t = x2.shape[0]
    tm = min(tm, t)
    tok = lambda i: (i, 0)
    const = lambda i: (0, 0)
    wspec = pl.BlockSpec((D_MODEL, D_MODEL), const, pipeline_mode=pl.Buffered(1))
    vspec = pl.BlockSpec((1, D_MODEL), const)
    return pl.pallas_call(
        functools.partial(_merge_kernel, sub=sub),
        grid=(t // tm,),
        in_specs=[
            pl.BlockSpec((tm, D_MODEL), tok),
            pl.BlockSpec((tm, ML_V), tok),
            pl.BlockSpec((tm, FOX_W), tok),
            pl.BlockSpec((tm, D_MODEL), lambda i: (i, COL_GA // D_MODEL)),
            pl.BlockSpec((tm, D_MODEL), lambda i: (i, COL_GB // D_MODEL)),
            wspec, wspec, wspec, vspec, vspec, vspec,
        ],
        out_specs=pl.BlockSpec((tm, D_MODEL), tok),
        out_shape=jax.ShapeDtypeStruct((t, D_MODEL), f32),
        compiler_params=pltpu.CompilerParams(
            dimension_semantics=("arbitrary",), vmem_limit_bytes=VMEM_LIMIT),
        name="merge",
    )(x2, h_a, h_b, proj, proj, wa, wb, wo, bga, bgb, npost)


def _ffn_kernel(x1_ref, npre_ref, wup_ref, cw_ref, cb_ref, wdn_ref, npost_ref, out_ref,
                xs_sc, xp_sc, halo_sc, ubuf_sc, tmp_sc, act_sc, acc_sc, ys_sc, *, blocks_per_seq, tf, down_group):
    i = pl.program_id(0)
    tm = x1_ref.shape[0]
    G = SUBLANES
    gr = tm // G
    n_lt = D_MODEL // LANES
    assert CONV_W - 1 <= G and gr % SUBLANES == 0

    @pl.when(i % blocks_per_seq == 0)
    def _():
        halo_sc[...] = jnp.zeros_like(halo_sc)

    for l in range(n_lt):
        xs_sc[l] = x1_ref[:, l * LANES:(l + 1) * LANES]
    for r in range(G):
        for l in range(n_lt):
            xp_sc[r * gr:(r + 1) * gr, l * LANES:(l + 1) * LANES] = xs_sc[l, pl.ds(r, gr, stride=G), :]
    h2 = (_rms(xp_sc[...]) * npre_ref[...]).astype(bf16)
    n_chunks = D_FF // tf
    n_sh = CONV_W - 1

    def up(c):
        for half in range(2):
            col0 = half * D_FF + c * tf
            slot = 2 * (c % 2) + half
            u = _dot(h2, wup_ref[:, col0:col0 + tf])
            ubuf_sc[slot, n_sh * gr:n_sh * gr + tm, :] = u
            for s in range(n_sh):
                g = G - n_sh + s
                tmp_sc[slot, s, 0:SUBLANES, :] = halo_sc[s, :, col0:col0 + tf]
                tmp_sc[slot, s, SUBLANES:SUBLANES + gr, :] = u[g * gr:(g + 1) * gr, :]
                ubuf_sc[slot, s * gr:(s + 1) * gr, :] = tmp_sc[slot, s, SUBLANES - 1:SUBLANES - 1 + gr, :]
                halo_sc[s, :, col0:col0 + tf] = u[(g + 1) * gr - SUBLANES:(g + 1) * gr, :]

    def conv(c, half):
        col0 = half * D_FF + c * tf
        slot = 2 * (c % 2) + half
        w = cw_ref[:, col0:col0 + tf]
        y = cb_ref[:, col0:col0 + tf]
        for j in range(CONV_W):
            y = y + w[j:j + 1] * ubuf_sc[slot, j * gr:j * gr + tm, :]
        return y

    def gate(g, a):
        inner = g * (GELU_K1 + (GELU_K1 * GELU_CUBIC) * (g * g))
        return (g * a) * (1.0 / (1.0 + jnp.exp2(inner)))

    def down(c0, c1):
        return _dot(act_sc[:, c0 * tf:c1 * tf], wdn_ref[c0 * tf:c1 * tf, :])

    up(0)
    pending, done = None, 0
    for c in range(n_chunks):
        if c + 1 < n_chunks:
            up(c + 1)
        if pending is not None:
            if pending[0] == 0:
                acc_sc[...] = down(*pending)
            else:
                acc_sc[...] += down(*pending)
            pending = None
        act_sc[:, c * tf:(c + 1) * tf] = gate(conv(c, 1), conv(c, 0)).astype(bf16)
        if (c + 1) % down_group == 0 or c + 1 == n_chunks:
            pending, done = (done, c + 1), c + 1

    y = down(*pending)
    if pending[0] != 0:
        y = y + acc_sc[...]
    o = xp_sc[...] + _rms(y) * npost_ref[...]

    pitch = gr + SUBLANES
    for r in range(G):
        for l in range(n_lt):
            ys_sc[l, r * pitch:r * pitch + gr, :] = o[r * gr:(r + 1) * gr, l * LANES:(l + 1) * LANES]

    def untangle(j, _):
        rows = pl.ds(pl.multiple_of(j * G, G), G)
        for l in range(n_lt):
            out_ref[rows, l * LANES:(l + 1) * LANES] = ys_sc[l, pl.ds(j, G, stride=pitch), :]
        return 0

    lax.fori_loop(0, gr, untangle, 0)


def _ffn(x1, npre, wup, cw, cb, wdn, npost, *, seq, tm=512, tf=256, down_group=2):
    t = x1.shape[0]
    tok = lambda i: (i, 0)
    const = lambda i: (0, 0)
    gr = tm // SUBLANES
    n_sh = CONV_W - 1
    kern = functools.partial(_ffn_kernel, blocks_per_seq=seq // tm, tf=tf, down_group=down_group)
    return pl.pallas_call(
        kern,
        grid=(t // tm,),
        in_specs=[
            pl.BlockSpec((tm, D_MODEL), tok),
            pl.BlockSpec((1, D_MODEL), const),
            pl.BlockSpec((D_MODEL, 2 * D_FF), const, pipeline_mode=pl.Buffered(1)),
            pl.BlockSpec((CONV_W, 2 * D_FF), const),
            pl.BlockSpec((1, 2 * D_FF), const),
            pl.BlockSpec((D_FF, D_MODEL), const, pipeline_mode=pl.Buffered(1)),
            pl.BlockSpec((1, D_MODEL), const),
        ],
        out_specs=pl.BlockSpec((tm, D_MODEL), tok),
        out_shape=jax.ShapeDtypeStruct((t, D_MODEL), f32),
        scratch_shapes=[pltpu.VMEM((D_MODEL // LANES, tm, LANES), f32),
                        pltpu.VMEM((tm, D_MODEL), f32),
                        pltpu.VMEM((n_sh, SUBLANES, 2 * D_FF), f32),
                        pltpu.VMEM((4, n_sh * gr + tm, tf), f32),
                        pltpu.VMEM((4, n_sh, SUBLANES + gr, tf), f32),
                        pltpu.VMEM((tm, D_FF), bf16),
                        pltpu.VMEM((tm, D_MODEL), f32),
                        pltpu.VMEM((D_MODEL // LANES, SUBLANES * (gr + SUBLANES), LANES), f32)],
        compiler_params=pltpu.CompilerParams(
            dimension_semantics=("arbitrary",), vmem_limit_bytes=VMEM_LIMIT),
        name="ffn",
    )(x1, npre, wup, cw, cb, wdn, npost)


def _layer(x2, p, l, *, batch, seq):
    w_in = p["w_in"][l]
    o_i = 2 * ML_QK + ML_V
    o_o = o_i + 2 * ML_HEADS
    o_qf = o_o + ML_V
    o_ff = o_qf + 3 * FOX_W
    o_ga = o_ff + FOX_HEADS
    w_pieces = [
        w_in[:, :o_i].astype(bf16),
        w_in[:, o_o:o_qf].astype(bf16),
        w_in[:, o_qf:o_ff].astype(bf16),
        w_in[:, o_ga:].astype(bf16),
    ]
    w_gate = jnp.concatenate([
        w_in[:, o_i:o_o], w_in[:, o_ff:o_ga],
        jnp.zeros((D_MODEL, GATE_W - N_GATES), f32)], axis=1).astype(bf16)
    b_gate = jnp.concatenate([
        p["b_ml_i"][l], p["b_ml_f"][l], p["b_fox_f"][l], jnp.zeros((GATE_W - N_GATES,), f32)])[None, :]
    tri = jnp.tril(jnp.ones((ML_CHUNK, ML_CHUNK), f32)).astype(bf16)
    row = lambda v: v[None, :].astype(f32)

    proj, gcol, grow = _inproj(x2, row(p["norm_mix_pre"][l]), w_pieces, w_gate, b_gate, tri, seq=seq)
    h_a = _mlstm(proj, gcol, grow, row(p["ml_head_norm"][l]), batch=batch, seq=seq)
    h_b = _fox(proj, gcol, batch=batch, seq=seq)
    x1 = _merge(x2, h_a, h_b, proj,
                p["w_branch_a"][l].astype(bf16), p["w_branch_b"][l].astype(bf16),
                p["w_out"][l].astype(bf16), row(p["b_gate_a"][l]), row(p["b_gate_b"][l]),
                row(p["norm_mix_post"][l]))
    return _ffn(x1, row(p["norm_ffn_pre"][l]), p["w_up"][l].astype(bf16), p["conv_w"][l], row(p["conv_b"][l]),
                p["w_down"][l].astype(bf16), row(p["norm_ffn_post"][l]), seq=seq)


def kernel(x, norm_mix_pre, w_in, b_ml_i, b_ml_f, ml_head_norm, b_fox_f, b_gate_a, b_gate_b,
           w_branch_a, w_branch_b, w_out, norm_mix_post, norm_ffn_pre, w_up, conv_w, conv_b,
           w_down, norm_ffn_post):
    batch, seq, _ = x.shape
    p = dict(norm_mix_pre=norm_mix_pre, w_in=w_in, b_ml_i=b_ml_i, b_ml_f=b_ml_f,
             ml_head_norm=ml_head_norm, b_fox_f=b_fox_f, b_gate_a=b_gate_a, b_gate_b=b_gate_b,
             w_branch_a=w_branch_a, w_branch_b=w_branch_b, w_out=w_out,
             norm_mix_post=norm_mix_post, norm_ffn_pre=norm_ffn_pre, w_up=w_up, conv_w=conv_w,
             conv_b=conv_b, w_down=w_down, norm_ffn_post=norm_ffn_post)
    x2 = x.reshape(batch * seq, D_MODEL)
    for l in range(w_in.shape[0]):
        x2 = _layer(x2, p, l, batch=batch, seq=seq)
    return x2.reshape(batch, seq, D_MODEL)
```

```python
import functools

import jax
import jax.numpy as jnp
from jax import lax
from jax.experimental import pallas as pl
from jax.experimental.pallas import tpu as pltpu

D_MODEL = 1024
ML_HEADS = 4
ML_DQK = 128
ML_DV = 256
ML_QK = ML_HEADS * ML_DQK
ML_V = ML_HEADS * ML_DV
FOX_HEADS = 8
FOX_DH = 128
FOX_W = FOX_HEADS * FOX_DH
D_FF = 2816
CONV_W = 3
GATE_CAP = 15.0
EPS = 1e-6
LOG2E = 1.4426950408889634
GELU_CUBIC = 0.044715
GELU_K1 = -2.0 * (2.0 / 3.141592653589793) ** 0.5 * LOG2E

LANES = 128
SUBLANES = 8
GATE_W = LANES
N_GATES = 2 * ML_HEADS + FOX_HEADS
ML_CHUNK = 128
VMEM_LIMIT = 56 * 1024 * 1024

PROJ_W = 2 * ML_QK + 2 * ML_V + 3 * FOX_W + 2 * D_MODEL
COL_QM, COL_KM, COL_VM, COL_OM = 0, ML_QK, 2 * ML_QK, 2 * ML_QK + ML_V
COL_QF = COL_OM + ML_V
COL_KF = COL_QF + FOX_W
COL_VF = COL_KF + FOX_W
COL_GA = COL_VF + FOX_W
COL_GB = COL_GA + D_MODEL

f32 = jnp.float32
bf16 = jnp.bfloat16


def _log_sigmoid(z):
    return jnp.minimum(z, 0.0) - jnp.log1p(jnp.exp(-jnp.abs(z)))


def _rms(v):
    return v * lax.rsqrt(jnp.mean(v * v, axis=-1, keepdims=True) + EPS)


def _dot(a, b):
    return jnp.dot(a, b, preferred_element_type=f32)


def _dot_nt(a, b):
    return lax.dot_general(a, b, (((1,), (1,)), ((), ())), preferred_element_type=f32)


def _dot_tn(a, b):
    return lax.dot_general(a, b, (((0,), (0,)), ((), ())), preferred_element_type=f32)


def _inproj_kernel(*refs, blocks_per_seq, piece, w_blocks):
    x_ref, nw_ref, wg_ref, bg_ref, tri_ref, cs_ref = refs[:6]
    w_refs = refs[6:6 + len(w_blocks)]
    proj_ref, gcol_ref, grow_ref, h_sc, carry_sc = refs[6 + len(w_blocks):]
    assert w_blocks[0][0] == 0
    i = pl.program_id(0)
    j = pl.program_id(1)
    tm = x_ref.shape[0]

    def project(hb, w_ref=w_refs[0]):
        return (_dot(hb, w_ref[...]) * cs_ref[...]).astype(bf16)

    @pl.when(j == 0)
    def _():
        @pl.when(i % blocks_per_seq == 0)
        def _():
            carry_sc[...] = jnp.zeros_like(carry_sc)

        tri = tri_ref[...]
        lane = lax.broadcasted_iota(jnp.int32, (ML_CHUNK, GATE_W), 1)
        per_piece = piece // ML_CHUNK

        def normed(p):
            prow = slice(p * piece, (p + 1) * piece)
            hb = (_rms(x_ref[prow, :]) * nw_ref[...]).astype(bf16)
            h_sc[prow, :] = hb
            return hb

        acts = []
        hb = normed(0)
        for p in range(tm // piece):
            proj_ref[p * piece:(p + 1) * piece, :] = project(hb)
            g_piece = _dot(hb, wg_ref[...]) + bg_ref[...]
            if (p + 1) * piece < tm:
                hb = normed(p + 1)
            for q in range(per_piece):
                g = g_piece[q * ML_CHUNK:(q + 1) * ML_CHUNK]
                cap = GATE_CAP * jnp.tanh(g / GATE_CAP)
                a = jnp.where(lane < ML_HEADS, cap,
                              jnp.where(lane < 2 * ML_HEADS, _log_sigmoid(cap), _log_sigmoid(g)))
                acts.append(jnp.where(lane < N_GATES, a, 0.0))

        for r in range(tm // ML_CHUNK):
            rows = slice(r * ML_CHUNK, (r + 1) * ML_CHUNK)
            a = acts[r]
            a_hi = a.astype(bf16)
            r1 = a - a_hi.astype(f32)
            a_mid = r1.astype(bf16)
            a_lo = (r1 - a_mid.astype(f32)).astype(bf16)
            cs3 = _dot(tri, jnp.concatenate([a_hi, a_mid, a_lo], axis=1))
            cs = cs3[:, :GATE_W] + cs3[:, GATE_W:2 * GATE_W] + cs3[:, 2 * GATE_W:]
            glob = cs + carry_sc[...]
            carry_sc[...] = glob[ML_CHUNK - 1:ML_CHUNK, :]
            out = jnp.where(lane < ML_HEADS, a, jnp.where(lane < 2 * ML_HEADS, cs, glob))
            gcol_ref[rows, :] = out
            grow_ref[:, rows] = out.T[:N_GATES, :]

    for w_ref, (first, count) in zip(w_refs, w_blocks):
        @pl.when((j >= max(first, 1)) & (j < first + count))
        def _(w_ref=w_ref):
            proj_ref[...] = project(h_sc[...], w_ref)


def _proj_col_scale():
    cs = jnp.ones((1, PROJ_W), f32)
    cs = cs.at[:, COL_QM:COL_QM + ML_QK].set(ML_DQK ** -0.5)
    return cs.at[:, COL_QF:COL_QF + FOX_W].set(LOG2E * FOX_DH ** -0.5)


def _inproj(x2, nw, w_pieces, w_gate, b_gate, tri, *, seq, tm=2048, tn=1024, piece=256):
    t = x2.shape[0]
    piece = max(piece, ML_CHUNK)
    w_blocks, first = [], 0
    for w in w_pieces:
        w_blocks.append((first, w.shape[1] // tn))
        first += w.shape[1] // tn
    assert first == PROJ_W // tn
    kern = functools.partial(_inproj_kernel, blocks_per_seq=seq // tm, piece=piece, w_blocks=tuple(w_blocks))

    def w_spec(first, count):
        return pl.BlockSpec((D_MODEL, tn), lambda i, j: (
            0, jnp.where(j < first - 1, count - 1, jnp.clip(j - first, 0, count - 1))))

    return pl.pallas_call(
        kern,
        grid=(t // tm, PROJ_W // tn),
        in_specs=[
            pl.BlockSpec((tm, D_MODEL), lambda i, j: (i, 0)),
            pl.BlockSpec((1, D_MODEL), lambda i, j: (0, 0)),
            pl.BlockSpec((D_MODEL, GATE_W), lambda i, j: (0, 0)),
            pl.BlockSpec((1, GATE_W), lambda i, j: (0, 0)),
            pl.BlockSpec((ML_CHUNK, ML_CHUNK), lambda i, j: (0, 0)),
            pl.BlockSpec((1, tn), lambda i, j: (0, j)),
        ] + [w_spec(f, n) for f, n in w_blocks],
        out_specs=[
            pl.BlockSpec((tm, tn), lambda i, j: (i, j)),
            pl.BlockSpec((tm, GATE_W), lambda i, j: (i, 0)),
            pl.BlockSpec((N_GATES, tm), lambda i, j: (0, i)),
        ],
        out_shape=[
            jax.ShapeDtypeStruct((t, PROJ_W), bf16),
            jax.ShapeDtypeStruct((t, GATE_W), f32),
            jax.ShapeDtypeStruct((N_GATES, t), f32),
        ],
        scratch_shapes=[pltpu.VMEM((tm, D_MODEL), bf16), pltpu.VMEM((1, GATE_W), f32)],
        compiler_params=pltpu.CompilerParams(
            dimension_semantics=("arbitrary", "arbitrary"), vmem_limit_bytes=VMEM_LIMIT),
        name="inproj",
    )(x2, nw, w_gate, b_gate, tri, _proj_col_scale(), *w_pieces)


def _mlstm_kernel(q_ref, k_ref, v_ref, o_ref, gcol_ref, grow_ref, hnw_ref, spread_ref, out_ref, s_sc, m_sc):
    c = pl.program_id(1)
    L = ML_CHUNK
    nb = q_ref.shape[0]
    assert L == LANES

    @pl.when(c == 0)
    def _():
        s_sc[...] = jnp.zeros_like(s_sc)
        m_sc[...] = jnp.zeros_like(m_sc)

    row = lax.broadcasted_iota(jnp.int32, (L, L), 0)
    col = lax.broadcasted_iota(jnp.int32, (L, L), 1)
    causal = col <= row
    ones_blk = jnp.ones((L, LANES), bf16)

    def wide(tile, n):
        return jnp.concatenate([tile] * n, axis=1)

    P = [(n, h) for n in range(nb) for h in range(ML_HEADS)]
    C = range(len(P))
    qk_cols = lambda h: slice(h * ML_DQK, (h + 1) * ML_DQK)
    v_cols = lambda h: slice(h * ML_DV, (h + 1) * ML_DV)
    spread = []
    for n in range(nb):
        g = gcol_ref[n]
        g_hi = g.astype(bf16)
        r1 = g - g_hi.astype(f32)
        g_mid = r1.astype(bf16)
        g_lo = (r1 - g_mid.astype(f32)).astype(bf16)
        spread.append(_dot(jnp.concatenate([g_hi, g_mid, g_lo], axis=1), spread_ref[...]))
    tile = lambda n, j: spread[n][:, j * LANES:(j + 1) * LANES]
    b_t = [tile(n, h) for n, h in P]
    a_t = [tile(n, ML_HEADS + h) for n, h in P]
    gr = [grow_ref[n] for n in range(nb)]
    a_r = [gr[n][h:h + 1, :] - gr[n][ML_HEADS + h:ML_HEADS + h + 1, :] for n, h in P]
    q = [q_ref[n, :, qk_cols(h)] for n, h in P]
    k = [k_ref[n, :, qk_cols(h)] for n, h in P]
    v_aug = [jnp.concatenate([v_ref[n, :, v_cols(h)], ones_blk], axis=1) for n, h in P]
    m_prev = [m_sc[i:i + 1, :] for i in C]
    state = [s_sc[i] for i in C]

    qk = [_dot_nt(q[i], k[i]) for i in C]
    inter_mm = [_dot(q[i], state[i].astype(bf16)) for i in C]
    amat = [jnp.where(causal, a_r[i], -jnp.inf) for i in C]
    mx = [jnp.maximum(m_prev[i], jnp.broadcast_to(jnp.max(amat[i], axis=-1, keepdims=True), (L, LANES)))
          for i in C]

    mx_last = [mx[i][L - 1:L, :] for i in C]
    b_last = [b_t[i][L - 1:L, :] for i in C]
    wk = [jnp.exp(a_t[i] - mx_last[i]) for i in C]
    upd = [_dot_tn((wk[i] * k[i].astype(f32)).astype(bf16), v_aug[i]) for i in C]

    s = [(qk[i] * jnp.exp(amat[i] - mx[i])).astype(bf16) for i in C]
    w_inter = [jnp.exp(m_prev[i] - mx[i]) for i in C]
    num = [_dot(s[i], v_aug[i]) + wide(w_inter[i], 3) * inter_mm[i] for i in C]
    for i in C:
        s_sc[i] = wide(jnp.exp(m_prev[i] - mx_last[i]), 3) * state[i] + upd[i]
        m_sc[i:i + 1, :] = b_last[i] + mx_last[i]

    inv = [1.0 / jnp.maximum(jnp.abs(num[i][:, ML_DV:]), jnp.exp(-b_t[i] - mx[i])) for i in C]
    msq = [jnp.broadcast_to(jnp.mean(num[i][:, :ML_DV] * num[i][:, :ML_DV], axis=-1, keepdims=True),
                            (L, LANES)) for i in C]
    fac = [inv[i] * lax.rsqrt(inv[i] * inv[i] * msq[i] + EPS) for i in C]
    for i, (n, h) in enumerate(P):
        gate = jax.nn.sigmoid(o_ref[n, :, v_cols(h)].astype(f32)) * hnw_ref[:, v_cols(h)]
        out_ref[n, :, v_cols(h)] = (num[i][:, :ML_DV] * wide(fac[i], ML_DV // LANES) * gate).astype(bf16)


def _mlstm(proj, gcol, grow, hnw, *, batch, seq, nb=4):
    while batch % nb:
        nb //= 2
    nc = seq // ML_CHUNK
    L = ML_CHUNK
    proj4 = proj.reshape(batch // nb, nb, seq, PROJ_W)
    gcol4 = gcol.reshape(batch // nb, nb, seq, GATE_W)
    grow4 = grow.reshape(N_GATES, batch // nb, nb, seq).transpose(1, 2, 0, 3)
    src = lax.broadcasted_iota(jnp.int32, (GATE_W, 2 * ML_HEADS * LANES), 0)
    dst = lax.broadcasted_iota(jnp.int32, (GATE_W, 2 * ML_HEADS * LANES), 1) // LANES
    spread = (jnp.where(src == ML_HEADS + dst % ML_HEADS, jnp.where(dst < ML_HEADS, 1.0, -1.0), 0.0)
              + jnp.where((dst >= ML_HEADS) & (src == dst - ML_HEADS), 1.0, 0.0)).astype(bf16)
    spread = jnp.concatenate([spread] * 3, axis=0)
    out = pl.pallas_call(
        _mlstm_kernel,
        grid=(batch // nb, nc),
        in_specs=[
            pl.BlockSpec((None, nb, L, ML_QK), lambda b, c: (b, 0, c, COL_QM // ML_QK)),
            pl.BlockSpec((None, nb, L, ML_QK), lambda b, c: (b, 0, c, COL_KM // ML_QK)),
            pl.BlockSpec((None, nb, L, ML_V), lambda b, c: (b, 0, c, COL_VM // ML_V)),
            pl.BlockSpec((None, nb, L, ML_V), lambda b, c: (b, 0, c, COL_OM // ML_V)),
            pl.BlockSpec((None, nb, L, GATE_W), lambda b, c: (b, 0, c, 0)),
            pl.BlockSpec((None, nb, N_GATES, L), lambda b, c: (b, 0, 0, c)),
            pl.BlockSpec((1, ML_V), lambda b, c: (0, 0)),
            pl.BlockSpec((3 * GATE_W, 2 * ML_HEADS * LANES), lambda b, c: (0, 0)),
        ],
        out_specs=pl.BlockSpec((None, nb, L, ML_V), lambda b, c: (b, 0, c, 0)),
        out_shape=jax.ShapeDtypeStruct((batch // nb, nb, seq, ML_V), bf16),
        scratch_shapes=[pltpu.VMEM((nb * ML_HEADS, ML_DQK, ML_DV + LANES), f32),
                        pltpu.VMEM((nb * ML_HEADS, LANES), f32)],
        compiler_params=pltpu.CompilerParams(
            dimension_semantics=("arbitrary", "arbitrary"), vmem_limit_bytes=VMEM_LIMIT),
        name="mlstm",
    )(proj4, proj4, proj4, proj4, gcol4, grow4, hnw, spread)
    return out.reshape(batch * seq, ML_V)


def _fox_kernel(q_ref, k_ref, v_ref, gcol_ref, out_ref, kaug_sc, st_sc, mx_sc, m_sc, l_sc, acc_sc,
                *, blk, setup_rows, ahead):
    hg = pl.program_id(1)
    qi = pl.program_id(2)
    seq = k_ref.shape[0]
    nh = kaug_sc.shape[0]
    head_cols = [slice(i * FOX_DH, (i + 1) * FOX_DH) for i in range(nh)]

    @pl.when(qi == 0)
    def _():
        r = lax.broadcasted_iota(jnp.int32, (GATE_W, nh * LANES), 0)
        c = lax.broadcasted_iota(jnp.int32, (GATE_W, nh * LANES), 1)
        src = 2 * ML_HEADS + hg * nh + c // LANES
        pick = jnp.concatenate([((r == src) & (c % LANES == j)).astype(bf16) for j in range(3)], axis=0)
        for n in range(seq // setup_rows):
            rows = pl.ds(n * setup_rows, setup_rows)
            a = gcol_ref[rows, :] * (-LOG2E)
            a_hi = a.astype(bf16)
            r1 = a - a_hi.astype(f32)
            a_mid = r1.astype(bf16)
            a_lo = (r1 - a_mid.astype(f32)).astype(bf16)
            aug = _dot(jnp.concatenate([a_hi, a_mid, a_lo], axis=1), pick).astype(bf16)
            for i in range(nh):
                kaug_sc[i, rows, :FOX_DH] = k_ref[rows, head_cols[i]]
                kaug_sc[i, rows, FOX_DH:] = aug[:, i * LANES:(i + 1) * LANES]

    ones3 = (lax.broadcasted_iota(jnp.int32, (blk, LANES), 1) < 3).astype(bf16)
    q_aug = [jnp.concatenate([q_ref[:, head_cols[i]], ones3], axis=1) for i in range(nh)]

    m_sc[...] = jnp.full(m_sc.shape, -jnp.inf, f32)
    l_sc[...] = jnp.zeros(l_sc.shape, f32)
    acc_sc[...] = jnp.zeros(acc_sc.shape, f32)

    def logits_stage(i, j, masked):
        start = pl.multiple_of(j * blk, blk)
        st = _dot_nt(kaug_sc[i, pl.ds(start, blk), :], q_aug[i])
        if masked:
            row = lax.broadcasted_iota(jnp.int32, (blk, blk), 0)
            col = lax.broadcasted_iota(jnp.int32, (blk, blk), 1)
            st = jnp.where(row <= col, st, -jnp.inf)
        st_sc[i] = st
        mx_sc[i] = jnp.max(st, axis=0, keepdims=True)

    def softmax_stage(i, j):
        start = pl.multiple_of(j * blk, blk)
        m = m_sc[i]
        m_new = jnp.maximum(m, mx_sc[i])
        alpha = jnp.exp2(m - m_new)
        p = jnp.exp2(st_sc[i] - m_new)
        l_sc[i] = alpha * l_sc[i] + jnp.sum(p, axis=0, keepdims=True)
        acc_sc[i] = alpha * acc_sc[i] + _dot_tn(v_ref[pl.ds(start, blk), head_cols[i]], p.astype(bf16))
        m_sc[i] = m_new

    def round_robin(cur, nxt, masked):
        for i in range(nh):
            if i + ahead < nh:
                logits_stage(i + ahead, cur, masked)
            elif nxt is not None:
                logits_stage(i + ahead - nh, nxt, False)
            softmax_stage(i, cur)

    for i in range(ahead):
        logits_stage(i, qi, True)
    round_robin(qi, 0, True)

    def body(k, _):
        round_robin(k - 1, jnp.minimum(k, qi - 1), False)
        return 0

    lax.fori_loop(1, qi + 1, body, 0)

    for i in range(nh):
        out_ref[:, head_cols[i]] = (acc_sc[i] * (1.0 / l_sc[i])).T.astype(bf16)


def _fox(proj, gcol, *, batch, seq, blk=512, nh=4, ahead=1):
    assert 1 <= ahead < nh
    t = proj.shape[0]
    nq = seq // blk
    w = nh * FOX_DH
    kern = functools.partial(_fox_kernel, blk=blk, setup_rows=min(seq, 512), ahead=ahead)
    return pl.pallas_call(
        kern,
        grid=(batch, FOX_HEADS // nh, nq),
        in_specs=[
            pl.BlockSpec((blk, w), lambda b, h, i: (b * nq + i, COL_QF // w + h)),
            pl.BlockSpec((seq, w), lambda b, h, i: (b, COL_KF // w + h)),
            pl.BlockSpec((seq, w), lambda b, h, i: (b, COL_VF // w + h)),
            pl.BlockSpec((seq, GATE_W), lambda b, h, i: (b, 0)),
        ],
        out_specs=pl.BlockSpec((blk, w), lambda b, h, i: (b * nq + i, h)),
        out_shape=jax.ShapeDtypeStruct((t, FOX_W), bf16),
        scratch_shapes=[pltpu.VMEM((nh, seq, FOX_DH + LANES), bf16),
                        pltpu.VMEM((nh, blk, blk), f32),
                        pltpu.VMEM((nh, 1, blk), f32),
                        pltpu.VMEM((nh, 1, blk), f32),
                        pltpu.VMEM((nh, 1, blk), f32),
                        pltpu.VMEM((nh, FOX_DH, blk), f32)],
        compiler_params=pltpu.CompilerParams(
            dimension_semantics=("arbitrary", "arbitrary", "arbitrary"), vmem_limit_bytes=VMEM_LIMIT),
        name="fox",
    )(proj, proj, proj, gcol)


def _merge_kernel(x_ref, ha_ref, hb_ref, ga_ref, gb_ref, wa_ref, wb_ref, wo_ref,
                  bga_ref, bgb_ref, npost_ref, npre_ref, x1_ref, h2_ref, *, sub):
    n_sub = x_ref.shape[0] // sub
    rows = [slice(r * sub, (r + 1) * sub) for r in range(n_sub)]

    def branches(r):
        return _dot(ha_ref[rows[r], :], wa_ref[...]), _dot(hb_ref[rows[r], :], wb_ref[...])

    def finish(r, z):
        x1 = x_ref[rows[r], :] + _rms(z) * npost_ref[...]
        x1_ref[rows[r], :] = x1
        h2_ref[rows[r], :] = (_rms(x1) * npre_ref[...]).astype(bf16)

    y_next = branches(0)
    z_prev = None
    for r in range(n_sub):
        ya, yb = y_next
        if r + 1 < n_sub:
            y_next = branches(r + 1)
        merged = (jax.nn.sigmoid(ga_ref[rows[r], :].astype(f32) + bga_ref[...]) * ya
                  + jax.nn.sigmoid(gb_ref[rows[r], :].astype(f32) + bgb_ref[...]) * yb)
        z = _dot(merged.astype(bf16), wo_ref[...])
        if z_prev is not None:
            finish(r - 1, z_prev)
        z_prev = z
    finish(n_sub - 1, z_prev)


def _merge(x2, h_a, h_b, proj, wa, wb, wo, bga, bgb, npost, npre, *, tm=1024, sub=256):
    t = x2.shape[0]
    tm = min(tm, t)
    tok = lambda i: (i, 0)
    const = lambda i: (0, 0)
    wspec = pl.BlockSpec((D_MODEL, D_MODEL), const, pipeline_mode=pl.Buffered(1))
    vspec = pl.BlockSpec((1, D_MODEL), const)
    return pl.pallas_call(
        functools.partial(_merge_kernel, sub=sub),
        grid=(t // tm,),
        in_specs=[
            pl.BlockSpec((tm, D_MODEL), tok),
            pl.BlockSpec((tm, ML_V), tok),
            pl.BlockSpec((tm, FOX_W), tok),
            pl.BlockSpec((tm, D_MODEL), lambda i: (i, COL_GA // D_MODEL)),
            pl.BlockSpec((tm, D_MODEL), lambda i: (i, COL_GB // D_MODEL)),
            wspec, wspec, wspec, vspec, vspec, vspec, vspec,
        ],
        out_specs=[pl.BlockSpec((tm, D_MODEL), tok), pl.BlockSpec((tm, D_MODEL), tok)],
        out_shape=[jax.ShapeDtypeStruct((t, D_MODEL), f32), jax.ShapeDtypeStruct((t, D_MODEL), bf16)],
        compiler_params=pltpu.CompilerParams(
            dimension_semantics=("arbitrary",), vmem_limit_bytes=VMEM_LIMIT),
        name="merge",
    )(x2, h_a, h_b, proj, proj, wa, wb, wo, bga, bgb, npost, npre)


def _ffn_kernel(x1_ref, h2_ref, wup_ref, cw_ref, cb_ref, wdn_ref, npost_ref, out_ref,
                halo_sc, ubuf_sc, act_sc, acc_sc, *, blocks_per_seq, tf, down_group):
    i = pl.program_id(0)
    tm = x1_ref.shape[0]

    @pl.when(i % blocks_per_seq == 0)
    def _():
        halo_sc[...] = jnp.zeros_like(halo_sc)

    h2 = h2_ref[...]
    n_chunks = D_FF // tf

    def up(c):
        for half in range(2):
            col0 = half * D_FF + c * tf
            slot = 2 * (c % 2) + half
            u = _dot(h2, wup_ref[:, col0:col0 + tf])
            ubuf_sc[slot, 0:SUBLANES, :] = halo_sc[:, col0:col0 + tf]
            ubuf_sc[slot, SUBLANES:SUBLANES + tm, :] = u
            halo_sc[:, col0:col0 + tf] = u[tm - SUBLANES:tm, :]

    def conv(c, half):
        col0 = half * D_FF + c * tf
        slot = 2 * (c % 2) + half
        w = cw_ref[:, col0:col0 + tf]
        y = cb_ref[:, col0:col0 + tf]
        for j in range(CONV_W):
            first = SUBLANES - (CONV_W - 1 - j)
            y = y + w[j:j + 1] * ubuf_sc[slot, first:first + tm, :]
        return y

    def gate(g, a):
        inner = g * (GELU_K1 + (GELU_K1 * GELU_CUBIC) * (g * g))
        return (g * a) * (1.0 / (1.0 + jnp.exp2(inner)))

    def down(c0, c1):
        return _dot(act_sc[:, c0 * tf:c1 * tf], wdn_ref[c0 * tf:c1 * tf, :])

    up(0)
    pending, done = None, 0
    for c in range(n_chunks):
        if c + 1 < n_chunks:
            up(c + 1)
        if pending is not None:
            if pending[0] == 0:
                acc_sc[...] = down(*pending)
            else:
                acc_sc[...] += down(*pending)
            pending = None
        act_sc[:, c * tf:(c + 1) * tf] = gate(conv(c, 1), conv(c, 0)).astype(bf16)
        if (c + 1) % down_group == 0 or c + 1 == n_chunks:
            pending, done = (done, c + 1), c + 1

    y = down(*pending)
    if pending[0] != 0:
        y = y + acc_sc[...]
    out_ref[...] = x1_ref[...] + _rms(y) * npost_ref[...]


def _ffn(x1, h2, wup, cw, cb, wdn, npost, *, seq, tm=512, tf=256, down_group=2):
    t = x1.shape[0]
    tok = lambda i: (i, 0)
    const = lambda i: (0, 0)
    kern = functools.partial(_ffn_kernel, blocks_per_seq=seq // tm, tf=tf, down_group=down_group)
    return pl.pallas_call(
        kern,
        grid=(t // tm,),
        in_specs=[
            pl.BlockSpec((tm, D_MODEL), tok),
            pl.BlockSpec((tm, D_MODEL), tok),
            pl.BlockSpec((D_MODEL, 2 * D_FF), const, pipeline_mode=pl.Buffered(1)),
            pl.BlockSpec((CONV_W, 2 * D_FF), const),
            pl.BlockSpec((1, 2 * D_FF), const),
            pl.BlockSpec((D_FF, D_MODEL), const, pipeline_mode=pl.Buffered(1)),
            pl.BlockSpec((1, D_MODEL), const),
        ],
        out_specs=pl.BlockSpec((tm, D_MODEL), tok),
        out_shape=jax.ShapeDtypeStruct((t, D_MODEL), f32),
        scratch_shapes=[pltpu.VMEM((SUBLANES, 2 * D_FF), f32),
                        pltpu.VMEM((4, tm + SUBLANES, tf), f32),
                        pltpu.VMEM((tm, D_FF), bf16),
                        pltpu.VMEM((tm, D_MODEL), f32)],
        compiler_params=pltpu.CompilerParams(
            dimension_semantics=("arbitrary",), vmem_limit_bytes=VMEM_LIMIT),
        name="ffn",
    )(x1, h2, wup, cw, cb, wdn, npost)


def _layer(x2, p, l, *, batch, seq):
    w_in = p["w_in"][l]
    o_i = 2 * ML_QK + ML_V
    o_o = o_i + 2 * ML_HEADS
    o_qf = o_o + ML_V
    o_ff = o_qf + 3 * FOX_W
    o_ga = o_ff + FOX_HEADS
    w_pieces = [
        w_in[:, :o_i].astype(bf16),
        w_in[:, o_o:o_qf].astype(bf16),
        w_in[:, o_qf:o_ff].astype(bf16),
        w_in[:, o_ga:].astype(bf16),
    ]
    w_gate = jnp.concatenate([
        w_in[:, o_i:o_o], w_in[:, o_ff:o_ga],
        jnp.zeros((D_MODEL, GATE_W - N_GATES), f32)], axis=1).astype(bf16)
    b_gate = jnp.concatenate([
        p["b_ml_i"][l], p["b_ml_f"][l], p["b_fox_f"][l], jnp.zeros((GATE_W - N_GATES,), f32)])[None, :]
    tri = jnp.tril(jnp.ones((ML_CHUNK, ML_CHUNK), f32)).astype(bf16)
    row = lambda v: v[None, :].astype(f32)

    proj, gcol, grow = _inproj(x2, row(p["norm_mix_pre"][l]), w_pieces, w_gate, b_gate, tri, seq=seq)
    h_a = _mlstm(proj, gcol, grow, row(p["ml_head_norm"][l]), batch=batch, seq=seq)
    h_b = _fox(proj, gcol, batch=batch, seq=seq)
    x1, h2 = _merge(x2, h_a, h_b, proj,
                    p["w_branch_a"][l].astype(bf16), p["w_branch_b"][l].astype(bf16),
                    p["w_out"][l].astype(bf16), row(p["b_gate_a"][l]), row(p["b_gate_b"][l]),
                    row(p["norm_mix_post"][l]), row(p["norm_ffn_pre"][l]))
    return _ffn(x1, h2, p["w_up"][l].astype(bf16), p["conv_w"][l], row(p["conv_b"][l]),
                p["w_down"][l].astype(bf16), row(p["norm_ffn_post"][l]), seq=seq)


def kernel(x, norm_mix_pre, w_in, b_ml_i, b_ml_f, ml_head_norm, b_fox_f, b_gate_a, b_gate_b,
           w_branch_a, w_branch_b, w_out, norm_mix_post, norm_ffn_pre, w_up, conv_w, conv_b,
           w_down, norm_ffn_post):
    batch, seq, _ = x.shape
    p = dict(norm_mix_pre=norm_mix_pre, w_in=w_in, b_ml_i=b_ml_i, b_ml_f=b_ml_f,
             ml_head_norm=ml_head_norm, b_fox_f=b_fox_f, b_gate_a=b_gate_a, b_gate_b=b_gate_b,
             w_branch_a=w_branch_a, w_branch_b=w_branch_b, w_out=w_out,
             norm_mix_post=norm_mix_post, norm_ffn_pre=norm_ffn_pre, w_up=w_up, conv_w=conv_w,
             conv_b=conv_b, w_down=w_down, norm_ffn_post=norm_ffn_post)
    x2 = x.reshape(batch * seq, D_MODEL)
    for l in range(w_in.shape[0]):
        x2 = _layer(x2, p, l, batch=batch, seq=seq)
    return x2.reshape(batch, seq, D_MODEL)
```

```python
import functools

import jax
import jax.numpy as jnp
from jax import lax
from jax.experimental import pallas as pl
from jax.experimental.pallas import tpu as pltpu

D_MODEL = 1024
ML_HEADS = 4
ML_DQK = 128
ML_DV = 256
ML_QK = ML_HEADS * ML_DQK
ML_V = ML_HEADS * ML_DV
FOX_HEADS = 8
FOX_DH = 128
FOX_W = FOX_HEADS * FOX_DH
D_FF = 2816
CONV_W = 3
GATE_CAP = 15.0
EPS = 1e-6
LOG2E = 1.4426950408889634
GELU_CUBIC = 0.044715
GELU_K1 = -2.0 * (2.0 / 3.141592653589793) ** 0.5 * LOG2E

LANES = 128
SUBLANES = 8
GATE_W = LANES
N_GATES = 2 * ML_HEADS + FOX_HEADS
ML_CHUNK = 128
VMEM_LIMIT = 56 * 1024 * 1024

PROJ_W = 2 * ML_QK + 2 * ML_V + 3 * FOX_W + 2 * D_MODEL
COL_QM, COL_KM, COL_VM, COL_OM = 0, ML_QK, 2 * ML_QK, 2 * ML_QK + ML_V
COL_QF = COL_OM + ML_V
COL_KF = COL_QF + FOX_W
COL_VF = COL_KF + FOX_W
COL_GA = COL_VF + FOX_W
COL_GB = COL_GA + D_MODEL

f32 = jnp.float32
bf16 = jnp.bfloat16


def _log_sigmoid(z):
    return jnp.minimum(z, 0.0) - jnp.log1p(jnp.exp(-jnp.abs(z)))


def _rms(v):
    return v * lax.rsqrt(jnp.mean(v * v, axis=-1, keepdims=True) + EPS)


def _dot(a, b):
    return jnp.dot(a, b, preferred_element_type=f32)


def _dot_nt(a, b):
    return lax.dot_general(a, b, (((1,), (1,)), ((), ())), preferred_element_type=f32)


def _dot_tn(a, b):
    return lax.dot_general(a, b, (((0,), (0,)), ((), ())), preferred_element_type=f32)


def _inproj_kernel(*refs, blocks_per_seq, piece, w_blocks):
    x_ref, nw_ref, wg_ref, bg_ref, tri_ref, cs_ref = refs[:6]
    w_refs = refs[6:6 + len(w_blocks)]
    proj_ref, gcol_ref, grow_ref, h_sc, carry_sc = refs[6 + len(w_blocks):]
    assert w_blocks[0][0] == 0
    i = pl.program_id(0)
    j = pl.program_id(1)
    tm = x_ref.shape[0]

    def project(hb, w_ref=w_refs[0]):
        return (_dot(hb, w_ref[...]) * cs_ref[...]).astype(bf16)

    @pl.when(j == 0)
    def _():
        @pl.when(i % blocks_per_seq == 0)
        def _():
            carry_sc[...] = jnp.zeros_like(carry_sc)

        tri = tri_ref[...]
        lane = lax.broadcasted_iota(jnp.int32, (ML_CHUNK, GATE_W), 1)
        per_piece = piece // ML_CHUNK

        def normed(p):
            prow = slice(p * piece, (p + 1) * piece)
            hb = (_rms(x_ref[prow, :]) * nw_ref[...]).astype(bf16)
            h_sc[prow, :] = hb
            return hb

        acts = []
        hb = normed(0)
        for p in range(tm // piece):
            proj_ref[p * piece:(p + 1) * piece, :] = project(hb)
            g_piece = _dot(hb, wg_ref[...]) + bg_ref[...]
            if (p + 1) * piece < tm:
                hb = normed(p + 1)
            for q in range(per_piece):
                g = g_piece[q * ML_CHUNK:(q + 1) * ML_CHUNK]
                cap = GATE_CAP * jnp.tanh(g / GATE_CAP)
                a = jnp.where(lane < ML_HEADS, cap,
                              jnp.where(lane < 2 * ML_HEADS, _log_sigmoid(cap), _log_sigmoid(g)))
                acts.append(jnp.where(lane < N_GATES, a, 0.0))

        for r in range(tm // ML_CHUNK):
            rows = slice(r * ML_CHUNK, (r + 1) * ML_CHUNK)
            a = acts[r]
            a_hi = a.astype(bf16)
            r1 = a - a_hi.astype(f32)
            a_mid = r1.astype(bf16)
            a_lo = (r1 - a_mid.astype(f32)).astype(bf16)
            cs3 = _dot(tri, jnp.concatenate([a_hi, a_mid, a_lo], axis=1))
            cs = cs3[:, :GATE_W] + cs3[:, GATE_W:2 * GATE_W] + cs3[:, 2 * GATE_W:]
            glob = cs + carry_sc[...]
            carry_sc[...] = glob[ML_CHUNK - 1:ML_CHUNK, :]
            out = jnp.where(lane < ML_HEADS, a, jnp.where(lane < 2 * ML_HEADS, cs, glob))
            gcol_ref[rows, :] = out
            grow_ref[:, rows] = out.T[:N_GATES, :]

    for w_ref, (first, count) in zip(w_refs, w_blocks):
        @pl.when((j >= max(first, 1)) & (j < first + count))
        def _(w_ref=w_ref):
            proj_ref[...] = project(h_sc[...], w_ref)


def _proj_col_scale():
    cs = jnp.ones((1, PROJ_W), f32)
    cs = cs.at[:, COL_QM:COL_QM + ML_QK].set(ML_DQK ** -0.5)
    return cs.at[:, COL_QF:COL_QF + FOX_W].set(LOG2E * FOX_DH ** -0.5)


def _inproj(x2, nw, w_pieces, w_gate, b_gate, tri, *, seq, tm=2048, tn=1024, piece=512):
    t = x2.shape[0]
    tm = min(tm, seq)
    assert seq % tm == 0 and tm % piece == 0
    piece = max(piece, ML_CHUNK)
    w_blocks, first = [], 0
    for w in w_pieces:
        w_blocks.append((first, w.shape[1] // tn))
        first += w.shape[1] // tn
    assert first == PROJ_W // tn
    kern = functools.partial(_inproj_kernel, blocks_per_seq=seq // tm, piece=piece, w_blocks=tuple(w_blocks))

    def w_spec(first, count):
        return pl.BlockSpec((D_MODEL, tn), lambda i, j: (
            0, jnp.where(j < first - 1, count - 1, jnp.clip(j - first, 0, count - 1))))

    return pl.pallas_call(
        kern,
        grid=(t // tm, PROJ_W // tn),
        in_specs=[
            pl.BlockSpec((tm, D_MODEL), lambda i, j: (i, 0)),
            pl.BlockSpec((1, D_MODEL), lambda i, j: (0, 0)),
            pl.BlockSpec((D_MODEL, GATE_W), lambda i, j: (0, 0)),
            pl.BlockSpec((1, GATE_W), lambda i, j: (0, 0)),
            pl.BlockSpec((ML_CHUNK, ML_CHUNK), lambda i, j: (0, 0)),
            pl.BlockSpec((1, tn), lambda i, j: (0, j)),
        ] + [w_spec(f, n) for f, n in w_blocks],
        out_specs=[
            pl.BlockSpec((tm, tn), lambda i, j: (i, j)),
            pl.BlockSpec((tm, GATE_W), lambda i, j: (i, 0)),
            pl.BlockSpec((N_GATES, tm), lambda i, j: (0, i)),
        ],
        out_shape=[
            jax.ShapeDtypeStruct((t, PROJ_W), bf16),
            jax.ShapeDtypeStruct((t, GATE_W), f32),
            jax.ShapeDtypeStruct((N_GATES, t), f32),
        ],
        scratch_shapes=[pltpu.VMEM((tm, D_MODEL), bf16), pltpu.VMEM((1, GATE_W), f32)],
        compiler_params=pltpu.CompilerParams(
            dimension_semantics=("arbitrary", "arbitrary"), vmem_limit_bytes=VMEM_LIMIT),
        name="inproj",
    )(x2, nw, w_gate, b_gate, tri, _proj_col_scale(), *w_pieces)


def _mlstm_kernel(q_ref, k_ref, v_ref, o_ref, gcol_ref, grow_ref, hnw_ref, spread_ref, out_ref, s_sc, m_sc):
    c = pl.program_id(1)
    L = ML_CHUNK
    nb = q_ref.shape[0]
    assert L == LANES

    @pl.when(c == 0)
    def _():
        s_sc[...] = jnp.zeros_like(s_sc)
        m_sc[...] = jnp.zeros_like(m_sc)

    row = lax.broadcasted_iota(jnp.int32, (L, L), 0)
    col = lax.broadcasted_iota(jnp.int32, (L, L), 1)
    causal = col <= row
    ones_blk = jnp.ones((L, LANES), bf16)

    def wide(tile, n):
        return jnp.concatenate([tile] * n, axis=1)

    P = [(n, h) for n in range(nb) for h in range(ML_HEADS)]
    C = range(len(P))
    qk_cols = lambda h: slice(h * ML_DQK, (h + 1) * ML_DQK)
    v_cols = lambda h: slice(h * ML_DV, (h + 1) * ML_DV)
    spread = []
    for n in range(nb):
        g = gcol_ref[n]
        g_hi = g.astype(bf16)
        r1 = g - g_hi.astype(f32)
        g_mid = r1.astype(bf16)
        g_lo = (r1 - g_mid.astype(f32)).astype(bf16)
        spread.append(_dot(jnp.concatenate([g_hi, g_mid, g_lo], axis=1), spread_ref[...]))
    tile = lambda n, j: spread[n][:, j * LANES:(j + 1) * LANES]
    b_t = [tile(n, h) for n, h in P]
    a_t = [tile(n, ML_HEADS + h) for n, h in P]
    gr = [grow_ref[n] for n in range(nb)]
    a_r = [gr[n][h:h + 1, :] - gr[n][ML_HEADS + h:ML_HEADS + h + 1, :] for n, h in P]
    q = [q_ref[n, :, qk_cols(h)] for n, h in P]
    k = [k_ref[n, :, qk_cols(h)] for n, h in P]
    v_aug = [jnp.concatenate([v_ref[n, :, v_cols(h)], ones_blk], axis=1) for n, h in P]
    m_prev = [m_sc[i:i + 1, :] for i in C]
    state = [s_sc[i] for i in C]

    qk = [_dot_nt(q[i], k[i]) for i in C]
    amat = [jnp.where(causal, a_r[i], -jnp.inf) for i in C]
    mx = [jnp.maximum(m_prev[i], jnp.broadcast_to(jnp.max(amat[i], axis=-1, keepdims=True), (L, LANES)))
          for i in C]

    mx_last = [mx[i][L - 1:L, :] for i in C]
    b_last = [b_t[i][L - 1:L, :] for i in C]
    wk = [jnp.exp(a_t[i] - mx_last[i]) for i in C]
    upd = [_dot_tn((wk[i] * k[i].astype(f32)).astype(bf16), v_aug[i]) for i in C]

    s = [(qk[i] * jnp.exp(amat[i] - mx[i])).astype(bf16) for i in C]
    q_in = [(jnp.exp(m_prev[i] - mx[i]) * q[i].astype(f32)).astype(bf16) for i in C]
    num = [_dot(jnp.concatenate([s[i], q_in[i]], axis=1),
                jnp.concatenate([v_aug[i], state[i].astype(bf16)], axis=0)) for i in C]
    for i in C:
        s_sc[i] = wide(jnp.exp(m_prev[i] - mx_last[i]), 3) * state[i] + upd[i]
        m_sc[i:i + 1, :] = b_last[i] + mx_last[i]

    inv = [1.0 / jnp.maximum(jnp.abs(num[i][:, ML_DV:]), jnp.exp(-b_t[i] - mx[i])) for i in C]
    msq = [jnp.broadcast_to(jnp.mean(num[i][:, :ML_DV] * num[i][:, :ML_DV], axis=-1, keepdims=True),
                            (L, LANES)) for i in C]
    fac = [inv[i] * lax.rsqrt(inv[i] * inv[i] * msq[i] + EPS) for i in C]
    for i, (n, h) in enumerate(P):
        gate = jax.nn.sigmoid(o_ref[n, :, v_cols(h)].astype(f32)) * hnw_ref[:, v_cols(h)]
        out_ref[n, :, v_cols(h)] = (num[i][:, :ML_DV] * wide(fac[i], ML_DV // LANES) * gate).astype(bf16)


def _mlstm(proj, gcol, grow, hnw, *, batch, seq, nb=4):
    while batch % nb:
        nb //= 2
    nc = seq // ML_CHUNK
    L = ML_CHUNK
    proj4 = proj.reshape(batch // nb, nb, seq, PROJ_W)
    gcol4 = gcol.reshape(batch // nb, nb, seq, GATE_W)
    grow4 = grow.reshape(N_GATES, batch // nb, nb, seq).transpose(1, 2, 0, 3)
    src = lax.broadcasted_iota(jnp.int32, (GATE_W, 2 * ML_HEADS * LANES), 0)
    dst = lax.broadcasted_iota(jnp.int32, (GATE_W, 2 * ML_HEADS * LANES), 1) // LANES
    spread = (jnp.where(src == ML_HEADS + dst % ML_HEADS, jnp.where(dst < ML_HEADS, 1.0, -1.0), 0.0)
              + jnp.where((dst >= ML_HEADS) & (src == dst - ML_HEADS), 1.0, 0.0)).astype(bf16)
    spread = jnp.concatenate([spread] * 3, axis=0)
    out = pl.pallas_call(
        _mlstm_kernel,
        grid=(batch // nb, nc),
        in_specs=[
            pl.BlockSpec((None, nb, L, ML_QK), lambda b, c: (b, 0, c, COL_QM // ML_QK)),
            pl.BlockSpec((None, nb, L, ML_QK), lambda b, c: (b, 0, c, COL_KM // ML_QK)),
            pl.BlockSpec((None, nb, L, ML_V), lambda b, c: (b, 0, c, COL_VM // ML_V)),
            pl.BlockSpec((None, nb, L, ML_V), lambda b, c: (b, 0, c, COL_OM // ML_V)),
            pl.BlockSpec((None, nb, L, GATE_W), lambda b, c: (b, 0, c, 0)),
            pl.BlockSpec((None, nb, N_GATES, L), lambda b, c: (b, 0, 0, c)),
            pl.BlockSpec((1, ML_V), lambda b, c: (0, 0)),
            pl.BlockSpec((3 * GATE_W, 2 * ML_HEADS * LANES), lambda b, c: (0, 0)),
        ],
        out_specs=pl.BlockSpec((None, nb, L, ML_V), lambda b, c: (b, 0, c, 0)),
        out_shape=jax.ShapeDtypeStruct((batch // nb, nb, seq, ML_V), bf16),
        scratch_shapes=[pltpu.VMEM((nb * ML_HEADS, ML_DQK, ML_DV + LANES), f32),
                        pltpu.VMEM((nb * ML_HEADS, LANES), f32)],
        compiler_params=pltpu.CompilerParams(
            dimension_semantics=("arbitrary", "arbitrary"), vmem_limit_bytes=VMEM_LIMIT),
        name="mlstm",
    )(proj4, proj4, proj4, proj4, gcol4, grow4, hnw, spread)
    return out.reshape(batch * seq, ML_V)


def _fox_kernel(q_ref, k_ref, v_ref, gcol_ref, out_ref, kaug_sc, st_sc, mx_sc, m_sc, l_sc, acc_sc,
                *, blk, setup_rows, ahead):
    hg = pl.program_id(1)
    qi = pl.program_id(2)
    seq = k_ref.shape[0]
    nh = kaug_sc.shape[0]
    head_cols = [slice(i * FOX_DH, (i + 1) * FOX_DH) for i in range(nh)]

    @pl.when(qi == 0)
    def _():
        r = lax.broadcasted_iota(jnp.int32, (GATE_W, nh * LANES), 0)
        c = lax.broadcasted_iota(jnp.int32, (GATE_W, nh * LANES), 1)
        src = 2 * ML_HEADS + hg * nh + c // LANES
        pick = jnp.concatenate([((r == src) & (c % LANES == j)).astype(bf16) for j in range(3)], axis=0)
        for n in range(seq // setup_rows):
            rows = pl.ds(n * setup_rows, setup_rows)
            a = gcol_ref[rows, :] * (-LOG2E)
            a_hi = a.astype(bf16)
            r1 = a - a_hi.astype(f32)
            a_mid = r1.astype(bf16)
            a_lo = (r1 - a_mid.astype(f32)).astype(bf16)
            aug = _dot(jnp.concatenate([a_hi, a_mid, a_lo], axis=1), pick).astype(bf16)
            for i in range(nh):
                kaug_sc[i, rows, :FOX_DH] = k_ref[rows, head_cols[i]]
                kaug_sc[i, rows, FOX_DH:] = aug[:, i * LANES:(i + 1) * LANES]

    ones3 = (lax.broadcasted_iota(jnp.int32, (blk, LANES), 1) < 3).astype(bf16)
    q_aug = [jnp.concatenate([q_ref[:, head_cols[i]], ones3], axis=1) for i in range(nh)]

    m_sc[...] = jnp.full(m_sc.shape, -jnp.inf, f32)
    l_sc[...] = jnp.zeros(l_sc.shape, f32)
    acc_sc[...] = jnp.zeros(acc_sc.shape, f32)

    def logits_stage(i, j, masked):
        start = pl.multiple_of(j * blk, blk)
        st = _dot_nt(kaug_sc[i, pl.ds(start, blk), :], q_aug[i])
        if masked:
            row = lax.broadcasted_iota(jnp.int32, (blk, blk), 0)
            col = lax.broadcasted_iota(jnp.int32, (blk, blk), 1)
            st = jnp.where(row <= col, st, -jnp.inf)
        st_sc[i] = st
        mx_sc[i] = jnp.max(st, axis=0, keepdims=True)

    def softmax_stage(i, j):
        start = pl.multiple_of(j * blk, blk)
        m = m_sc[i]
        m_new = jnp.maximum(m, mx_sc[i])
        alpha = jnp.exp2(m - m_new)
        p = jnp.exp2(st_sc[i] - m_new)
        l_sc[i] = alpha * l_sc[i] + jnp.sum(p, axis=0, keepdims=True)
        acc_sc[i] = alpha * acc_sc[i] + _dot_tn(v_ref[pl.ds(start, blk), head_cols[i]], p.astype(bf16))
        m_sc[i] = m_new

    def round_robin(cur, nxt, masked):
        for i in range(nh):
            if i + ahead < nh:
                logits_stage(i + ahead, cur, masked)
            elif nxt is not None:
                logits_stage(i + ahead - nh, nxt, False)
            softmax_stage(i, cur)

    for i in range(ahead):
        logits_stage(i, qi, True)
    round_robin(qi, 0, True)

    def body(k, _):
        round_robin(k - 1, jnp.minimum(k, qi - 1), False)
        return 0

    lax.fori_loop(1, qi + 1, body, 0)

    for i in range(nh):
        out_ref[:, head_cols[i]] = (acc_sc[i] * (1.0 / l_sc[i])).T.astype(bf16)


def _fox(proj, gcol, *, batch, seq, blk=512, nh=4, ahead=1):
    assert 1 <= ahead < nh
    assert seq % blk == 0 and FOX_HEADS % nh == 0
    t = proj.shape[0]
    nq = seq // blk
    w = nh * FOX_DH
    kern = functools.partial(_fox_kernel, blk=blk, setup_rows=blk, ahead=ahead)
    return pl.pallas_call(
        kern,
        grid=(batch, FOX_HEADS // nh, nq),
        in_specs=[
            pl.BlockSpec((blk, w), lambda b, h, i: (b * nq + i, COL_QF // w + h)),
            pl.BlockSpec((seq, w), lambda b, h, i: (b, COL_KF // w + h)),
            pl.BlockSpec((seq, w), lambda b, h, i: (b, COL_VF // w + h)),
            pl.BlockSpec((seq, GATE_W), lambda b, h, i: (b, 0)),
        ],
        out_specs=pl.BlockSpec((blk, w), lambda b, h, i: (b * nq + i, h)),
        out_shape=jax.ShapeDtypeStruct((t, FOX_W), bf16),
        scratch_shapes=[pltpu.VMEM((nh, seq, FOX_DH + LANES), bf16),
                        pltpu.VMEM((nh, blk, blk), f32),
                        pltpu.VMEM((nh, 1, blk), f32),
                        pltpu.VMEM((nh, 1, blk), f32),
                        pltpu.VMEM((nh, 1, blk), f32),
                        pltpu.VMEM((nh, FOX_DH, blk), f32)],
        compiler_params=pltpu.CompilerParams(
            dimension_semantics=("arbitrary", "arbitrary", "arbitrary"), vmem_limit_bytes=VMEM_LIMIT),
        name="fox",
    )(proj, proj, proj, gcol)


def _merge_kernel(x_ref, ha_ref, hb_ref, ga_ref, gb_ref, wa_ref, wb_ref, wo_ref,
                  bga_ref, bgb_ref, npost_ref, npre_ref, x1_ref, h2_ref, *, sub):
    n_sub = x_ref.shape[0] // sub
    rows = [slice(r * sub, (r + 1) * sub) for r in range(n_sub)]

    def branches(r):
        return _dot(ha_ref[rows[r], :], wa_ref[...]), _dot(hb_ref[rows[r], :], wb_ref[...])

    def finish(r, z):
        x1 = x_ref[rows[r], :] + _rms(z) * npost_ref[...]
        x1_ref[rows[r], :] = x1
        h2_ref[rows[r], :] = (_rms(x1) * npre_ref[...]).astype(bf16)

    y_next = branches(0)
    z_prev = None
    for r in range(n_sub):
        ya, yb = y_next
        if r + 1 < n_sub:
            y_next = branches(r + 1)
        merged = (jax.nn.sigmoid(ga_ref[rows[r], :].astype(f32) + bga_ref[...]) * ya
                  + jax.nn.sigmoid(gb_ref[rows[r], :].astype(f32) + bgb_ref[...]) * yb)
        z = _dot(merged.astype(bf16), wo_ref[...])
        if z_prev is not None:
            finish(r - 1, z_prev)
        z_prev = z
    finish(n_sub - 1, z_prev)


def _merge(x2, h_a, h_b, proj, wa, wb, wo, bga, bgb, npost, npre, *, tm=1024, sub=512):
    t = x2.shape[0]
    tm = min(tm, t)
    tok = lambda i: (i, 0)
    const = lambda i: (0, 0)
    wspec = pl.BlockSpec((D_MODEL, D_MODEL), const, pipeline_mode=pl.Buffered(1))
    vspec = pl.BlockSpec((1, D_MODEL), const)
    return pl.pallas_call(
        functools.partial(_merge_kernel, sub=sub),
        grid=(t // tm,),
        in_specs=[
            pl.BlockSpec((tm, D_MODEL), tok),
            pl.BlockSpec((tm, ML_V), tok),
            pl.BlockSpec((tm, FOX_W), tok),
            pl.BlockSpec((tm, D_MODEL), lambda i: (i, COL_GA // D_MODEL)),
            pl.BlockSpec((tm, D_MODEL), lambda i: (i, COL_GB // D_MODEL)),
            wspec, wspec, wspec, vspec, vspec, vspec, vspec,
        ],
        out_specs=[pl.BlockSpec((tm, D_MODEL), tok), pl.BlockSpec((tm, D_MODEL), tok)],
        out_shape=[jax.ShapeDtypeStruct((t, D_MODEL), f32), jax.ShapeDtypeStruct((t, D_MODEL), bf16)],
        compiler_params=pltpu.CompilerParams(
            dimension_semantics=("arbitrary",), vmem_limit_bytes=VMEM_LIMIT),
        name="merge",
    )(x2, h_a, h_b, proj, proj, wa, wb, wo, bga, bgb, npost, npre)


def _ffn_kernel(x1_ref, h2_ref, wup_ref, cw_ref, cb_ref, wdn_ref, npost_ref, out_ref,
                halo_sc, ubuf_sc, act_sc, acc_sc, *, blocks_per_seq, tf, down_group):
    i = pl.program_id(0)
    tm = x1_ref.shape[0]

    @pl.when(i % blocks_per_seq == 0)
    def _():
        halo_sc[...] = jnp.zeros_like(halo_sc)

    h2 = h2_ref[...]
    n_chunks = D_FF // tf

    def up(c):
        for half in range(2):
            col0 = half * D_FF + c * tf
            slot = 2 * (c % 2) + half
            u = _dot(h2, wup_ref[:, col0:col0 + tf])
            ubuf_sc[slot, 0:SUBLANES, :] = halo_sc[:, col0:col0 + tf]
            ubuf_sc[slot, SUBLANES:SUBLANES + tm, :] = u
            halo_sc[:, col0:col0 + tf] = u[tm - SUBLANES:tm, :]

    def conv(c, half):
        col0 = half * D_FF + c * tf
        slot = 2 * (c % 2) + half
        w = cw_ref[:, col0:col0 + tf]
        y = cb_ref[:, col0:col0 + tf]
        for j in range(CONV_W):
            first = SUBLANES - (CONV_W - 1 - j)
            y = y + w[j:j + 1] * ubuf_sc[slot, first:first + tm, :]
        return y

    def gate(g, a):
        inner = g * (GELU_K1 + (GELU_K1 * GELU_CUBIC) * (g * g))
        return (g * a) * (1.0 / (1.0 + jnp.exp2(inner)))

    def down(c0, c1):
        return _dot(act_sc[:, c0 * tf:c1 * tf], wdn_ref[c0 * tf:c1 * tf, :])

    up(0)
    pending, done = None, 0
    for c in range(n_chunks):
        if c + 1 < n_chunks:
            up(c + 1)
        if pending is not None:
            if pending[0] == 0:
                acc_sc[...] = down(*pending)
            else:
                acc_sc[...] += down(*pending)
            pending = None
        act_sc[:, c * tf:(c + 1) * tf] = gate(conv(c, 1), conv(c, 0)).astype(bf16)
        if (c + 1) % down_group == 0 or c + 1 == n_chunks:
            pending, done = (done, c + 1), c + 1

    y = down(*pending)
    if pending[0] != 0:
        y = y + acc_sc[...]
    out_ref[...] = x1_ref[...] + _rms(y) * npost_ref[...]


def _ffn(x1, h2, wup, cw, cb, wdn, npost, *, seq, tm=512, tf=256, down_group=2):
    t = x1.shape[0]
    tok = lambda i: (i, 0)
    const = lambda i: (0, 0)
    kern = functools.partial(_ffn_kernel, blocks_per_seq=seq // tm, tf=tf, down_group=down_group)
    return pl.pallas_call(
        kern,
        grid=(t // tm,),
        in_specs=[
            pl.BlockSpec((tm, D_MODEL), tok),
            pl.BlockSpec((tm, D_MODEL), tok),
            pl.BlockSpec((D_MODEL, 2 * D_FF), const, pipeline_mode=pl.Buffered(1)),
            pl.BlockSpec((CONV_W, 2 * D_FF), const),
            pl.BlockSpec((1, 2 * D_FF), const),
            pl.BlockSpec((D_FF, D_MODEL), const, pipeline_mode=pl.Buffered(1)),
            pl.BlockSpec((1, D_MODEL), const),
        ],
        out_specs=pl.BlockSpec((tm, D_MODEL), tok),
        out_shape=jax.ShapeDtypeStruct((t, D_MODEL), f32),
        scratch_shapes=[pltpu.VMEM((SUBLANES, 2 * D_FF), f32),
                        pltpu.VMEM((4, tm + SUBLANES, tf), f32),
                        pltpu.VMEM((tm, D_FF), bf16),
                        pltpu.VMEM((tm, D_MODEL), f32)],
        compiler_params=pltpu.CompilerParams(
            dimension_semantics=("arbitrary",), vmem_limit_bytes=VMEM_LIMIT),
        name="ffn",
    )(x1, h2, wup, cw, cb, wdn, npost)


def _layer(x2, p, l, *, batch, seq):
    w_in = p["w_in"][l]
    o_i = 2 * ML_QK + ML_V
    o_o = o_i + 2 * ML_HEADS
    o_qf = o_o + ML_V
    o_ff = o_qf + 3 * FOX_W
    o_ga = o_ff + FOX_HEADS
    w_pieces = [
        w_in[:, :o_i].astype(bf16),
        w_in[:, o_o:o_qf].astype(bf16),
        w_in[:, o_qf:o_ff].astype(bf16),
        w_in[:, o_ga:].astype(bf16),
    ]
    w_gate = jnp.concatenate([
        w_in[:, o_i:o_o], w_in[:, o_ff:o_ga],
        jnp.zeros((D_MODEL, GATE_W - N_GATES), f32)], axis=1).astype(bf16)
    b_gate = jnp.concatenate([
        p["b_ml_i"][l], p["b_ml_f"][l], p["b_fox_f"][l], jnp.zeros((GATE_W - N_GATES,), f32)])[None, :]
    tri = jnp.tril(jnp.ones((ML_CHUNK, ML_CHUNK), f32)).astype(bf16)
    row = lambda v: v[None, :].astype(f32)

    proj, gcol, grow = _inproj(x2, row(p["norm_mix_pre"][l]), w_pieces, w_gate, b_gate, tri, seq=seq)
    h_a = _mlstm(proj, gcol, grow, row(p["ml_head_norm"][l]), batch=batch, seq=seq)
    h_b = _fox(proj, gcol, batch=batch, seq=seq)
    x1, h2 = _merge(x2, h_a, h_b, proj,
                    p["w_branch_a"][l].astype(bf16), p["w_branch_b"][l].astype(bf16),
                    p["w_out"][l].astype(bf16), row(p["b_gate_a"][l]), row(p["b_gate_b"][l]),
                    row(p["norm_mix_post"][l]), row(p["norm_ffn_pre"][l]))
    return _ffn(x1, h2, p["w_up"][l].astype(bf16), p["conv_w"][l], row(p["conv_b"][l]),
                p["w_down"][l].astype(bf16), row(p["norm_ffn_post"][l]), seq=seq)


def kernel(x, norm_mix_pre, w_in, b_ml_i, b_ml_f, ml_head_norm, b_fox_f, b_gate_a, b_gate_b,
           w_branch_a, w_branch_b, w_out, norm_mix_post, norm_ffn_pre, w_up, conv_w, conv_b,
           w_down, norm_ffn_post):
    batch, seq, _ = x.shape
    p = dict(norm_mix_pre=norm_mix_pre, w_in=w_in, b_ml_i=b_ml_i, b_ml_f=b_ml_f,
             ml_head_norm=ml_head_norm, b_fox_f=b_fox_f, b_gate_a=b_gate_a, b_gate_b=b_gate_b,
             w_branch_a=w_branch_a, w_branch_b=w_branch_b, w_out=w_out,
             norm_mix_post=norm_mix_post, norm_ffn_pre=norm_ffn_pre, w_up=w_up, conv_w=conv_w,
             conv_b=conv_b, w_down=w_down, norm_ffn_post=norm_ffn_post)
    x2 = x.reshape(batch * seq, D_MODEL)
    for l in range(w_in.shape[0]):
        x2 = _layer(x2, p, l, batch=batch, seq=seq)
    return x2.reshape(batch, seq, D_MODEL)
```

```python
import functools

import jax
import jax.numpy as jnp
from jax import lax
from jax.experimental import pallas as pl
from jax.experimental.pallas import tpu as pltpu

D_MODEL = 1024
ML_HEADS = 4
ML_DQK = 128
ML_DV = 256
ML_QK = ML_HEADS * ML_DQK
ML_V = ML_HEADS * ML_DV
FOX_HEADS = 8
FOX_DH = 128
FOX_W = FOX_HEADS * FOX_DH
D_FF = 2816
CONV_W = 3
GATE_CAP = 15.0
EPS = 1e-6
LOG2E = 1.4426950408889634
GELU_CUBIC = 0.044715
GELU_K1 = -2.0 * (2.0 / 3.141592653589793) ** 0.5 * LOG2E

LANES = 128
SUBLANES = 8
GATE_W = LANES
N_GATES = 2 * ML_HEADS + FOX_HEADS
ML_CHUNK = 128
VMEM_LIMIT = 56 * 1024 * 1024

PROJ_W = 2 * ML_QK + 2 * ML_V + 3 * FOX_W + 2 * D_MODEL
COL_QM, COL_KM, COL_VM, COL_OM = 0, ML_QK, 2 * ML_QK, 2 * ML_QK + ML_V
COL_QF = COL_OM + ML_V
COL_KF = COL_QF + FOX_W
COL_VF = COL_KF + FOX_W
COL_GA = COL_VF + FOX_W
COL_GB = COL_GA + D_MODEL

f32 = jnp.float32
bf16 = jnp.bfloat16


def _log_sigmoid(z):
    return jnp.minimum(z, 0.0) - jnp.log1p(jnp.exp(-jnp.abs(z)))


def _rms(v):
    return v * lax.rsqrt(jnp.mean(v * v, axis=-1, keepdims=True) + EPS)


def _dot(a, b):
    return jnp.dot(a, b, preferred_element_type=f32)


def _dot_nt(a, b):
    return lax.dot_general(a, b, (((1,), (1,)), ((), ())), preferred_element_type=f32)


def _dot_tn(a, b):
    return lax.dot_general(a, b, (((0,), (0,)), ((), ())), preferred_element_type=f32)


def _inproj_kernel(*refs, blocks_per_seq, piece, w_blocks):
    x_ref, nw_ref, wg_ref, bg_ref, tri_ref, cs_ref = refs[:6]
    w_refs = refs[6:6 + len(w_blocks)]
    proj_ref, gcol_ref, grow_ref, h_sc, carry_sc = refs[6 + len(w_blocks):]
    assert w_blocks[0][0] == 0
    i = pl.program_id(0)
    j = pl.program_id(1)
    tm = x_ref.shape[0]

    def project(hb, w_ref=w_refs[0]):
        return (_dot(hb, w_ref[...]) * cs_ref[...]).astype(bf16)

    @pl.when(j == 0)
    def _():
        @pl.when(i % blocks_per_seq == 0)
        def _():
            carry_sc[...] = jnp.zeros_like(carry_sc)

        tri = tri_ref[...]
        lane = lax.broadcasted_iota(jnp.int32, (ML_CHUNK, GATE_W), 1)
        per_piece = piece // ML_CHUNK

        def normed(p):
            prow = slice(p * piece, (p + 1) * piece)
            hb = (_rms(x_ref[prow, :]) * nw_ref[...]).astype(bf16)
            h_sc[prow, :] = hb
            return hb

        acts = []
        hb = normed(0)
        for p in range(tm // piece):
            proj_ref[p * piece:(p + 1) * piece, :] = project(hb)
            g_piece = _dot(hb, wg_ref[...]) + bg_ref[...]
            if (p + 1) * piece < tm:
                hb = normed(p + 1)
            for q in range(per_piece):
                g = g_piece[q * ML_CHUNK:(q + 1) * ML_CHUNK]
                cap = GATE_CAP * jnp.tanh(g / GATE_CAP)
                a = jnp.where(lane < ML_HEADS, cap,
                              jnp.where(lane < 2 * ML_HEADS, _log_sigmoid(cap), _log_sigmoid(g)))
                acts.append(jnp.where(lane < N_GATES, a, 0.0))

        for r in range(tm // ML_CHUNK):
            rows = slice(r * ML_CHUNK, (r + 1) * ML_CHUNK)
            a = acts[r]
            a_hi = a.astype(bf16)
            r1 = a - a_hi.astype(f32)
            a_mid = r1.astype(bf16)
            a_lo = (r1 - a_mid.astype(f32)).astype(bf16)
            cs3 = _dot(tri, jnp.concatenate([a_hi, a_mid, a_lo], axis=1))
            cs = cs3[:, :GATE_W] + cs3[:, GATE_W:2 * GATE_W] + cs3[:, 2 * GATE_W:]
            glob = cs + carry_sc[...]
            carry_sc[...] = glob[ML_CHUNK - 1:ML_CHUNK, :]
            out = jnp.where(lane < ML_HEADS, a, jnp.where(lane < 2 * ML_HEADS, cs, glob))
            gcol_ref[rows, :] = out
            grow_ref[:, rows] = out.T[:N_GATES, :]

    for w_ref, (first, count) in zip(w_refs, w_blocks):
        @pl.when((j >= max(first, 1)) & (j < first + count))
        def _(w_ref=w_ref):
            proj_ref[...] = project(h_sc[...], w_ref)


def _proj_col_scale():
    cs = jnp.ones((1, PROJ_W), f32)
    cs = cs.at[:, COL_QM:COL_QM + ML_QK].set(ML_DQK ** -0.5)
    return cs.at[:, COL_QF:COL_QF + FOX_W].set(LOG2E * FOX_DH ** -0.5)


def _inproj(x2, nw, w_pieces, w_gate, b_gate, tri, *, seq, tm=2048, tn=1024, piece=512):
    t = x2.shape[0]
    tm = min(tm, seq)
    assert seq % tm == 0 and tm % piece == 0
    piece = max(piece, ML_CHUNK)
    w_blocks, first = [], 0
    for w in w_pieces:
        w_blocks.append((first, w.shape[1] // tn))
        first += w.shape[1] // tn
    assert first == PROJ_W // tn
    kern = functools.partial(_inproj_kernel, blocks_per_seq=seq // tm, piece=piece, w_blocks=tuple(w_blocks))

    def w_spec(first, count):
        return pl.BlockSpec((D_MODEL, tn), lambda i, j: (
            0, jnp.where(j < first - 1, count - 1, jnp.clip(j - first, 0, count - 1))))

    return pl.pallas_call(
        kern,
        grid=(t // tm, PROJ_W // tn),
        in_specs=[
            pl.BlockSpec((tm, D_MODEL), lambda i, j: (i, 0)),
            pl.BlockSpec((1, D_MODEL), lambda i, j: (0, 0)),
            pl.BlockSpec((D_MODEL, GATE_W), lambda i, j: (0, 0)),
            pl.BlockSpec((1, GATE_W), lambda i, j: (0, 0)),
            pl.BlockSpec((ML_CHUNK, ML_CHUNK), lambda i, j: (0, 0)),
            pl.BlockSpec((1, tn), lambda i, j: (0, j)),
        ] + [w_spec(f, n) for f, n in w_blocks],
        out_specs=[
            pl.BlockSpec((tm, tn), lambda i, j: (i, j)),
            pl.BlockSpec((tm, GATE_W), lambda i, j: (i, 0)),
            pl.BlockSpec((N_GATES, tm), lambda i, j: (0, i)),
        ],
        out_shape=[
            jax.ShapeDtypeStruct((t, PROJ_W), bf16),
            jax.ShapeDtypeStruct((t, GATE_W), f32),
            jax.ShapeDtypeStruct((N_GATES, t), f32),
        ],
        scratch_shapes=[pltpu.VMEM((tm, D_MODEL), bf16), pltpu.VMEM((1, GATE_W), f32)],
        compiler_params=pltpu.CompilerParams(
            dimension_semantics=("arbitrary", "arbitrary"), vmem_limit_bytes=VMEM_LIMIT),
        name="inproj",
    )(x2, nw, w_gate, b_gate, tri, _proj_col_scale(), *w_pieces)


def _mlstm_kernel(q_ref, k_ref, v_ref, o_ref, gcol_ref, grow_ref, hnw_ref, spread_ref, out_ref, s_sc, m_sc):
    c = pl.program_id(1)
    L = ML_CHUNK
    nb = q_ref.shape[0]
    assert L == LANES

    @pl.when(c == 0)
    def _():
        s_sc[...] = jnp.zeros_like(s_sc)
        m_sc[...] = jnp.zeros_like(m_sc)

    row = lax.broadcasted_iota(jnp.int32, (L, L), 0)
    col = lax.broadcasted_iota(jnp.int32, (L, L), 1)
    causal = col <= row
    ones_blk = jnp.ones((L, LANES), bf16)

    def wide(tile, n):
        return jnp.concatenate([tile] * n, axis=1)

    P = [(n, h) for n in range(nb) for h in range(ML_HEADS)]
    C = range(len(P))
    qk_cols = lambda h: slice(h * ML_DQK, (h + 1) * ML_DQK)
    v_cols = lambda h: slice(h * ML_DV, (h + 1) * ML_DV)
    spread = []
    for n in range(nb):
        g = gcol_ref[n]
        g_hi = g.astype(bf16)
        r1 = g - g_hi.astype(f32)
        g_mid = r1.astype(bf16)
        g_lo = (r1 - g_mid.astype(f32)).astype(bf16)
        spread.append(_dot(jnp.concatenate([g_hi, g_mid, g_lo], axis=1), spread_ref[...]))
    tile = lambda n, j: spread[n][:, j * LANES:(j + 1) * LANES]
    b_t = [tile(n, h) for n, h in P]
    a_t = [tile(n, ML_HEADS + h) for n, h in P]
    gr = [grow_ref[n] for n in range(nb)]
    a_r = [gr[n][h:h + 1, :] - gr[n][ML_HEADS + h:ML_HEADS + h + 1, :] for n, h in P]
    q = [q_ref[n, :, qk_cols(h)] for n, h in P]
    k = [k_ref[n, :, qk_cols(h)] for n, h in P]
    v_aug = [jnp.concatenate([v_ref[n, :, v_cols(h)], ones_blk], axis=1) for n, h in P]
    m_prev = [m_sc[i:i + 1, :] for i in C]
    state = [s_sc[i] for i in C]

    qk = [_dot_nt(q[i], k[i]) for i in C]
    amat = [jnp.where(causal, a_r[i], -jnp.inf) for i in C]
    mx = [jnp.maximum(m_prev[i], jnp.broadcast_to(jnp.max(amat[i], axis=-1, keepdims=True), (L, LANES)))
          for i in C]

    mx_last = [mx[i][L - 1:L, :] for i in C]
    b_last = [b_t[i][L - 1:L, :] for i in C]
    wk = [jnp.exp(a_t[i] - mx_last[i]) for i in C]
    upd = [_dot_tn((wk[i] * k[i].astype(f32)).astype(bf16), v_aug[i]) for i in C]

    s = [(qk[i] * jnp.exp(amat[i] - mx[i])).astype(bf16) for i in C]
    q_in = [(jnp.exp(m_prev[i] - mx[i]) * q[i].astype(f32)).astype(bf16) for i in C]
    num = [_dot(jnp.concatenate([s[i], q_in[i]], axis=1),
                jnp.concatenate([v_aug[i], state[i].astype(bf16)], axis=0)) for i in C]
    for i in C:
        s_sc[i] = wide(jnp.exp(m_prev[i] - mx_last[i]), 3) * state[i] + upd[i]
        m_sc[i:i + 1, :] = b_last[i] + mx_last[i]

    inv = [1.0 / jnp.maximum(jnp.abs(num[i][:, ML_DV:]), jnp.exp(-b_t[i] - mx[i])) for i in C]
    msq = [jnp.broadcast_to(jnp.mean(num[i][:, :ML_DV] * num[i][:, :ML_DV], axis=-1, keepdims=True),
                            (L, LANES)) for i in C]
    fac = [inv[i] * lax.rsqrt(inv[i] * inv[i] * msq[i] + EPS) for i in C]
    for i, (n, h) in enumerate(P):
        gate = jax.nn.sigmoid(o_ref[n, :, v_cols(h)].astype(f32)) * hnw_ref[:, v_cols(h)]
        out_ref[n, :, v_cols(h)] = (num[i][:, :ML_DV] * wide(fac[i], ML_DV // LANES) * gate).astype(bf16)


def _mlstm(proj, gcol, grow, hnw, *, batch, seq, nb=4):
    while batch % nb:
        nb //= 2
    nc = seq // ML_CHUNK
    L = ML_CHUNK
    proj4 = proj.reshape(batch // nb, nb, seq, PROJ_W)
    gcol4 = gcol.reshape(batch // nb, nb, seq, GATE_W)
    grow4 = grow.reshape(N_GATES, batch // nb, nb, seq).transpose(1, 2, 0, 3)
    src = lax.broadcasted_iota(jnp.int32, (GATE_W, 2 * ML_HEADS * LANES), 0)
    dst = lax.broadcasted_iota(jnp.int32, (GATE_W, 2 * ML_HEADS * LANES), 1) // LANES
    spread = (jnp.where(src == ML_HEADS + dst % ML_HEADS, jnp.where(dst < ML_HEADS, 1.0, -1.0), 0.0)
              + jnp.where((dst >= ML_HEADS) & (src == dst - ML_HEADS), 1.0, 0.0)).astype(bf16)
    spread = jnp.concatenate([spread] * 3, axis=0)
    out = pl.pallas_call(
        _mlstm_kernel,
        grid=(batch // nb, nc),
        in_specs=[
            pl.BlockSpec((None, nb, L, ML_QK), lambda b, c: (b, 0, c, COL_QM // ML_QK)),
            pl.BlockSpec((None, nb, L, ML_QK), lambda b, c: (b, 0, c, COL_KM // ML_QK)),
            pl.BlockSpec((None, nb, L, ML_V), lambda b, c: (b, 0, c, COL_VM // ML_V)),
            pl.BlockSpec((None, nb, L, ML_V), lambda b, c: (b, 0, c, COL_OM // ML_V)),
            pl.BlockSpec((None, nb, L, GATE_W), lambda b, c: (b, 0, c, 0)),
            pl.BlockSpec((None, nb, N_GATES, L), lambda b, c: (b, 0, 0, c)),
            pl.BlockSpec((1, ML_V), lambda b, c: (0, 0)),
            pl.BlockSpec((3 * GATE_W, 2 * ML_HEADS * LANES), lambda b, c: (0, 0)),
        ],
        out_specs=pl.BlockSpec((None, nb, L, ML_V), lambda b, c: (b, 0, c, 0)),
        out_shape=jax.ShapeDtypeStruct((batch // nb, nb, seq, ML_V), bf16),
        scratch_shapes=[pltpu.VMEM((nb * ML_HEADS, ML_DQK, ML_DV + LANES), f32),
                        pltpu.VMEM((nb * ML_HEADS, LANES), f32)],
        compiler_params=pltpu.CompilerParams(
            dimension_semantics=("arbitrary", "arbitrary"), vmem_limit_bytes=VMEM_LIMIT),
        name="mlstm",
    )(proj4, proj4, proj4, proj4, gcol4, grow4, hnw, spread)
    return out.reshape(batch * seq, ML_V)


def _fox_kernel(q_ref, k_ref, v_ref, gcol_ref, out_ref, kaug_sc, st_sc, mx_sc, m_sc, l_sc, acc_sc,
                *, blk, setup_rows, ahead):
    hg = pl.program_id(1)
    seq = k_ref.shape[0]
    nh = kaug_sc.shape[0]
    head_cols = [slice(i * FOX_DH, (i + 1) * FOX_DH) for i in range(nh)]

    r = lax.broadcasted_iota(jnp.int32, (GATE_W, nh * LANES), 0)
    c = lax.broadcasted_iota(jnp.int32, (GATE_W, nh * LANES), 1)
    src = 2 * ML_HEADS + hg * nh + c // LANES
    pick = jnp.concatenate([((r == src) & (c % LANES == j)).astype(bf16) for j in range(3)], axis=0)
    for n in range(seq // setup_rows):
        rows = pl.ds(n * setup_rows, setup_rows)
        a = gcol_ref[rows, :] * (-LOG2E)
        a_hi = a.astype(bf16)
        r1 = a - a_hi.astype(f32)
        a_mid = r1.astype(bf16)
        a_lo = (r1 - a_mid.astype(f32)).astype(bf16)
        aug = _dot(jnp.concatenate([a_hi, a_mid, a_lo], axis=1), pick).astype(bf16)
        for i in range(nh):
            kaug_sc[i, rows, :FOX_DH] = k_ref[rows, head_cols[i]]
            kaug_sc[i, rows, FOX_DH:] = aug[:, i * LANES:(i + 1) * LANES]

    ones3 = (lax.broadcasted_iota(jnp.int32, (blk, LANES), 1) < 3).astype(bf16)

    def query_block(qi, _):
        qrows = pl.ds(pl.multiple_of(qi * blk, blk), blk)
        q_aug = [jnp.concatenate([q_ref[qrows, head_cols[i]], ones3], axis=1) for i in range(nh)]

        m_sc[...] = jnp.full(m_sc.shape, -jnp.inf, f32)
        l_sc[...] = jnp.zeros(l_sc.shape, f32)
        acc_sc[...] = jnp.zeros(acc_sc.shape, f32)

        def logits_stage(i, j, masked):
            start = pl.multiple_of(j * blk, blk)
            st = _dot_nt(kaug_sc[i, pl.ds(start, blk), :], q_aug[i])
            if masked:
                row = lax.broadcasted_iota(jnp.int32, (blk, blk), 0)
                col = lax.broadcasted_iota(jnp.int32, (blk, blk), 1)
                st = jnp.where(row <= col, st, -jnp.inf)
            st_sc[i] = st
            mx_sc[i] = jnp.max(st, axis=0, keepdims=True)

        def softmax_stage(i, j):
            start = pl.multiple_of(j * blk, blk)
            m = m_sc[i]
            m_new = jnp.maximum(m, mx_sc[i])
            alpha = jnp.exp2(m - m_new)
            p = jnp.exp2(st_sc[i] - m_new)
            l_sc[i] = alpha * l_sc[i] + jnp.sum(p, axis=0, keepdims=True)
            acc_sc[i] = alpha * acc_sc[i] + _dot_tn(v_ref[pl.ds(start, blk), head_cols[i]], p.astype(bf16))
            m_sc[i] = m_new

        def round_robin(cur, nxt, masked):
            for i in range(nh):
                if i + ahead < nh:
                    logits_stage(i + ahead, cur, masked)
                else:
                    logits_stage(i + ahead - nh, nxt, False)
                softmax_stage(i, cur)

        for i in range(ahead):
            logits_stage(i, qi, True)
        round_robin(qi, 0, True)

        def body(k, _):
            round_robin(k - 1, jnp.minimum(k, qi - 1), False)
            return 0

        lax.fori_loop(1, qi + 1, body, 0)

        for i in range(nh):
            out_ref[qrows, head_cols[i]] = (acc_sc[i] * (1.0 / l_sc[i])).T.astype(bf16)
        return 0

    lax.fori_loop(0, seq // blk, query_block, 0)


def _fox(proj, gcol, *, batch, seq, blk=512, nh=4, ahead=1):
    assert 1 <= ahead < nh
    assert seq % blk == 0 and FOX_HEADS % nh == 0
    t = proj.shape[0]
    w = nh * FOX_DH
    kern = functools.partial(_fox_kernel, blk=blk, setup_rows=blk, ahead=ahead)
    return pl.pallas_call(
        kern,
        grid=(batch, FOX_HEADS // nh),
        in_specs=[
            pl.BlockSpec((seq, w), lambda b, h: (b, COL_QF // w + h)),
            pl.BlockSpec((seq, w), lambda b, h: (b, COL_KF // w + h)),
            pl.BlockSpec((seq, w), lambda b, h: (b, COL_VF // w + h)),
            pl.BlockSpec((seq, GATE_W), lambda b, h: (b, 0)),
        ],
        out_specs=pl.BlockSpec((seq, w), lambda b, h: (b, h)),
        out_shape=jax.ShapeDtypeStruct((t, FOX_W), bf16),
        scratch_shapes=[pltpu.VMEM((nh, seq, FOX_DH + LANES), bf16),
                        pltpu.VMEM((nh, blk, blk), f32),
                        pltpu.VMEM((nh, 1, blk), f32),
                        pltpu.VMEM((nh, 1, blk), f32),
                        pltpu.VMEM((nh, 1, blk), f32),
                        pltpu.VMEM((nh, FOX_DH, blk), f32)],
        compiler_params=pltpu.CompilerParams(
            dimension_semantics=("arbitrary", "arbitrary"), vmem_limit_bytes=VMEM_LIMIT),
        name="fox",
    )(proj, proj, proj, gcol)


def _merge_kernel(x_ref, ha_ref, hb_ref, ga_ref, gb_ref, wa_ref, wb_ref, wo_ref,
                  bga_ref, bgb_ref, npost_ref, npre_ref, x1_ref, h2_ref, *, sub):
    n_sub = x_ref.shape[0] // sub
    rows = [slice(r * sub, (r + 1) * sub) for r in range(n_sub)]

    def branches(r):
        return _dot(ha_ref[rows[r], :], wa_ref[...]), _dot(hb_ref[rows[r], :], wb_ref[...])

    def finish(r, z):
        x1 = x_ref[rows[r], :] + _rms(z) * npost_ref[...]
        x1_ref[rows[r], :] = x1
        h2_ref[rows[r], :] = (_rms(x1) * npre_ref[...]).astype(bf16)

    y_next = branches(0)
    z_prev = None
    for r in range(n_sub):
        ya, yb = y_next
        if r + 1 < n_sub:
            y_next = branches(r + 1)
        merged = (jax.nn.sigmoid(ga_ref[rows[r], :].astype(f32) + bga_ref[...]) * ya
                  + jax.nn.sigmoid(gb_ref[rows[r], :].astype(f32) + bgb_ref[...]) * yb)
        z = _dot(merged.astype(bf16), wo_ref[...])
        if z_prev is not None:
            finish(r - 1, z_prev)
        z_prev = z
    finish(n_sub - 1, z_prev)


def _merge(x2, h_a, h_b, proj, wa, wb, wo, bga, bgb, npost, npre, *, tm=1024, sub=512):
    t = x2.shape[0]
    tm = min(tm, t)
    tok = lambda i: (i, 0)
    const = lambda i: (0, 0)
    wspec = pl.BlockSpec((D_MODEL, D_MODEL), const, pipeline_mode=pl.Buffered(1))
    vspec = pl.BlockSpec((1, D_MODEL), const)
    return pl.pallas_call(
        functools.partial(_merge_kernel, sub=sub),
        grid=(t // tm,),
        in_specs=[
            pl.BlockSpec((tm, D_MODEL), tok),
            pl.BlockSpec((tm, ML_V), tok),
            pl.BlockSpec((tm, FOX_W), tok),
            pl.BlockSpec((tm, D_MODEL), lambda i: (i, COL_GA // D_MODEL)),
            pl.BlockSpec((tm, D_MODEL), lambda i: (i, COL_GB // D_MODEL)),
            wspec, wspec, wspec, vspec, vspec, vspec, vspec,
        ],
        out_specs=[pl.BlockSpec((tm, D_MODEL), tok), pl.BlockSpec((tm, D_MODEL), tok)],
        out_shape=[jax.ShapeDtypeStruct((t, D_MODEL), f32), jax.ShapeDtypeStruct((t, D_MODEL), bf16)],
        compiler_params=pltpu.CompilerParams(
            dimension_semantics=("arbitrary",), vmem_limit_bytes=VMEM_LIMIT),
        name="merge",
    )(x2, h_a, h_b, proj, proj, wa, wb, wo, bga, bgb, npost, npre)


def _ffn_kernel(x1_ref, h2_ref, wup_ref, cw_ref, cb_ref, wdn_ref, npost_ref, out_ref,
                halo_sc, ubuf_sc, act_sc, acc_sc, *, blocks_per_seq, tf, down_group):
    i = pl.program_id(0)
    tm = x1_ref.shape[0]

    @pl.when(i % blocks_per_seq == 0)
    def _():
        halo_sc[...] = jnp.zeros_like(halo_sc)

    h2 = h2_ref[...]
    n_chunks = D_FF // tf

    def up(c):
        for half in range(2):
            col0 = half * D_FF + c * tf
            slot = 2 * (c % 2) + half
            u = _dot(h2, wup_ref[:, col0:col0 + tf])
            ubuf_sc[slot, 0:SUBLANES, :] = halo_sc[:, col0:col0 + tf]
            ubuf_sc[slot, SUBLANES:SUBLANES + tm, :] = u
            halo_sc[:, col0:col0 + tf] = u[tm - SUBLANES:tm, :]

    def conv(c, half):
        col0 = half * D_FF + c * tf
        slot = 2 * (c % 2) + half
        w = cw_ref[:, col0:col0 + tf]
        y = cb_ref[:, col0:col0 + tf]
        for j in range(CONV_W):
            first = SUBLANES - (CONV_W - 1 - j)
            y = y + w[j:j + 1] * ubuf_sc[slot, first:first + tm, :]
        return y

    def gate(g, a):
        inner = g * (GELU_K1 + (GELU_K1 * GELU_CUBIC) * (g * g))
        return (g * a) * (1.0 / (1.0 + jnp.exp2(inner)))

    def down(c0, c1):
        return _dot(act_sc[:, c0 * tf:c1 * tf], wdn_ref[c0 * tf:c1 * tf, :])

    up(0)
    pending, done = None, 0
    for c in range(n_chunks):
        if c + 1 < n_chunks:
            up(c + 1)
        if pending is not None:
            if pending[0] == 0:
                acc_sc[...] = down(*pending)
            else:
                acc_sc[...] += down(*pending)
            pending = None
        act_sc[:, c * tf:(c + 1) * tf] = gate(conv(c, 1), conv(c, 0)).astype(bf16)
        if (c + 1) % down_group == 0 or c + 1 == n_chunks:
            pending, done = (done, c + 1), c + 1

    y = down(*pending)
    if pending[0] != 0:
        y = y + acc_sc[...]
    out_ref[...] = x1_ref[...] + _rms(y) * npost_ref[...]


def _ffn(x1, h2, wup, cw, cb, wdn, npost, *, seq, tm=512, tf=256, down_group=2):
    t = x1.shape[0]
    tok = lambda i: (i, 0)
    const = lambda i: (0, 0)
    kern = functools.partial(_ffn_kernel, blocks_per_seq=seq // tm, tf=tf, down_group=down_group)
    return pl.pallas_call(
        kern,
        grid=(t // tm,),
        in_specs=[
            pl.BlockSpec((tm, D_MODEL), tok),
            pl.BlockSpec((tm, D_MODEL), tok),
            pl.BlockSpec((D_MODEL, 2 * D_FF), const, pipeline_mode=pl.Buffered(1)),
            pl.BlockSpec((CONV_W, 2 * D_FF), const),
            pl.BlockSpec((1, 2 * D_FF), const),
            pl.BlockSpec((D_FF, D_MODEL), const, pipeline_mode=pl.Buffered(1)),
            pl.BlockSpec((1, D_MODEL), const),
        ],
        out_specs=pl.BlockSpec((tm, D_MODEL), tok),
        out_shape=jax.ShapeDtypeStruct((t, D_MODEL), f32),
        scratch_shapes=[pltpu.VMEM((SUBLANES, 2 * D_FF), f32),
                        pltpu.VMEM((4, tm + SUBLANES, tf), f32),
                        pltpu.VMEM((tm, D_FF), bf16),
                        pltpu.VMEM((tm, D_MODEL), f32)],
        compiler_params=pltpu.CompilerParams(
            dimension_semantics=("arbitrary",), vmem_limit_bytes=VMEM_LIMIT),
        name="ffn",
    )(x1, h2, wup, cw, cb, wdn, npost)


def _layer(x2, p, l, *, batch, seq):
    w_in = p["w_in"][l]
    o_i = 2 * ML_QK + ML_V
    o_o = o_i + 2 * ML_HEADS
    o_qf = o_o + ML_V
    o_ff = o_qf + 3 * FOX_W
    o_ga = o_ff + FOX_HEADS
    w_pieces = [
        w_in[:, :o_i].astype(bf16),
        w_in[:, o_o:o_qf].astype(bf16),
        w_in[:, o_qf:o_ff].astype(bf16),
        w_in[:, o_ga:].astype(bf16),
    ]
    w_gate = jnp.concatenate([
        w_in[:, o_i:o_o], w_in[:, o_ff:o_ga],
        jnp.zeros((D_MODEL, GATE_W - N_GATES), f32)], axis=1).astype(bf16)
    b_gate = jnp.concatenate([
        p["b_ml_i"][l], p["b_ml_f"][l], p["b_fox_f"][l], jnp.zeros((GATE_W - N_GATES,), f32)])[None, :]
    tri = jnp.tril(jnp.ones((ML_CHUNK, ML_CHUNK), f32)).astype(bf16)
    row = lambda v: v[None, :].astype(f32)

    proj, gcol, grow = _inproj(x2, row(p["norm_mix_pre"][l]), w_pieces, w_gate, b_gate, tri, seq=seq)
    h_a = _mlstm(proj, gcol, grow, row(p["ml_head_norm"][l]), batch=batch, seq=seq)
    h_b = _fox(proj, gcol, batch=batch, seq=seq)
    x1, h2 = _merge(x2, h_a, h_b, proj,
                    p["w_branch_a"][l].astype(bf16), p["w_branch_b"][l].astype(bf16),
                    p["w_out"][l].astype(bf16), row(p["b_gate_a"][l]), row(p["b_gate_b"][l]),
                    row(p["norm_mix_post"][l]), row(p["norm_ffn_pre"][l]))
    return _ffn(x1, h2, p["w_up"][l].astype(bf16), p["conv_w"][l], row(p["conv_b"][l]),
                p["w_down"][l].astype(bf16), row(p["norm_ffn_post"][l]), seq=seq)


def kernel(x, norm_mix_pre, w_in, b_ml_i, b_ml_f, ml_head_norm, b_fox_f, b_gate_a, b_gate_b,
           w_branch_a, w_branch_b, w_out, norm_mix_post, norm_ffn_pre, w_up, conv_w, conv_b,
           w_down, norm_ffn_post):
    batch, seq, _ = x.shape
    p = dict(norm_mix_pre=norm_mix_pre, w_in=w_in, b_ml_i=b_ml_i, b_ml_f=b_ml_f,
             ml_head_norm=ml_head_norm, b_fox_f=b_fox_f, b_gate_a=b_gate_a, b_gate_b=b_gate_b,
             w_branch_a=w_branch_a, w_branch_b=w_branch_b, w_out=w_out,
             norm_mix_post=norm_mix_post, norm_ffn_pre=norm_ffn_pre, w_up=w_up, conv_w=conv_w,
             conv_b=conv_b, w_down=w_down, norm_ffn_post=norm_ffn_post)
    x2 = x.reshape(batch * seq, D_MODEL)
    for l in range(w_in.shape[0]):
        x2 = _layer(x2, p, l, batch=batch, seq=seq)
    return x2.reshape(batch, seq, D_MODEL)
```

```python
import functools

import jax
import jax.numpy as jnp
from jax import lax
from jax.experimental import pallas as pl
from jax.experimental.pallas import tpu as pltpu

D_MODEL = 1024
ML_HEADS = 4
ML_DQK = 128
ML_DV = 256
ML_QK = ML_HEADS * ML_DQK
ML_V = ML_HEADS * ML_DV
FOX_HEADS = 8
FOX_DH = 128
FOX_W = FOX_HEADS * FOX_DH
D_FF = 2816
CONV_W = 3
GATE_CAP = 15.0
EPS = 1e-6
LOG2E = 1.4426950408889634
GELU_CUBIC = 0.044715
GELU_K1 = -2.0 * (2.0 / 3.141592653589793) ** 0.5 * LOG2E

LANES = 128
SUBLANES = 8
GATE_W = LANES
N_GATES = 2 * ML_HEADS + FOX_HEADS
ML_CHUNK = 128
VMEM_LIMIT = 56 * 1024 * 1024

PROJ_W = 2 * ML_QK + 2 * ML_V + 3 * FOX_W + 2 * D_MODEL
COL_QM, COL_KM, COL_VM, COL_OM = 0, ML_QK, 2 * ML_QK, 2 * ML_QK + ML_V
COL_QF = COL_OM + ML_V
COL_KF = COL_QF + FOX_W
COL_VF = COL_KF + FOX_W
COL_GA = COL_VF + FOX_W
COL_GB = COL_GA + D_MODEL

f32 = jnp.float32
bf16 = jnp.bfloat16


def _log_sigmoid(z):
    return jnp.minimum(z, 0.0) - jnp.log1p(jnp.exp(-jnp.abs(z)))


def _rms(v):
    return v * lax.rsqrt(jnp.mean(v * v, axis=-1, keepdims=True) + EPS)


def _dot(a, b):
    return jnp.dot(a, b, preferred_element_type=f32)


def _dot_nt(a, b):
    return lax.dot_general(a, b, (((1,), (1,)), ((), ())), preferred_element_type=f32)


def _dot_tn(a, b):
    return lax.dot_general(a, b, (((0,), (0,)), ((), ())), preferred_element_type=f32)


def _inproj_kernel(*refs, blocks_per_seq, piece, w_blocks):
    x_ref, nw_ref, wg_ref, bg_ref, tri_ref, cs_ref = refs[:6]
    w_refs = refs[6:6 + len(w_blocks)]
    proj_ref, gcol_ref, grow_ref, h_sc, carry_sc = refs[6 + len(w_blocks):]
    assert w_blocks[0][0] == 0
    i = pl.program_id(0)
    j = pl.program_id(1)
    tm = x_ref.shape[0]

    def project(hb, w_ref=w_refs[0]):
        return (_dot(hb, w_ref[...]) * cs_ref[...]).astype(bf16)

    @pl.when(j == 0)
    def _():
        @pl.when(i % blocks_per_seq == 0)
        def _():
            carry_sc[...] = jnp.zeros_like(carry_sc)

        tri = tri_ref[...]
        lane = lax.broadcasted_iota(jnp.int32, (ML_CHUNK, GATE_W), 1)
        per_piece = piece // ML_CHUNK

        def normed(p):
            prow = slice(p * piece, (p + 1) * piece)
            hb = (_rms(x_ref[prow, :]) * nw_ref[...]).astype(bf16)
            h_sc[prow, :] = hb
            return hb

        acts = []
        hb = normed(0)
        for p in range(tm // piece):
            proj_ref[p * piece:(p + 1) * piece, :] = project(hb)
            g_piece = _dot(hb, wg_ref[...]) + bg_ref[...]
            if (p + 1) * piece < tm:
                hb = normed(p + 1)
            for q in range(per_piece):
                g = g_piece[q * ML_CHUNK:(q + 1) * ML_CHUNK]
                cap = GATE_CAP * jnp.tanh(g / GATE_CAP)
                a = jnp.where(lane < ML_HEADS, cap,
                              jnp.where(lane < 2 * ML_HEADS, _log_sigmoid(cap), _log_sigmoid(g)))
                acts.append(jnp.where(lane < N_GATES, a, 0.0))

        for r in range(tm // ML_CHUNK):
            rows = slice(r * ML_CHUNK, (r + 1) * ML_CHUNK)
            a = acts[r]
            a_hi = a.astype(bf16)
            r1 = a - a_hi.astype(f32)
            a_mid = r1.astype(bf16)
            a_lo = (r1 - a_mid.astype(f32)).astype(bf16)
            cs3 = _dot(tri, jnp.concatenate([a_hi, a_mid, a_lo], axis=1))
            cs = cs3[:, :GATE_W] + cs3[:, GATE_W:2 * GATE_W] + cs3[:, 2 * GATE_W:]
            glob = cs + carry_sc[...]
            carry_sc[...] = glob[ML_CHUNK - 1:ML_CHUNK, :]
            out = jnp.where(lane < ML_HEADS, a, jnp.where(lane < 2 * ML_HEADS, cs, glob))
            gcol_ref[rows, :] = out
            grow_ref[:, rows] = out.T[:N_GATES, :]

    for w_ref, (first, count) in zip(w_refs, w_blocks):
        @pl.when((j >= max(first, 1)) & (j < first + count))
        def _(w_ref=w_ref):
            proj_ref[...] = project(h_sc[...], w_ref)


def _proj_col_scale():
    cs = jnp.ones((1, PROJ_W), f32)
    cs = cs.at[:, COL_QM:COL_QM + ML_QK].set(ML_DQK ** -0.5)
    return cs.at[:, COL_QF:COL_QF + FOX_W].set(LOG2E * FOX_DH ** -0.5)


def _inproj(x2, nw, w_pieces, w_gate, b_gate, tri, *, seq, tm=2048, tn=1024, piece=512):
    t = x2.shape[0]
    tm = min(tm, seq)
    assert seq % tm == 0 and tm % piece == 0
    piece = max(piece, ML_CHUNK)
    w_blocks, first = [], 0
    for w in w_pieces:
        w_blocks.append((first, w.shape[1] // tn))
        first += w.shape[1] // tn
    assert first == PROJ_W // tn
    kern = functools.partial(_inproj_kernel, blocks_per_seq=seq // tm, piece=piece, w_blocks=tuple(w_blocks))

    def w_spec(first, count):
        return pl.BlockSpec((D_MODEL, tn), lambda i, j: (
            0, jnp.where(j < first - 1, count - 1, jnp.clip(j - first, 0, count - 1))))

    return pl.pallas_call(
        kern,
        grid=(t // tm, PROJ_W // tn),
        in_specs=[
            pl.BlockSpec((tm, D_MODEL), lambda i, j: (i, 0)),
            pl.BlockSpec((1, D_MODEL), lambda i, j: (0, 0)),
            pl.BlockSpec((D_MODEL, GATE_W), lambda i, j: (0, 0)),
            pl.BlockSpec((1, GATE_W), lambda i, j: (0, 0)),
            pl.BlockSpec((ML_CHUNK, ML_CHUNK), lambda i, j: (0, 0)),
            pl.BlockSpec((1, tn), lambda i, j: (0, j)),
        ] + [w_spec(f, n) for f, n in w_blocks],
        out_specs=[
            pl.BlockSpec((tm, tn), lambda i, j: (i, j)),
            pl.BlockSpec((tm, GATE_W), lambda i, j: (i, 0)),
            pl.BlockSpec((N_GATES, tm), lambda i, j: (0, i)),
        ],
        out_shape=[
            jax.ShapeDtypeStruct((t, PROJ_W), bf16),
            jax.ShapeDtypeStruct((t, GATE_W), f32),
            jax.ShapeDtypeStruct((N_GATES, t), f32),
        ],
        scratch_shapes=[pltpu.VMEM((tm, D_MODEL), bf16), pltpu.VMEM((1, GATE_W), f32)],
        compiler_params=pltpu.CompilerParams(
            dimension_semantics=("arbitrary", "arbitrary"), vmem_limit_bytes=VMEM_LIMIT),
        name="inproj",
    )(x2, nw, w_gate, b_gate, tri, _proj_col_scale(), *w_pieces)


def _mlstm_kernel(*refs):
    s_sc, m_sc = refs[-2:]

    @pl.when(pl.program_id(1) == 0)
    def _():
        s_sc[...] = jnp.zeros_like(s_sc)
        m_sc[...] = jnp.zeros_like(m_sc)

    def chunk(ci, _):
        _mlstm_chunk(ci, *refs)
        return 0

    lax.fori_loop(0, refs[0].shape[1] // ML_CHUNK, chunk, 0)


def _mlstm_chunk(ci, q_ref, k_ref, v_ref, o_ref, gcol_ref, grow_ref, hnw_ref, spread_ref, out_ref, s_sc, m_sc):
    L = ML_CHUNK
    nb = q_ref.shape[0]
    rows = pl.ds(pl.multiple_of(ci * L, L), L)
    assert L == LANES

    row = lax.broadcasted_iota(jnp.int32, (L, L), 0)
    col = lax.broadcasted_iota(jnp.int32, (L, L), 1)
    causal = col <= row
    ones_blk = jnp.ones((L, LANES), bf16)

    def wide(tile, n):
        return jnp.concatenate([tile] * n, axis=1)

    P = [(n, h) for n in range(nb) for h in range(ML_HEADS)]
    C = range(len(P))
    qk_cols = lambda h: slice(h * ML_DQK, (h + 1) * ML_DQK)
    v_cols = lambda h: slice(h * ML_DV, (h + 1) * ML_DV)
    spread = []
    for n in range(nb):
        g = gcol_ref[n, rows, :]
        g_hi = g.astype(bf16)
        r1 = g - g_hi.astype(f32)
        g_mid = r1.astype(bf16)
        g_lo = (r1 - g_mid.astype(f32)).astype(bf16)
        spread.append(_dot(jnp.concatenate([g_hi, g_mid, g_lo], axis=1), spread_ref[...]))
    tile = lambda n, j: spread[n][:, j * LANES:(j + 1) * LANES]
    b_t = [tile(n, h) for n, h in P]
    a_t = [tile(n, ML_HEADS + h) for n, h in P]
    gr = [grow_ref[n, ci] for n in range(nb)]
    a_r = [gr[n][h:h + 1, :] - gr[n][ML_HEADS + h:ML_HEADS + h + 1, :] for n, h in P]
    q = [q_ref[n, rows, qk_cols(h)] for n, h in P]
    k = [k_ref[n, rows, qk_cols(h)] for n, h in P]
    v_aug = [jnp.concatenate([v_ref[n, rows, v_cols(h)], ones_blk], axis=1) for n, h in P]
    m_prev = [m_sc[i:i + 1, :] for i in C]
    state = [s_sc[i] for i in C]

    qk = [_dot_nt(q[i], k[i]) for i in C]
    amat = [jnp.where(causal, a_r[i], -jnp.inf) for i in C]
    mx = [jnp.maximum(m_prev[i], jnp.broadcast_to(jnp.max(amat[i], axis=-1, keepdims=True), (L, LANES)))
          for i in C]

    mx_last = [mx[i][L - 1:L, :] for i in C]
    b_last = [b_t[i][L - 1:L, :] for i in C]
    wk = [jnp.exp(a_t[i] - mx_last[i]) for i in C]
    upd = [_dot_tn((wk[i] * k[i].astype(f32)).astype(bf16), v_aug[i]) for i in C]

    s = [(qk[i] * jnp.exp(amat[i] - mx[i])).astype(bf16) for i in C]
    q_in = [(jnp.exp(m_prev[i] - mx[i]) * q[i].astype(f32)).astype(bf16) for i in C]
    num = [_dot(jnp.concatenate([s[i], q_in[i]], axis=1),
                jnp.concatenate([v_aug[i], state[i].astype(bf16)], axis=0)) for i in C]
    for i in C:
        s_sc[i] = wide(jnp.exp(m_prev[i] - mx_last[i]), 3) * state[i] + upd[i]
        m_sc[i:i + 1, :] = b_last[i] + mx_last[i]

    inv = [1.0 / jnp.maximum(jnp.abs(num[i][:, ML_DV:]), jnp.exp(-b_t[i] - mx[i])) for i in C]
    msq = [jnp.broadcast_to(jnp.mean(num[i][:, :ML_DV] * num[i][:, :ML_DV], axis=-1, keepdims=True),
                            (L, LANES)) for i in C]
    fac = [inv[i] * lax.rsqrt(inv[i] * inv[i] * msq[i] + EPS) for i in C]
    for i, (n, h) in enumerate(P):
        gate = jax.nn.sigmoid(o_ref[n, rows, v_cols(h)].astype(f32)) * hnw_ref[:, v_cols(h)]
        out_ref[n, rows, v_cols(h)] = (num[i][:, :ML_DV] * wide(fac[i], ML_DV // LANES) * gate).astype(bf16)


def _mlstm(proj, gcol, grow, hnw, *, batch, seq, nb=4, cps=4):
    while batch % nb:
        nb //= 2
    nc = seq // ML_CHUNK
    while nc % cps:
        cps //= 2
    L = cps * ML_CHUNK
    proj4 = proj.reshape(batch // nb, nb, seq, PROJ_W)
    gcol4 = gcol.reshape(batch // nb, nb, seq, GATE_W)
    grow4 = grow.reshape(N_GATES, batch // nb, nb, nc, ML_CHUNK).transpose(1, 2, 3, 0, 4)
    src = lax.broadcasted_iota(jnp.int32, (GATE_W, 2 * ML_HEADS * LANES), 0)
    dst = lax.broadcasted_iota(jnp.int32, (GATE_W, 2 * ML_HEADS * LANES), 1) // LANES
    spread = (jnp.where(src == ML_HEADS + dst % ML_HEADS, jnp.where(dst < ML_HEADS, 1.0, -1.0), 0.0)
              + jnp.where((dst >= ML_HEADS) & (src == dst - ML_HEADS), 1.0, 0.0)).astype(bf16)
    spread = jnp.concatenate([spread] * 3, axis=0)
    out = pl.pallas_call(
        _mlstm_kernel,
        grid=(batch // nb, nc // cps),
        in_specs=[
            pl.BlockSpec((None, nb, L, ML_QK), lambda b, c: (b, 0, c, COL_QM // ML_QK)),
            pl.BlockSpec((None, nb, L, ML_QK), lambda b, c: (b, 0, c, COL_KM // ML_QK)),
            pl.BlockSpec((None, nb, L, ML_V), lambda b, c: (b, 0, c, COL_VM // ML_V)),
            pl.BlockSpec((None, nb, L, ML_V), lambda b, c: (b, 0, c, COL_OM // ML_V)),
            pl.BlockSpec((None, nb, L, GATE_W), lambda b, c: (b, 0, c, 0)),
            pl.BlockSpec((None, nb, cps, N_GATES, ML_CHUNK), lambda b, c: (b, 0, c, 0, 0)),
            pl.BlockSpec((1, ML_V), lambda b, c: (0, 0)),
            pl.BlockSpec((3 * GATE_W, 2 * ML_HEADS * LANES), lambda b, c: (0, 0)),
        ],
        out_specs=pl.BlockSpec((None, nb, L, ML_V), lambda b, c: (b, 0, c, 0)),
        out_shape=jax.ShapeDtypeStruct((batch // nb, nb, seq, ML_V), bf16),
        scratch_shapes=[pltpu.VMEM((nb * ML_HEADS, ML_DQK, ML_DV + LANES), f32),
                        pltpu.VMEM((nb * ML_HEADS, LANES), f32)],
        compiler_params=pltpu.CompilerParams(
            dimension_semantics=("arbitrary", "arbitrary"), vmem_limit_bytes=VMEM_LIMIT),
        name="mlstm",
    )(proj4, proj4, proj4, proj4, gcol4, grow4, hnw, spread)
    return out.reshape(batch * seq, ML_V)


def _fox_kernel(q_ref, k_ref, v_ref, gcol_ref, out_ref, kaug_sc, st_sc, mx_sc, m_sc, l_sc, acc_sc,
                *, blk, setup_rows, ahead):
    hg = pl.program_id(1)
    seq = k_ref.shape[0]
    nh = kaug_sc.shape[0]
    head_cols = [slice(i * FOX_DH, (i + 1) * FOX_DH) for i in range(nh)]

    r = lax.broadcasted_iota(jnp.int32, (GATE_W, nh * LANES), 0)
    c = lax.broadcasted_iota(jnp.int32, (GATE_W, nh * LANES), 1)
    src = 2 * ML_HEADS + hg * nh + c // LANES
    pick = jnp.concatenate([((r == src) & (c % LANES == j)).astype(bf16) for j in range(3)], axis=0)
    for n in range(seq // setup_rows):
        rows = pl.ds(n * setup_rows, setup_rows)
        a = gcol_ref[rows, :] * (-LOG2E)
        a_hi = a.astype(bf16)
        r1 = a - a_hi.astype(f32)
        a_mid = r1.astype(bf16)
        a_lo = (r1 - a_mid.astype(f32)).astype(bf16)
        aug = _dot(jnp.concatenate([a_hi, a_mid, a_lo], axis=1), pick).astype(bf16)
        for i in range(nh):
            kaug_sc[i, rows, :FOX_DH] = k_ref[rows, head_cols[i]]
            kaug_sc[i, rows, FOX_DH:] = aug[:, i * LANES:(i + 1) * LANES]

    ones3 = (lax.broadcasted_iota(jnp.int32, (blk, LANES), 1) < 3).astype(bf16)

    def query_block(qi, _):
        qrows = pl.ds(pl.multiple_of(qi * blk, blk), blk)
        q_aug = [jnp.concatenate([q_ref[qrows, head_cols[i]], ones3], axis=1) for i in range(nh)]

        m_sc[...] = jnp.full(m_sc.shape, -jnp.inf, f32)
        l_sc[...] = jnp.zeros(l_sc.shape, f32)
        acc_sc[...] = jnp.zeros(acc_sc.shape, f32)

        def logits_stage(i, j, masked):
            start = pl.multiple_of(j * blk, blk)
            st = _dot_nt(kaug_sc[i, pl.ds(start, blk), :], q_aug[i])
            if masked:
                row = lax.broadcasted_iota(jnp.int32, (blk, blk), 0)
                col = lax.broadcasted_iota(jnp.int32, (blk, blk), 1)
                st = jnp.where(row <= col, st, -jnp.inf)
            st_sc[i] = st
            mx_sc[i] = jnp.max(st, axis=0, keepdims=True)

        def softmax_stage(i, j):
            start = pl.multiple_of(j * blk, blk)
            m = m_sc[i]
            m_new = jnp.maximum(m, mx_sc[i])
            alpha = jnp.exp2(m - m_new)
            p = jnp.exp2(st_sc[i] - m_new)
            l_sc[i] = alpha * l_sc[i] + jnp.sum(p, axis=0, keepdims=True)
            acc_sc[i] = alpha * acc_sc[i] + _dot_tn(v_ref[pl.ds(start, blk), head_cols[i]], p.astype(bf16))
            m_sc[i] = m_new

        def round_robin(cur, nxt, masked):
            for i in range(nh):
                if i + ahead < nh:
                    logits_stage(i + ahead, cur, masked)
                else:
                    logits_stage(i + ahead - nh, nxt, False)
                softmax_stage(i, cur)

        for i in range(ahead):
            logits_stage(i, qi, True)
        round_robin(qi, 0, True)

        def body(k, _):
            round_robin(k - 1, jnp.minimum(k, qi - 1), False)
            return 0

        lax.fori_loop(1, qi + 1, body, 0)

        for i in range(nh):
            out_ref[qrows, head_cols[i]] = (acc_sc[i] * (1.0 / l_sc[i])).T.astype(bf16)
        return 0

    lax.fori_loop(0, seq // blk, query_block, 0)


def _fox(proj, gcol, *, batch, seq, blk=512, nh=4, ahead=1):
    assert 1 <= ahead < nh
    assert seq % blk == 0 and FOX_HEADS % nh == 0
    t = proj.shape[0]
    w = nh * FOX_DH
    kern = functools.partial(_fox_kernel, blk=blk, setup_rows=blk, ahead=ahead)
    return pl.pallas_call(
        kern,
        grid=(batch, FOX_HEADS // nh),
        in_specs=[
            pl.BlockSpec((seq, w), lambda b, h: (b, COL_QF // w + h)),
            pl.BlockSpec((seq, w), lambda b, h: (b, COL_KF // w + h)),
            pl.BlockSpec((seq, w), lambda b, h: (b, COL_VF // w + h)),
            pl.BlockSpec((seq, GATE_W), lambda b, h: (b, 0)),
        ],
        out_specs=pl.BlockSpec((seq, w), lambda b, h: (b, h)),
        out_shape=jax.ShapeDtypeStruct((t, FOX_W), bf16),
        scratch_shapes=[pltpu.VMEM((nh, seq, FOX_DH + LANES), bf16),
                        pltpu.VMEM((nh, blk, blk), f32),
                        pltpu.VMEM((nh, 1, blk), f32),
                        pltpu.VMEM((nh, 1, blk), f32),
                        pltpu.VMEM((nh, 1, blk), f32),
                        pltpu.VMEM((nh, FOX_DH, blk), f32)],
        compiler_params=pltpu.CompilerParams(
            dimension_semantics=("arbitrary", "arbitrary"), vmem_limit_bytes=VMEM_LIMIT),
        name="fox",
    )(proj, proj, proj, gcol)


def _merge_kernel(x_ref, ha_ref, hb_ref, ga_ref, gb_ref, wa_ref, wb_ref, wo_ref,
                  bga_ref, bgb_ref, npost_ref, npre_ref, x1_ref, h2_ref, *, sub):
    n_sub = x_ref.shape[0] // sub
    rows = [slice(r * sub, (r + 1) * sub) for r in range(n_sub)]

    def branches(r):
        return _dot(ha_ref[rows[r], :], wa_ref[...]), _dot(hb_ref[rows[r], :], wb_ref[...])

    def finish(r, z):
        x1 = x_ref[rows[r], :] + _rms(z) * npost_ref[...]
        x1_ref[rows[r], :] = x1
        h2_ref[rows[r], :] = (_rms(x1) * npre_ref[...]).astype(bf16)

    y_next = branches(0)
    z_prev = None
    for r in range(n_sub):
        ya, yb = y_next
        if r + 1 < n_sub:
            y_next = branches(r + 1)
        merged = (jax.nn.sigmoid(ga_ref[rows[r], :].astype(f32) + bga_ref[...]) * ya
                  + jax.nn.sigmoid(gb_ref[rows[r], :].astype(f32) + bgb_ref[...]) * yb)
        z = _dot(merged.astype(bf16), wo_ref[...])
        if z_prev is not None:
            finish(r - 1, z_prev)
        z_prev = z
    finish(n_sub - 1, z_prev)


def _merge(x2, h_a, h_b, proj, wa, wb, wo, bga, bgb, npost, npre, *, tm=1024, sub=512):
    t = x2.shape[0]
    tm = min(tm, t)
    tok = lambda i: (i, 0)
    const = lambda i: (0, 0)
    wspec = pl.BlockSpec((D_MODEL, D_MODEL), const, pipeline_mode=pl.Buffered(1))
    vspec = pl.BlockSpec((1, D_MODEL), const)
    return pl.pallas_call(
        functools.partial(_merge_kernel, sub=sub),
        grid=(t // tm,),
        in_specs=[
            pl.BlockSpec((tm, D_MODEL), tok),
            pl.BlockSpec((tm, ML_V), tok),
            pl.BlockSpec((tm, FOX_W), tok),
            pl.BlockSpec((tm, D_MODEL), lambda i: (i, COL_GA // D_MODEL)),
            pl.BlockSpec((tm, D_MODEL), lambda i: (i, COL_GB // D_MODEL)),
            wspec, wspec, wspec, vspec, vspec, vspec, vspec,
        ],
        out_specs=[pl.BlockSpec((tm, D_MODEL), tok), pl.BlockSpec((tm, D_MODEL), tok)],
        out_shape=[jax.ShapeDtypeStruct((t, D_MODEL), f32), jax.ShapeDtypeStruct((t, D_MODEL), bf16)],
        compiler_params=pltpu.CompilerParams(
            dimension_semantics=("arbitrary",), vmem_limit_bytes=VMEM_LIMIT),
        name="merge",
    )(x2, h_a, h_b, proj, proj, wa, wb, wo, bga, bgb, npost, npre)


def _ffn_kernel(*refs, blocks_per_seq, tf, down_group):
    halo_sc, act_sc = refs[8], refs[10]

    @pl.when(pl.program_id(0) % blocks_per_seq == 0)
    def _():
        halo_sc[...] = jnp.zeros_like(halo_sc)

    def piece(s, _):
        _ffn_rows(s, *refs, tf=tf, down_group=down_group)
        return 0

    lax.fori_loop(0, refs[0].shape[0] // act_sc.shape[0], piece, 0)


def _ffn_rows(s, x1_ref, h2_ref, wup_ref, cw_ref, cb_ref, wdn_ref, npost_ref, out_ref,
              halo_sc, ubuf_sc, act_sc, acc_sc, *, tf, down_group):
    tm = act_sc.shape[0]
    rows = pl.ds(pl.multiple_of(s * tm, tm), tm)
    h2 = h2_ref[rows, :]
    n_chunks = D_FF // tf

    def up(c):
        for half in range(2):
            col0 = half * D_FF + c * tf
            slot = 2 * (c % 2) + half
            u = _dot(h2, wup_ref[:, col0:col0 + tf])
            ubuf_sc[slot, 0:SUBLANES, :] = halo_sc[:, col0:col0 + tf]
            ubuf_sc[slot, SUBLANES:SUBLANES + tm, :] = u
            halo_sc[:, col0:col0 + tf] = u[tm - SUBLANES:tm, :]

    def conv(c, half):
        col0 = half * D_FF + c * tf
        slot = 2 * (c % 2) + half
        w = cw_ref[:, col0:col0 + tf]
        y = cb_ref[:, col0:col0 + tf]
        for j in range(CONV_W):
            first = SUBLANES - (CONV_W - 1 - j)
            y = y + w[j:j + 1] * ubuf_sc[slot, first:first + tm, :]
        return y

    def gate(g, a):
        inner = g * (GELU_K1 + (GELU_K1 * GELU_CUBIC) * (g * g))
        return (g * a) * (1.0 / (1.0 + jnp.exp2(inner)))

    def down(c0, c1):
        return _dot(act_sc[:, c0 * tf:c1 * tf], wdn_ref[c0 * tf:c1 * tf, :])

    up(0)
    pending, done = None, 0
    for c in range(n_chunks):
        if c + 1 < n_chunks:
            up(c + 1)
        if pending is not None:
            if pending[0] == 0:
                acc_sc[...] = down(*pending)
            else:
                acc_sc[...] += down(*pending)
            pending = None
        act_sc[:, c * tf:(c + 1) * tf] = gate(conv(c, 1), conv(c, 0)).astype(bf16)
        if (c + 1) % down_group == 0 or c + 1 == n_chunks:
            pending, done = (done, c + 1), c + 1

    y = down(*pending)
    if pending[0] != 0:
        y = y + acc_sc[...]
    out_ref[rows, :] = x1_ref[rows, :] + _rms(y) * npost_ref[...]


def _ffn(x1, h2, wup, cw, cb, wdn, npost, *, seq, tm=1024, sub=512, tf=256, down_group=2):
    t = x1.shape[0]
    tm = min(tm, seq)
    assert seq % tm == 0 and tm % sub == 0
    tok = lambda i: (i, 0)
    const = lambda i: (0, 0)
    kern = functools.partial(_ffn_kernel, blocks_per_seq=seq // tm, tf=tf, down_group=down_group)
    return pl.pallas_call(
        kern,
        grid=(t // tm,),
        in_specs=[
            pl.BlockSpec((tm, D_MODEL), tok),
            pl.BlockSpec((tm, D_MODEL), tok),
            pl.BlockSpec((D_MODEL, 2 * D_FF), const, pipeline_mode=pl.Buffered(1)),
            pl.BlockSpec((CONV_W, 2 * D_FF), const),
            pl.BlockSpec((1, 2 * D_FF), const),
            pl.BlockSpec((D_FF, D_MODEL), const, pipeline_mode=pl.Buffered(1)),
            pl.BlockSpec((1, D_MODEL), const),
        ],
        out_specs=pl.BlockSpec((tm, D_MODEL), tok),
        out_shape=jax.ShapeDtypeStruct((t, D_MODEL), f32),
        scratch_shapes=[pltpu.VMEM((SUBLANES, 2 * D_FF), f32),
                        pltpu.VMEM((4, sub + SUBLANES, tf), f32),
                        pltpu.VMEM((sub, D_FF), bf16),
                        pltpu.VMEM((sub, D_MODEL), f32)],
        compiler_params=pltpu.CompilerParams(
            dimension_semantics=("arbitrary",), vmem_limit_bytes=VMEM_LIMIT),
        name="ffn",
    )(x1, h2, wup, cw, cb, wdn, npost)


def _layer(x2, p, l, *, batch, seq):
    w_in = p["w_in"][l]
    o_i = 2 * ML_QK + ML_V
    o_o = o_i + 2 * ML_HEADS
    o_qf = o_o + ML_V
    o_ff = o_qf + 3 * FOX_W
    o_ga = o_ff + FOX_HEADS
    w_pieces = [
        w_in[:, :o_i].astype(bf16),
        w_in[:, o_o:o_qf].astype(bf16),
        w_in[:, o_qf:o_ff].astype(bf16),
        w_in[:, o_ga:].astype(bf16),
    ]
    w_gate = jnp.concatenate([
        w_in[:, o_i:o_o], w_in[:, o_ff:o_ga],
        jnp.zeros((D_MODEL, GATE_W - N_GATES), f32)], axis=1).astype(bf16)
    b_gate = jnp.concatenate([
        p["b_ml_i"][l], p["b_ml_f"][l], p["b_fox_f"][l], jnp.zeros((GATE_W - N_GATES,), f32)])[None, :]
    tri = jnp.tril(jnp.ones((ML_CHUNK, ML_CHUNK), f32)).astype(bf16)
    row = lambda v: v[None, :].astype(f32)

    proj, gcol, grow = _inproj(x2, row(p["norm_mix_pre"][l]), w_pieces, w_gate, b_gate, tri, seq=seq)
    h_a = _mlstm(proj, gcol, grow, row(p["ml_head_norm"][l]), batch=batch, seq=seq)
    h_b = _fox(proj, gcol, batch=batch, seq=seq)
    x1, h2 = _merge(x2, h_a, h_b, proj,
                    p["w_branch_a"][l].astype(bf16), p["w_branch_b"][l].astype(bf16),
                    p["w_out"][l].astype(bf16), row(p["b_gate_a"][l]), row(p["b_gate_b"][l]),
                    row(p["norm_mix_post"][l]), row(p["norm_ffn_pre"][l]))
    return _ffn(x1, h2, p["w_up"][l].astype(bf16), p["conv_w"][l], row(p["conv_b"][l]),
                p["w_down"][l].astype(bf16), row(p["norm_ffn_post"][l]), seq=seq)


def kernel(x, norm_mix_pre, w_in, b_ml_i, b_ml_f, ml_head_norm, b_fox_f, b_gate_a, b_gate_b,
           w_branch_a, w_branch_b, w_out, norm_mix_post, norm_ffn_pre, w_up, conv_w, conv_b,
           w_down, norm_ffn_post):
    batch, seq, _ = x.shape
    p = dict(norm_mix_pre=norm_mix_pre, w_in=w_in, b_ml_i=b_ml_i, b_ml_f=b_ml_f,
             ml_head_norm=ml_head_norm, b_fox_f=b_fox_f, b_gate_a=b_gate_a, b_gate_b=b_gate_b,
             w_branch_a=w_branch_a, w_branch_b=w_branch_b, w_out=w_out,
             norm_mix_post=norm_mix_post, norm_ffn_pre=norm_ffn_pre, w_up=w_up, conv_w=conv_w,
             conv_b=conv_b, w_down=w_down, norm_ffn_post=norm_ffn_post)
    x2 = x.reshape(batch * seq, D_MODEL)
    for l in range(w_in.shape[0]):
        x2 = _layer(x2, p, l, batch=batch, seq=seq)
    return x2.reshape(batch, seq, D_MODEL)
```

```python
import functools

import jax
import jax.numpy as jnp
from jax import lax
from jax.experimental import pallas as pl
from jax.experimental.pallas import tpu as pltpu

D_MODEL = 1024
ML_HEADS = 4
ML_DQK = 128
ML_DV = 256
ML_QK = ML_HEADS * ML_DQK
ML_V = ML_HEADS * ML_DV
FOX_HEADS = 8
FOX_DH = 128
FOX_W = FOX_HEADS * FOX_DH
D_FF = 2816
CONV_W = 3
GATE_CAP = 15.0
EPS = 1e-6
LOG2E = 1.4426950408889634
GELU_CUBIC = 0.044715
GELU_K1 = -2.0 * (2.0 / 3.141592653589793) ** 0.5 * LOG2E

LANES = 128
SUBLANES = 8
GATE_W = LANES
N_GATES = 2 * ML_HEADS + FOX_HEADS
ML_CHUNK = 128
VMEM_LIMIT = 56 * 1024 * 1024

PROJ_W = 2 * ML_QK + 2 * ML_V + 3 * FOX_W + 2 * D_MODEL
COL_QM, COL_KM, COL_VM, COL_OM = 0, ML_QK, 2 * ML_QK, 2 * ML_QK + ML_V
COL_QF = COL_OM + ML_V
COL_KF = COL_QF + FOX_W
COL_VF = COL_KF + FOX_W
COL_GA = COL_VF + FOX_W
COL_GB = COL_GA + D_MODEL

f32 = jnp.float32
bf16 = jnp.bfloat16


def _log_sigmoid(z):
    return jnp.minimum(z, 0.0) - jnp.log1p(jnp.exp(-jnp.abs(z)))


def _rms(v):
    return v * lax.rsqrt(jnp.mean(v * v, axis=-1, keepdims=True) + EPS)


def _dot(a, b):
    return jnp.dot(a, b, preferred_element_type=f32)


def _dot_nt(a, b):
    return lax.dot_general(a, b, (((1,), (1,)), ((), ())), preferred_element_type=f32)


def _dot_tn(a, b):
    return lax.dot_general(a, b, (((0,), (0,)), ((), ())), preferred_element_type=f32)


def _inproj_kernel(*refs, blocks_per_seq, piece, w_blocks):
    x_ref, nw_ref, wg_ref, bg_ref, tri_ref, cs_ref = refs[:6]
    w_refs = refs[6:6 + len(w_blocks)]
    proj_ref, gcol_ref, grow_ref, h_sc, carry_sc = refs[6 + len(w_blocks):]
    assert w_blocks[0][0] == 0
    i = pl.program_id(0)
    j = pl.program_id(1)
    tm = x_ref.shape[0]

    def project(hb, w_ref=w_refs[0]):
        return (_dot(hb, w_ref[...]) * cs_ref[...]).astype(bf16)

    @pl.when(j == 0)
    def _():
        @pl.when(i % blocks_per_seq == 0)
        def _():
            carry_sc[...] = jnp.zeros_like(carry_sc)

        tri = tri_ref[...]
        lane = lax.broadcasted_iota(jnp.int32, (ML_CHUNK, GATE_W), 1)
        per_piece = piece // ML_CHUNK

        def normed(p):
            prow = slice(p * piece, (p + 1) * piece)
            hb = (_rms(x_ref[prow, :]) * nw_ref[...]).astype(bf16)
            h_sc[prow, :] = hb
            return hb

        acts = []
        hb = normed(0)
        for p in range(tm // piece):
            proj_ref[p * piece:(p + 1) * piece, :] = project(hb)
            g_piece = _dot(hb, wg_ref[...]) + bg_ref[...]
            if (p + 1) * piece < tm:
                hb = normed(p + 1)
            for q in range(per_piece):
                g = g_piece[q * ML_CHUNK:(q + 1) * ML_CHUNK]
                cap = GATE_CAP * jnp.tanh(g / GATE_CAP)
                a = jnp.where(lane < ML_HEADS, cap,
                              jnp.where(lane < 2 * ML_HEADS, _log_sigmoid(cap), _log_sigmoid(g)))
                acts.append(jnp.where(lane < N_GATES, a, 0.0))

        for r in range(tm // ML_CHUNK):
            rows = slice(r * ML_CHUNK, (r + 1) * ML_CHUNK)
            a = acts[r]
            a_hi = a.astype(bf16)
            r1 = a - a_hi.astype(f32)
            a_mid = r1.astype(bf16)
            a_lo = (r1 - a_mid.astype(f32)).astype(bf16)
            cs3 = _dot(tri, jnp.concatenate([a_hi, a_mid, a_lo], axis=1))
            cs = cs3[:, :GATE_W] + cs3[:, GATE_W:2 * GATE_W] + cs3[:, 2 * GATE_W:]
            glob = cs + carry_sc[...]
            carry_sc[...] = glob[ML_CHUNK - 1:ML_CHUNK, :]
            out = jnp.where(lane < ML_HEADS, a, jnp.where(lane < 2 * ML_HEADS, cs, glob))
            gcol_ref[rows, :] = out
            grow_ref[:, rows] = out.T[:N_GATES, :]

    for w_ref, (first, count) in zip(w_refs, w_blocks):
        @pl.when((j >= max(first, 1)) & (j < first + count))
        def _(w_ref=w_ref):
            proj_ref[...] = project(h_sc[...], w_ref)


def _proj_col_scale():
    cs = jnp.ones((1, PROJ_W), f32)
    cs = cs.at[:, COL_QM:COL_QM + ML_QK].set(ML_DQK ** -0.5)
    return cs.at[:, COL_QF:COL_QF + FOX_W].set(LOG2E * FOX_DH ** -0.5)


def _inproj(x2, nw, w_pieces, w_gate, b_gate, tri, *, seq, tm=2048, tn=1024, piece=512):
    t = x2.shape[0]
    tm = min(tm, seq)
    assert seq % tm == 0 and tm % piece == 0
    piece = max(piece, ML_CHUNK)
    w_blocks, first = [], 0
    for w in w_pieces:
        w_blocks.append((first, w.shape[1] // tn))
        first += w.shape[1] // tn
    assert first == PROJ_W // tn
    kern = functools.partial(_inproj_kernel, blocks_per_seq=seq // tm, piece=piece, w_blocks=tuple(w_blocks))

    def w_spec(first, count):
        return pl.BlockSpec((D_MODEL, tn), lambda i, j: (
            0, jnp.where(j < first - 1, count - 1, jnp.clip(j - first, 0, count - 1))))

    return pl.pallas_call(
        kern,
        grid=(t // tm, PROJ_W // tn),
        in_specs=[
            pl.BlockSpec((tm, D_MODEL), lambda i, j: (i, 0)),
            pl.BlockSpec((1, D_MODEL), lambda i, j: (0, 0)),
            pl.BlockSpec((D_MODEL, GATE_W), lambda i, j: (0, 0)),
            pl.BlockSpec((1, GATE_W), lambda i, j: (0, 0)),
            pl.BlockSpec((ML_CHUNK, ML_CHUNK), lambda i, j: (0, 0)),
            pl.BlockSpec((1, tn), lambda i, j: (0, j)),
        ] + [w_spec(f, n) for f, n in w_blocks],
        out_specs=[
            pl.BlockSpec((tm, tn), lambda i, j: (i, j)),
            pl.BlockSpec((tm, GATE_W), lambda i, j: (i, 0)),
            pl.BlockSpec((N_GATES, tm), lambda i, j: (0, i)),
        ],
        out_shape=[
            jax.ShapeDtypeStruct((t, PROJ_W), bf16),
            jax.ShapeDtypeStruct((t, GATE_W), f32),
            jax.ShapeDtypeStruct((N_GATES, t), f32),
        ],
        scratch_shapes=[pltpu.VMEM((tm, D_MODEL), bf16), pltpu.VMEM((1, GATE_W), f32)],
        compiler_params=pltpu.CompilerParams(
            dimension_semantics=("arbitrary", "arbitrary"), vmem_limit_bytes=VMEM_LIMIT),
        name="inproj",
    )(x2, nw, w_gate, b_gate, tri, _proj_col_scale(), *w_pieces)


def _mlstm_kernel(q_ref, k_ref, v_ref, o_ref, gcol_ref, grow_ref, hnw_ref, spread_ref, out_ref, s_sc, m_sc):
    c = pl.program_id(1)
    L = ML_CHUNK
    nb = q_ref.shape[0]
    assert L == LANES

    @pl.when(c == 0)
    def _():
        s_sc[...] = jnp.zeros_like(s_sc)
        m_sc[...] = jnp.zeros_like(m_sc)

    row = lax.broadcasted_iota(jnp.int32, (L, L), 0)
    col = lax.broadcasted_iota(jnp.int32, (L, L), 1)
    causal = col <= row
    ones_blk = jnp.ones((L, LANES), bf16)

    def wide(tile, n):
        return jnp.concatenate([tile] * n, axis=1)

    P = [(n, h) for n in range(nb) for h in range(ML_HEADS)]
    C = range(len(P))
    qk_cols = lambda h: slice(h * ML_DQK, (h + 1) * ML_DQK)
    v_cols = lambda h: slice(h * ML_DV, (h + 1) * ML_DV)
    spread = []
    for n in range(nb):
        g = gcol_ref[n]
        g_hi = g.astype(bf16)
        r1 = g - g_hi.astype(f32)
        g_mid = r1.astype(bf16)
        g_lo = (r1 - g_mid.astype(f32)).astype(bf16)
        spread.append(_dot(jnp.concatenate([g_hi, g_mid, g_lo], axis=1), spread_ref[...]))
    tile = lambda n, j: spread[n][:, j * LANES:(j + 1) * LANES]
    b_t = [tile(n, h) for n, h in P]
    a_t = [tile(n, ML_HEADS + h) for n, h in P]
    gr = [grow_ref[n] for n in range(nb)]
    a_r = [gr[n][h:h + 1, :] - gr[n][ML_HEADS + h:ML_HEADS + h + 1, :] for n, h in P]
    q = [q_ref[n, :, qk_cols(h)] for n, h in P]
    k = [k_ref[n, :, qk_cols(h)] for n, h in P]
    v_aug = [jnp.concatenate([v_ref[n, :, v_cols(h)], ones_blk], axis=1) for n, h in P]
    m_prev = [m_sc[i:i + 1, :] for i in C]
    state = [s_sc[i] for i in C]

    qk = [_dot_nt(q[i], k[i]) for i in C]
    amat = [jnp.where(causal, a_r[i], -jnp.inf) for i in C]
    mx = [jnp.maximum(m_prev[i], jnp.broadcast_to(jnp.max(amat[i], axis=-1, keepdims=True), (L, LANES)))
          for i in C]

    mx_last = [mx[i][L - 1:L, :] for i in C]
    b_last = [b_t[i][L - 1:L, :] for i in C]
    wk = [jnp.exp(a_t[i] - mx_last[i]) for i in C]
    upd = [_dot_tn((wk[i] * k[i].astype(f32)).astype(bf16), v_aug[i]) for i in C]

    s = [(qk[i] * jnp.exp(amat[i] - mx[i])).astype(bf16) for i in C]
    q_in = [(jnp.exp(m_prev[i] - mx[i]) * q[i].astype(f32)).astype(bf16) for i in C]
    num = [_dot(jnp.concatenate([s[i], q_in[i]], axis=1),
                jnp.concatenate([v_aug[i], state[i].astype(bf16)], axis=0)) for i in C]
    for i in C:
        s_sc[i] = wide(jnp.exp(m_prev[i] - mx_last[i]), 3) * state[i] + upd[i]
        m_sc[i:i + 1, :] = b_last[i] + mx_last[i]

    inv = [1.0 / jnp.maximum(jnp.abs(num[i][:, ML_DV:]), jnp.exp(-b_t[i] - mx[i])) for i in C]
    msq = [jnp.broadcast_to(jnp.mean(num[i][:, :ML_DV] * num[i][:, :ML_DV], axis=-1, keepdims=True),
                            (L, LANES)) for i in C]
    fac = [inv[i] * lax.rsqrt(inv[i] * inv[i] * msq[i] + EPS) for i in C]
    for i, (n, h) in enumerate(P):
        gate = jax.nn.sigmoid(o_ref[n, :, v_cols(h)].astype(f32)) * hnw_ref[:, v_cols(h)]
        out_ref[n, :, v_cols(h)] = (num[i][:, :ML_DV] * wide(fac[i], ML_DV // LANES) * gate).astype(bf16)


def _mlstm(proj, gcol, grow, hnw, *, batch, seq, nb=4):
    while batch % nb:
        nb //= 2
    nc = seq // ML_CHUNK
    L = ML_CHUNK
    proj4 = proj.reshape(batch // nb, nb, seq, PROJ_W)
    gcol4 = gcol.reshape(batch // nb, nb, seq, GATE_W)
    grow4 = grow.reshape(N_GATES, batch // nb, nb, seq).transpose(1, 2, 0, 3)
    src = lax.broadcasted_iota(jnp.int32, (GATE_W, 2 * ML_HEADS * LANES), 0)
    dst = lax.broadcasted_iota(jnp.int32, (GATE_W, 2 * ML_HEADS * LANES), 1) // LANES
    spread = (jnp.where(src == ML_HEADS + dst % ML_HEADS, jnp.where(dst < ML_HEADS, 1.0, -1.0), 0.0)
              + jnp.where((dst >= ML_HEADS) & (src == dst - ML_HEADS), 1.0, 0.0)).astype(bf16)
    spread = jnp.concatenate([spread] * 3, axis=0)
    out = pl.pallas_call(
        _mlstm_kernel,
        grid=(batch // nb, nc),
        in_specs=[
            pl.BlockSpec((None, nb, L, ML_QK), lambda b, c: (b, 0, c, COL_QM // ML_QK)),
            pl.BlockSpec((None, nb, L, ML_QK), lambda b, c: (b, 0, c, COL_KM // ML_QK)),
            pl.BlockSpec((None, nb, L, ML_V), lambda b, c: (b, 0, c, COL_VM // ML_V)),
            pl.BlockSpec((None, nb, L, ML_V), lambda b, c: (b, 0, c, COL_OM // ML_V)),
            pl.BlockSpec((None, nb, L, GATE_W), lambda b, c: (b, 0, c, 0)),
            pl.BlockSpec((None, nb, N_GATES, L), lambda b, c: (b, 0, 0, c)),
            pl.BlockSpec((1, ML_V), lambda b, c: (0, 0)),
            pl.BlockSpec((3 * GATE_W, 2 * ML_HEADS * LANES), lambda b, c: (0, 0)),
        ],
        out_specs=pl.BlockSpec((None, nb, L, ML_V), lambda b, c: (b, 0, c, 0)),
        out_shape=jax.ShapeDtypeStruct((batch // nb, nb, seq, ML_V), bf16),
        scratch_shapes=[pltpu.VMEM((nb * ML_HEADS, ML_DQK, ML_DV + LANES), f32),
                        pltpu.VMEM((nb * ML_HEADS, LANES), f32)],
        compiler_params=pltpu.CompilerParams(
            dimension_semantics=("arbitrary", "arbitrary"), vmem_limit_bytes=VMEM_LIMIT),
        name="mlstm",
    )(proj4, proj4, proj4, proj4, gcol4, grow4, hnw, spread)
    return out.reshape(batch * seq, ML_V)


def _fox_kernel(q_ref, k_ref, v_ref, gcol_ref, out_ref, kaug_sc, st_sc, mx_sc, m_sc, l_sc, acc_sc,
                *, blk, setup_rows, ahead):
    hg = pl.program_id(1)
    seq = k_ref.shape[0]
    nh = kaug_sc.shape[0]
    head_cols = [slice(i * FOX_DH, (i + 1) * FOX_DH) for i in range(nh)]

    r = lax.broadcasted_iota(jnp.int32, (GATE_W, nh * LANES), 0)
    c = lax.broadcasted_iota(jnp.int32, (GATE_W, nh * LANES), 1)
    src = 2 * ML_HEADS + hg * nh + c // LANES
    pick = jnp.concatenate([((r == src) & (c % LANES == j)).astype(bf16) for j in range(3)], axis=0)
    for n in range(seq // setup_rows):
        rows = pl.ds(n * setup_rows, setup_rows)
        a = gcol_ref[rows, :] * (-LOG2E)
        a_hi = a.astype(bf16)
        r1 = a - a_hi.astype(f32)
        a_mid = r1.astype(bf16)
        a_lo = (r1 - a_mid.astype(f32)).astype(bf16)
        aug = _dot(jnp.concatenate([a_hi, a_mid, a_lo], axis=1), pick).astype(bf16)
        for i in range(nh):
            kaug_sc[i, rows, :FOX_DH] = k_ref[rows, head_cols[i]]
            kaug_sc[i, rows, FOX_DH:] = aug[:, i * LANES:(i + 1) * LANES]

    ones3 = (lax.broadcasted_iota(jnp.int32, (blk, LANES), 1) < 3).astype(bf16)

    def query_block(qi, _):
        qrows = pl.ds(pl.multiple_of(qi * blk, blk), blk)
        q_aug = [jnp.concatenate([q_ref[qrows, head_cols[i]], ones3], axis=1) for i in range(nh)]

        m_sc[...] = jnp.full(m_sc.shape, -jnp.inf, f32)
        l_sc[...] = jnp.zeros(l_sc.shape, f32)
        acc_sc[...] = jnp.zeros(acc_sc.shape, f32)

        def logits_stage(i, j, masked):
            start = pl.multiple_of(j * blk, blk)
            st = _dot_nt(kaug_sc[i, pl.ds(start, blk), :], q_aug[i])
            if masked:
                row = lax.broadcasted_iota(jnp.int32, (blk, blk), 0)
                col = lax.broadcasted_iota(jnp.int32, (blk, blk), 1)
                st = jnp.where(row <= col, st, -jnp.inf)
            st_sc[i] = st
            mx_sc[i] = jnp.max(st, axis=0, keepdims=True)

        def softmax_stage(i, j):
            start = pl.multiple_of(j * blk, blk)
            m = m_sc[i]
            m_new = jnp.maximum(m, mx_sc[i])
            alpha = jnp.exp2(m - m_new)
            p = jnp.exp2(st_sc[i] - m_new)
            l_sc[i] = alpha * l_sc[i] + jnp.sum(p, axis=0, keepdims=True)
            acc_sc[i] = alpha * acc_sc[i] + _dot_tn(v_ref[pl.ds(start, blk), head_cols[i]], p.astype(bf16))
            m_sc[i] = m_new

        def round_robin(cur, nxt, masked):
            for i in range(nh):
                if i + ahead < nh:
                    logits_stage(i + ahead, cur, masked)
                else:
                    logits_stage(i + ahead - nh, nxt, False)
                softmax_stage(i, cur)

        for i in range(ahead):
            logits_stage(i, qi, True)
        round_robin(qi, 0, True)

        def body(k, _):
            round_robin(k - 1, jnp.minimum(k, qi - 1), False)
            return 0

        lax.fori_loop(1, qi + 1, body, 0)

        for i in range(nh):
            out_ref[qrows, head_cols[i]] = (acc_sc[i] * (1.0 / l_sc[i])).T.astype(bf16)
        return 0

    lax.fori_loop(0, seq // blk, query_block, 0)


def _fox(proj, gcol, *, batch, seq, blk=512, nh=4, ahead=1):
    assert 1 <= ahead < nh
    assert seq % blk == 0 and FOX_HEADS % nh == 0
    t = proj.shape[0]
    w = nh * FOX_DH
    kern = functools.partial(_fox_kernel, blk=blk, setup_rows=blk, ahead=ahead)
    return pl.pallas_call(
        kern,
        grid=(batch, FOX_HEADS // nh),
        in_specs=[
            pl.BlockSpec((seq, w), lambda b, h: (b, COL_QF // w + h)),
            pl.BlockSpec((seq, w), lambda b, h: (b, COL_KF // w + h)),
            pl.BlockSpec((seq, w), lambda b, h: (b, COL_VF // w + h)),
            pl.BlockSpec((seq, GATE_W), lambda b, h: (b, 0)),
        ],
        out_specs=pl.BlockSpec((seq, w), lambda b, h: (b, h)),
        out_shape=jax.ShapeDtypeStruct((t, FOX_W), bf16),
        scratch_shapes=[pltpu.VMEM((nh, seq, FOX_DH + LANES), bf16),
                        pltpu.VMEM((nh, blk, blk), f32),
                        pltpu.VMEM((nh, 1, blk), f32),
                        pltpu.VMEM((nh, 1, blk), f32),
                        pltpu.VMEM((nh, 1, blk), f32),
                        pltpu.VMEM((nh, FOX_DH, blk), f32)],
        compiler_params=pltpu.CompilerParams(
            dimension_semantics=("arbitrary", "arbitrary"), vmem_limit_bytes=VMEM_LIMIT),
        name="fox",
    )(proj, proj, proj, gcol)


def _merge_kernel(x_ref, ha_ref, hb_ref, ga_ref, gb_ref, wa_ref, wb_ref, wo_ref,
                  bga_ref, bgb_ref, npost_ref, npre_ref, x1_ref, h2_ref, *, sub):
    n_sub = x_ref.shape[0] // sub
    rows = [slice(r * sub, (r + 1) * sub) for r in range(n_sub)]

    def branches(r):
        return _dot(ha_ref[rows[r], :], wa_ref[...]), _dot(hb_ref[rows[r], :], wb_ref[...])

    def finish(r, z):
        x1 = x_ref[rows[r], :] + _rms(z) * npost_ref[...]
        x1_ref[rows[r], :] = x1
        h2_ref[rows[r], :] = (_rms(x1) * npre_ref[...]).astype(bf16)

    y_next = branches(0)
    z_prev = None
    for r in range(n_sub):
        ya, yb = y_next
        if r + 1 < n_sub:
            y_next = branches(r + 1)
        merged = (jax.nn.sigmoid(ga_ref[rows[r], :].astype(f32) + bga_ref[...]) * ya
                  + jax.nn.sigmoid(gb_ref[rows[r], :].astype(f32) + bgb_ref[...]) * yb)
        z = _dot(merged.astype(bf16), wo_ref[...])
        if z_prev is not None:
            finish(r - 1, z_prev)
        z_prev = z
    finish(n_sub - 1, z_prev)


def _merge(x2, h_a, h_b, proj, wa, wb, wo, bga, bgb, npost, npre, *, tm=1024, sub=512):
    t = x2.shape[0]
    tm = min(tm, t)
    tok = lambda i: (i, 0)
    const = lambda i: (0, 0)
    wspec = pl.BlockSpec((D_MODEL, D_MODEL), const, pipeline_mode=pl.Buffered(1))
    vspec = pl.BlockSpec((1, D_MODEL), const)
    return pl.pallas_call(
        functools.partial(_merge_kernel, sub=sub),
        grid=(t // tm,),
        in_specs=[
            pl.BlockSpec((tm, D_MODEL), tok),
            pl.BlockSpec((tm, ML_V), tok),
            pl.BlockSpec((tm, FOX_W), tok),
            pl.BlockSpec((tm, D_MODEL), lambda i: (i, COL_GA // D_MODEL)),
            pl.BlockSpec((tm, D_MODEL), lambda i: (i, COL_GB // D_MODEL)),
            wspec, wspec, wspec, vspec, vspec, vspec, vspec,
        ],
        out_specs=[pl.BlockSpec((tm, D_MODEL), tok), pl.BlockSpec((tm, D_MODEL), tok)],
        out_shape=[jax.ShapeDtypeStruct((t, D_MODEL), f32), jax.ShapeDtypeStruct((t, D_MODEL), bf16)],
        compiler_params=pltpu.CompilerParams(
            dimension_semantics=("arbitrary",), vmem_limit_bytes=VMEM_LIMIT),
        name="merge",
    )(x2, h_a, h_b, proj, proj, wa, wb, wo, bga, bgb, npost, npre)


def _ffn_kernel(x1_ref, h2_ref, wup_ref, cw_ref, cb_ref, wdn_ref, npost_ref, out_ref,
                halo_sc, ubuf_sc, act_sc, acc_sc, *, blocks_per_seq, tf, down_group):
    i = pl.program_id(0)
    tm = x1_ref.shape[0]

    @pl.when(i % blocks_per_seq == 0)
    def _():
        halo_sc[...] = jnp.zeros_like(halo_sc)

    h2 = h2_ref[...]
    n_chunks = D_FF // tf

    def up(c):
        for half in range(2):
            col0 = half * D_FF + c * tf
            slot = 2 * (c % 2) + half
            u = _dot(h2, wup_ref[:, col0:col0 + tf])
            ubuf_sc[slot, 0:SUBLANES, :] = halo_sc[:, col0:col0 + tf]
            ubuf_sc[slot, SUBLANES:SUBLANES + tm, :] = u
            halo_sc[:, col0:col0 + tf] = u[tm - SUBLANES:tm, :]

    def conv(c, half):
        col0 = half * D_FF + c * tf
        slot = 2 * (c % 2) + half
        w = cw_ref[:, col0:col0 + tf]
        y = cb_ref[:, col0:col0 + tf]
        for j in range(CONV_W):
            first = SUBLANES - (CONV_W - 1 - j)
            y = y + w[j:j + 1] * ubuf_sc[slot, first:first + tm, :]
        return y

    def gate(g, a):
        inner = g * (GELU_K1 + (GELU_K1 * GELU_CUBIC) * (g * g))
        return (g * a) * (1.0 / (1.0 + jnp.exp2(inner)))

    def down(c0, c1):
        return _dot(act_sc[:, c0 * tf:c1 * tf], wdn_ref[c0 * tf:c1 * tf, :])

    up(0)
    pending, done = None, 0
    for c in range(n_chunks):
        if c + 1 < n_chunks:
            up(c + 1)
        if pending is not None:
            if pending[0] == 0:
                acc_sc[...] = down(*pending)
            else:
                acc_sc[...] += down(*pending)
            pending = None
        act_sc[:, c * tf:(c + 1) * tf] = gate(conv(c, 1), conv(c, 0)).astype(bf16)
        if (c + 1) % down_group == 0 or c + 1 == n_chunks:
            pending, done = (done, c + 1), c + 1

    y = down(*pending)
    if pending[0] != 0:
        y = y + acc_sc[...]
    out_ref[...] = x1_ref[...] + _rms(y) * npost_ref[...]


def _ffn(x1, h2, wup, cw, cb, wdn, npost, *, seq, tm=512, tf=256, down_group=2):
    t = x1.shape[0]
    tok = lambda i: (i, 0)
    const = lambda i: (0, 0)
    kern = functools.partial(_ffn_kernel, blocks_per_seq=seq // tm, tf=tf, down_group=down_group)
    return pl.pallas_call(
        kern,
        grid=(t // tm,),
        in_specs=[
            pl.BlockSpec((tm, D_MODEL), tok),
            pl.BlockSpec((tm, D_MODEL), tok),
            pl.BlockSpec((D_MODEL, 2 * D_FF), const, pipeline_mode=pl.Buffered(1)),
            pl.BlockSpec((CONV_W, 2 * D_FF), const),
            pl.BlockSpec((1, 2 * D_FF), const),
            pl.BlockSpec((D_FF, D_MODEL), const, pipeline_mode=pl.Buffered(1)),
            pl.BlockSpec((1, D_MODEL), const),
        ],
        out_specs=pl.BlockSpec((tm, D_MODEL), tok),
        out_shape=jax.ShapeDtypeStruct((t, D_MODEL), f32),
        scratch_shapes=[pltpu.VMEM((SUBLANES, 2 * D_FF), f32),
                        pltpu.VMEM((4, tm + SUBLANES, tf), f32),
                        pltpu.VMEM((tm, D_FF), bf16),
                        pltpu.VMEM((tm, D_MODEL), f32)],
        compiler_params=pltpu.CompilerParams(
            dimension_semantics=("arbitrary",), vmem_limit_bytes=VMEM_LIMIT),
        name="ffn",
    )(x1, h2, wup, cw, cb, wdn, npost)


def _layer(x2, p, l, *, batch, seq):
    w_in = p["w_in"][l]
    o_i = 2 * ML_QK + ML_V
    o_o = o_i + 2 * ML_HEADS
    o_qf = o_o + ML_V
    o_ff = o_qf + 3 * FOX_W
    o_ga = o_ff + FOX_HEADS
    w_pieces = [
        w_in[:, :o_i].astype(bf16),
        w_in[:, o_o:o_ff].astype(bf16),
        w_in[:, o_ga:].astype(bf16),
    ]
    w_gate = jnp.concatenate([
        w_in[:, o_i:o_o], w_in[:, o_ff:o_ga],
        jnp.zeros((D_MODEL, GATE_W - N_GATES), f32)], axis=1).astype(bf16)
    b_gate = jnp.concatenate([
        p["b_ml_i"][l], p["b_ml_f"][l], p["b_fox_f"][l], jnp.zeros((GATE_W - N_GATES,), f32)])[None, :]
    tri = jnp.tril(jnp.ones((ML_CHUNK, ML_CHUNK), f32)).astype(bf16)
    row = lambda v: v[None, :].astype(f32)

    proj, gcol, grow = _inproj(x2, row(p["norm_mix_pre"][l]), w_pieces, w_gate, b_gate, tri, seq=seq)
    h_a = _mlstm(proj, gcol, grow, row(p["ml_head_norm"][l]), batch=batch, seq=seq)
    h_b = _fox(proj, gcol, batch=batch, seq=seq)
    x1, h2 = _merge(x2, h_a, h_b, proj,
                    p["w_branch_a"][l].astype(bf16), p["w_branch_b"][l].astype(bf16),
                    p["w_out"][l].astype(bf16), row(p["b_gate_a"][l]), row(p["b_gate_b"][l]),
                    row(p["norm_mix_post"][l]), row(p["norm_ffn_pre"][l]))
    return _ffn(x1, h2, p["w_up"][l].astype(bf16), p["conv_w"][l], row(p["conv_b"][l]),
                p["w_down"][l].astype(bf16), row(p["norm_ffn_post"][l]), seq=seq)


def kernel(x, norm_mix_pre, w_in, b_ml_i, b_ml_f, ml_head_norm, b_fox_f, b_gate_a, b_gate_b,
           w_branch_a, w_branch_b, w_out, norm_mix_post, norm_ffn_pre, w_up, conv_w, conv_b,
           w_down, norm_ffn_post):
    batch, seq, _ = x.shape
    p = dict(norm_mix_pre=norm_mix_pre, w_in=w_in, b_ml_i=b_ml_i, b_ml_f=b_ml_f,
             ml_head_norm=ml_head_norm, b_fox_f=b_fox_f, b_gate_a=b_gate_a, b_gate_b=b_gate_b,
             w_branch_a=w_branch_a, w_branch_b=w_branch_b, w_out=w_out,
             norm_mix_post=norm_mix_post, norm_ffn_pre=norm_ffn_pre, w_up=w_up, conv_w=conv_w,
             conv_b=conv_b, w_down=w_down, norm_ffn_post=norm_ffn_post)
    x2 = x.reshape(batch * seq, D_MODEL)
    for l in range(w_in.shape[0]):
        x2 = _layer(x2, p, l, batch=batch, seq=seq)
    return x2.reshape(batch, seq, D_MODEL)
```

```python
import functools

import jax
import jax.numpy as jnp
from jax import lax
from jax.experimental import pallas as pl
from jax.experimental.pallas import tpu as pltpu

D_MODEL = 1024
ML_HEADS = 4
ML_DQK = 128
ML_DV = 256
ML_QK = ML_HEADS * ML_DQK
ML_V = ML_HEADS * ML_DV
FOX_HEADS = 8
FOX_DH = 128
FOX_W = FOX_HEADS * FOX_DH
D_FF = 2816
CONV_W = 3
GATE_CAP = 15.0
EPS = 1e-6
LOG2E = 1.4426950408889634
GELU_CUBIC = 0.044715
GELU_K1 = -2.0 * (2.0 / 3.141592653589793) ** 0.5 * LOG2E

LANES = 128
SUBLANES = 8
GATE_W = LANES
N_GATES = 2 * ML_HEADS + FOX_HEADS
ML_CHUNK = 128
VMEM_LIMIT = 56 * 1024 * 1024

PROJ_W = 2 * ML_QK + 2 * ML_V + 3 * FOX_W + 2 * D_MODEL
COL_QM, COL_KM, COL_VM, COL_OM = 0, ML_QK, 2 * ML_QK, 2 * ML_QK + ML_V
COL_QF = COL_OM + ML_V
COL_KF = COL_QF + FOX_W
COL_VF = COL_KF + FOX_W
COL_GA = COL_VF + FOX_W
COL_GB = COL_GA + D_MODEL

f32 = jnp.float32
bf16 = jnp.bfloat16


def _log_sigmoid(z):
    return jnp.minimum(z, 0.0) - jnp.log1p(jnp.exp(-jnp.abs(z)))


def _rms(v):
    return v * lax.rsqrt(jnp.mean(v * v, axis=-1, keepdims=True) + EPS)


def _dot(a, b):
    return jnp.dot(a, b, preferred_element_type=f32)


def _dot_nt(a, b):
    return lax.dot_general(a, b, (((1,), (1,)), ((), ())), preferred_element_type=f32)


def _dot_tn(a, b):
    return lax.dot_general(a, b, (((0,), (0,)), ((), ())), preferred_element_type=f32)


def _inproj_kernel(*refs, blocks_per_seq, piece, w_blocks):
    x_ref, nw_ref, wg_ref, bg_ref, tri_ref, cs_ref = refs[:6]
    w_refs = refs[6:6 + len(w_blocks)]
    proj_ref, gcol_ref, grow_ref, h_sc, carry_sc = refs[6 + len(w_blocks):]
    assert w_blocks[0][0] == 0
    i = pl.program_id(0)
    j = pl.program_id(1)
    tm = x_ref.shape[0]

    def project(hb, w_ref=w_refs[0]):
        return (_dot(hb, w_ref[...]) * cs_ref[...]).astype(bf16)

    @pl.when(j == 0)
    def _():
        @pl.when(i % blocks_per_seq == 0)
        def _():
            carry_sc[...] = jnp.zeros_like(carry_sc)

        tri = tri_ref[...]
        lane = lax.broadcasted_iota(jnp.int32, (ML_CHUNK, GATE_W), 1)
        per_piece = piece // ML_CHUNK

        def normed(p):
            prow = slice(p * piece, (p + 1) * piece)
            hb = (_rms(x_ref[prow, :]) * nw_ref[...]).astype(bf16)
            h_sc[prow, :] = hb
            return hb

        acts = []
        hb = normed(0)
        for p in range(tm // piece):
            proj_ref[p * piece:(p + 1) * piece, :] = project(hb)
            g_piece = _dot(hb, wg_ref[...]) + bg_ref[...]
            if (p + 1) * piece < tm:
                hb = normed(p + 1)
            for q in range(per_piece):
                g = g_piece[q * ML_CHUNK:(q + 1) * ML_CHUNK]
                cap = GATE_CAP * jnp.tanh(g / GATE_CAP)
                a = jnp.where(lane < ML_HEADS, cap,
                              jnp.where(lane < 2 * ML_HEADS, _log_sigmoid(cap), _log_sigmoid(g)))
                acts.append(jnp.where(lane < N_GATES, a, 0.0))

        for r in range(tm // ML_CHUNK):
            rows = slice(r * ML_CHUNK, (r + 1) * ML_CHUNK)
            a = acts[r]
            a_hi = a.astype(bf16)
            r1 = a - a_hi.astype(f32)
            a_mid = r1.astype(bf16)
            a_lo = (r1 - a_mid.astype(f32)).astype(bf16)
            cs3 = _dot(tri, jnp.concatenate([a_hi, a_mid, a_lo], axis=1))
            cs = cs3[:, :GATE_W] + cs3[:, GATE_W:2 * GATE_W] + cs3[:, 2 * GATE_W:]
            glob = cs + carry_sc[...]
            carry_sc[...] = glob[ML_CHUNK - 1:ML_CHUNK, :]
            out = jnp.where(lane < ML_HEADS, a, jnp.where(lane < 2 * ML_HEADS, cs, glob))
            gcol_ref[rows, :] = out
            grow_ref[:, rows] = out.T[:N_GATES, :]

    for w_ref, (first, count) in zip(w_refs, w_blocks):
        @pl.when((j >= max(first, 1)) & (j < first + count))
        def _(w_ref=w_ref):
            proj_ref[...] = project(h_sc[...], w_ref)


def _proj_col_scale():
    cs = jnp.ones((1, PROJ_W), f32)
    cs = cs.at[:, COL_QM:COL_QM + ML_QK].set(ML_DQK ** -0.5)
    return cs.at[:, COL_QF:COL_QF + FOX_W].set(LOG2E * FOX_DH ** -0.5)


def _inproj(x2, nw, w_pieces, w_gate, b_gate, tri, *, seq, tm=2048, tn=1024, piece=512):
    t = x2.shape[0]
    tm = min(tm, seq)
    assert seq % tm == 0 and tm % piece == 0
    piece = max(piece, ML_CHUNK)
    w_blocks, first = [], 0
    for w in w_pieces:
        w_blocks.append((first, w.shape[1] // tn))
        first += w.shape[1] // tn
    assert first == PROJ_W // tn
    kern = functools.partial(_inproj_kernel, blocks_per_seq=seq // tm, piece=piece, w_blocks=tuple(w_blocks))

    def w_spec(first, count):
        return pl.BlockSpec((D_MODEL, tn), lambda i, j: (
            0, jnp.where(j < first - 1, count - 1, jnp.clip(j - first, 0, count - 1))))

    return pl.pallas_call(
        kern,
        grid=(t // tm, PROJ_W // tn),
        in_specs=[
            pl.BlockSpec((tm, D_MODEL), lambda i, j: (i, 0)),
            pl.BlockSpec((1, D_MODEL), lambda i, j: (0, 0)),
            pl.BlockSpec((D_MODEL, GATE_W), lambda i, j: (0, 0)),
            pl.BlockSpec((1, GATE_W), lambda i, j: (0, 0)),
            pl.BlockSpec((ML_CHUNK, ML_CHUNK), lambda i, j: (0, 0)),
            pl.BlockSpec((1, tn), lambda i, j: (0, j)),
        ] + [w_spec(f, n) for f, n in w_blocks],
        out_specs=[
            pl.BlockSpec((tm, tn), lambda i, j: (i, j)),
            pl.BlockSpec((tm, GATE_W), lambda i, j: (i, 0)),
            pl.BlockSpec((N_GATES, tm), lambda i, j: (0, i)),
        ],
        out_shape=[
            jax.ShapeDtypeStruct((t, PROJ_W), bf16),
            jax.ShapeDtypeStruct((t, GATE_W), f32),
            jax.ShapeDtypeStruct((N_GATES, t), f32),
        ],
        scratch_shapes=[pltpu.VMEM((tm, D_MODEL), bf16), pltpu.VMEM((1, GATE_W), f32)],
        compiler_params=pltpu.CompilerParams(
            dimension_semantics=("arbitrary", "arbitrary"), vmem_limit_bytes=VMEM_LIMIT),
        name="inproj",
    )(x2, nw, w_gate, b_gate, tri, _proj_col_scale(), *w_pieces)


def _mlstm_kernel(qkvo_ref, gcol_ref, grow_ref, hnw_ref, spread_ref, out_ref, s_sc, m_sc):
    c = pl.program_id(1)
    L = ML_CHUNK
    nb = qkvo_ref.shape[0]
    assert L == LANES

    @pl.when(c == 0)
    def _():
        s_sc[...] = jnp.zeros_like(s_sc)
        m_sc[...] = jnp.zeros_like(m_sc)

    row = lax.broadcasted_iota(jnp.int32, (L, L), 0)
    col = lax.broadcasted_iota(jnp.int32, (L, L), 1)
    causal = col <= row
    ones_blk = jnp.ones((L, LANES), bf16)

    def wide(tile, n):
        return jnp.concatenate([tile] * n, axis=1)

    P = [(n, h) for n in range(nb) for h in range(ML_HEADS)]
    C = range(len(P))
    qk_cols = lambda h, base: slice(base + h * ML_DQK, base + (h + 1) * ML_DQK)
    v_cols = lambda h, base=0: slice(base + h * ML_DV, base + (h + 1) * ML_DV)
    spread = []
    for n in range(nb):
        g = gcol_ref[n]
        g_hi = g.astype(bf16)
        r1 = g - g_hi.astype(f32)
        g_mid = r1.astype(bf16)
        g_lo = (r1 - g_mid.astype(f32)).astype(bf16)
        spread.append(_dot(jnp.concatenate([g_hi, g_mid, g_lo], axis=1), spread_ref[...]))
    tile = lambda n, j: spread[n][:, j * LANES:(j + 1) * LANES]
    b_t = [tile(n, h) for n, h in P]
    a_t = [tile(n, ML_HEADS + h) for n, h in P]
    gr = [grow_ref[n] for n in range(nb)]
    a_r = [gr[n][h:h + 1, :] - gr[n][ML_HEADS + h:ML_HEADS + h + 1, :] for n, h in P]
    q = [qkvo_ref[n, :, qk_cols(h, COL_QM)] for n, h in P]
    k = [qkvo_ref[n, :, qk_cols(h, COL_KM)] for n, h in P]
    v_aug = [jnp.concatenate([qkvo_ref[n, :, v_cols(h, COL_VM)], ones_blk], axis=1) for n, h in P]
    m_prev = [m_sc[i:i + 1, :] for i in C]
    state = [s_sc[i] for i in C]

    qk = [_dot_nt(q[i], k[i]) for i in C]
    amat = [jnp.where(causal, a_r[i], -jnp.inf) for i in C]
    mx = [jnp.maximum(m_prev[i], jnp.broadcast_to(jnp.max(amat[i], axis=-1, keepdims=True), (L, LANES)))
          for i in C]

    mx_last = [mx[i][L - 1:L, :] for i in C]
    b_last = [b_t[i][L - 1:L, :] for i in C]
    wk = [jnp.exp(a_t[i] - mx_last[i]) for i in C]
    upd = [_dot_tn((wk[i] * k[i].astype(f32)).astype(bf16), v_aug[i]) for i in C]

    s = [(qk[i] * jnp.exp(amat[i] - mx[i])).astype(bf16) for i in C]
    q_in = [(jnp.exp(m_prev[i] - mx[i]) * q[i].astype(f32)).astype(bf16) for i in C]
    num = [_dot(jnp.concatenate([s[i], q_in[i]], axis=1),
                jnp.concatenate([v_aug[i], state[i].astype(bf16)], axis=0)) for i in C]
    for i in C:
        s_sc[i] = wide(jnp.exp(m_prev[i] - mx_last[i]), 3) * state[i] + upd[i]
        m_sc[i:i + 1, :] = b_last[i] + mx_last[i]

    inv = [1.0 / jnp.maximum(jnp.abs(num[i][:, ML_DV:]), jnp.exp(-b_t[i] - mx[i])) for i in C]
    msq = [jnp.broadcast_to(jnp.mean(num[i][:, :ML_DV] * num[i][:, :ML_DV], axis=-1, keepdims=True),
                            (L, LANES)) for i in C]
    fac = [inv[i] * lax.rsqrt(inv[i] * inv[i] * msq[i] + EPS) for i in C]
    for i, (n, h) in enumerate(P):
        gate = jax.nn.sigmoid(qkvo_ref[n, :, v_cols(h, COL_OM)].astype(f32)) * hnw_ref[:, v_cols(h)]
        out_ref[n, :, v_cols(h)] = (num[i][:, :ML_DV] * wide(fac[i], ML_DV // LANES) * gate).astype(bf16)


def _mlstm(proj, gcol, grow, hnw, *, batch, seq, nb=4):
    while batch % nb:
        nb //= 2
    nc = seq // ML_CHUNK
    L = ML_CHUNK
    proj4 = proj.reshape(batch // nb, nb, seq, PROJ_W)
    gcol4 = gcol.reshape(batch // nb, nb, seq, GATE_W)
    grow4 = grow.reshape(N_GATES, batch // nb, nb, seq).transpose(1, 2, 0, 3)
    src = lax.broadcasted_iota(jnp.int32, (GATE_W, 2 * ML_HEADS * LANES), 0)
    dst = lax.broadcasted_iota(jnp.int32, (GATE_W, 2 * ML_HEADS * LANES), 1) // LANES
    spread = (jnp.where(src == ML_HEADS + dst % ML_HEADS, jnp.where(dst < ML_HEADS, 1.0, -1.0), 0.0)
              + jnp.where((dst >= ML_HEADS) & (src == dst - ML_HEADS), 1.0, 0.0)).astype(bf16)
    spread = jnp.concatenate([spread] * 3, axis=0)
    out = pl.pallas_call(
        _mlstm_kernel,
        grid=(batch // nb, nc),
        in_specs=[
            pl.BlockSpec((None, nb, L, COL_QF), lambda b, c: (b, 0, c, 0)),
            pl.BlockSpec((None, nb, L, GATE_W), lambda b, c: (b, 0, c, 0)),
            pl.BlockSpec((None, nb, N_GATES, L), lambda b, c: (b, 0, 0, c)),
            pl.BlockSpec((1, ML_V), lambda b, c: (0, 0)),
            pl.BlockSpec((3 * GATE_W, 2 * ML_HEADS * LANES), lambda b, c: (0, 0)),
        ],
        out_specs=pl.BlockSpec((None, nb, L, ML_V), lambda b, c: (b, 0, c, 0)),
        out_shape=jax.ShapeDtypeStruct((batch // nb, nb, seq, ML_V), bf16),
        scratch_shapes=[pltpu.VMEM((nb * ML_HEADS, ML_DQK, ML_DV + LANES), f32),
                        pltpu.VMEM((nb * ML_HEADS, LANES), f32)],
        compiler_params=pltpu.CompilerParams(
            dimension_semantics=("arbitrary", "arbitrary"), vmem_limit_bytes=VMEM_LIMIT),
        name="mlstm",
    )(proj4, gcol4, grow4, hnw, spread)
    return out.reshape(batch * seq, ML_V)


def _fox_kernel(q_ref, k_ref, v_ref, gcol_ref, out_ref, kaug_sc, st_sc, mx_sc, m_sc, l_sc, acc_sc,
                *, blk, setup_rows, ahead):
    hg = pl.program_id(1)
    seq = k_ref.shape[0]
    nh = kaug_sc.shape[0]
    head_cols = [slice(i * FOX_DH, (i + 1) * FOX_DH) for i in range(nh)]

    r = lax.broadcasted_iota(jnp.int32, (GATE_W, nh * LANES), 0)
    c = lax.broadcasted_iota(jnp.int32, (GATE_W, nh * LANES), 1)
    src = 2 * ML_HEADS + hg * nh + c // LANES
    pick = jnp.concatenate([((r == src) & (c % LANES == j)).astype(bf16) for j in range(3)], axis=0)
    for n in range(seq // setup_rows):
        rows = pl.ds(n * setup_rows, setup_rows)
        a = gcol_ref[rows, :] * (-LOG2E)
        a_hi = a.astype(bf16)
        r1 = a - a_hi.astype(f32)
        a_mid = r1.astype(bf16)
        a_lo = (r1 - a_mid.astype(f32)).astype(bf16)
        aug = _dot(jnp.concatenate([a_hi, a_mid, a_lo], axis=1), pick).astype(bf16)
        for i in range(nh):
            kaug_sc[i, rows, :FOX_DH] = k_ref[rows, head_cols[i]]
            kaug_sc[i, rows, FOX_DH:] = aug[:, i * LANES:(i + 1) * LANES]

    ones3 = (lax.broadcasted_iota(jnp.int32, (blk, LANES), 1) < 3).astype(bf16)

    def query_block(qi, _):
        qrows = pl.ds(pl.multiple_of(qi * blk, blk), blk)
        q_aug = [jnp.concatenate([q_ref[qrows, head_cols[i]], ones3], axis=1) for i in range(nh)]

        m_sc[...] = jnp.full(m_sc.shape, -jnp.inf, f32)
        l_sc[...] = jnp.zeros(l_sc.shape, f32)
        acc_sc[...] = jnp.zeros(acc_sc.shape, f32)

        def logits_stage(i, j, masked):
            start = pl.multiple_of(j * blk, blk)
            st = _dot_nt(kaug_sc[i, pl.ds(start, blk), :], q_aug[i])
            if masked:
                row = lax.broadcasted_iota(jnp.int32, (blk, blk), 0)
                col = lax.broadcasted_iota(jnp.int32, (blk, blk), 1)
                st = jnp.where(row <= col, st, -jnp.inf)
            st_sc[i] = st
            mx_sc[i] = jnp.max(st, axis=0, keepdims=True)

        def softmax_stage(i, j):
            start = pl.multiple_of(j * blk, blk)
            m = m_sc[i]
            m_new = jnp.maximum(m, mx_sc[i])
            alpha = jnp.exp2(m - m_new)
            p = jnp.exp2(st_sc[i] - m_new)
            l_sc[i] = alpha * l_sc[i] + jnp.sum(p, axis=0, keepdims=True)
            acc_sc[i] = alpha * acc_sc[i] + _dot_tn(v_ref[pl.ds(start, blk), head_cols[i]], p.astype(bf16))
            m_sc[i] = m_new

        def round_robin(cur, nxt, masked):
            for i in range(nh):
                if i + ahead < nh:
                    logits_stage(i + ahead, cur, masked)
                else:
                    logits_stage(i + ahead - nh, nxt, False)
                softmax_stage(i, cur)

        for i in range(ahead):
            logits_stage(i, qi, True)
        round_robin(qi, 0, True)

        def body(k, _):
            round_robin(k - 1, jnp.minimum(k, qi - 1), False)
            return 0

        lax.fori_loop(1, qi + 1, body, 0)

        for i in range(nh):
            out_ref[qrows, head_cols[i]] = (acc_sc[i] * (1.0 / l_sc[i])).T.astype(bf16)
        return 0

    lax.fori_loop(0, seq // blk, query_block, 0)


def _fox(proj, gcol, *, batch, seq, blk=512, nh=4, ahead=1):
    assert 1 <= ahead < nh
    assert seq % blk == 0 and FOX_HEADS % nh == 0
    t = proj.shape[0]
    w = nh * FOX_DH
    kern = functools.partial(_fox_kernel, blk=blk, setup_rows=blk, ahead=ahead)
    return pl.pallas_call(
        kern,
        grid=(batch, FOX_HEADS // nh),
        in_specs=[
            pl.BlockSpec((seq, w), lambda b, h: (b, COL_QF // w + h)),
            pl.BlockSpec((seq, w), lambda b, h: (b, COL_KF // w + h)),
            pl.BlockSpec((seq, w), lambda b, h: (b, COL_VF // w + h)),
            pl.BlockSpec((seq, GATE_W), lambda b, h: (b, 0)),
        ],
        out_specs=pl.BlockSpec((seq, w), lambda b, h: (b, h)),
        out_shape=jax.ShapeDtypeStruct((t, FOX_W), bf16),
        scratch_shapes=[pltpu.VMEM((nh, seq, FOX_DH + LANES), bf16),
                        pltpu.VMEM((nh, blk, blk), f32),
                        pltpu.VMEM((nh, 1, blk), f32),
                        pltpu.VMEM((nh, 1, blk), f32),
                        pltpu.VMEM((nh, 1, blk), f32),
                        pltpu.VMEM((nh, FOX_DH, blk), f32)],
        compiler_params=pltpu.CompilerParams(
            dimension_semantics=("arbitrary", "arbitrary"), vmem_limit_bytes=VMEM_LIMIT),
        name="fox",
    )(proj, proj, proj, gcol)


def _merge_kernel(x_ref, ha_ref, hb_ref, ga_ref, gb_ref, wa_ref, wb_ref, wo_ref,
                  bga_ref, bgb_ref, npost_ref, npre_ref, x1_ref, h2_ref, *, sub):
    n_sub = x_ref.shape[0] // sub
    rows = [slice(r * sub, (r + 1) * sub) for r in range(n_sub)]

    def branches(r):
        return _dot(ha_ref[rows[r], :], wa_ref[...]), _dot(hb_ref[rows[r], :], wb_ref[...])

    def finish(r, z):
        x1 = x_ref[rows[r], :] + _rms(z) * npost_ref[...]
        x1_ref[rows[r], :] = x1
        h2_ref[rows[r], :] = (_rms(x1) * npre_ref[...]).astype(bf16)

    y_next = branches(0)
    z_prev = None
    for r in range(n_sub):
        ya, yb = y_next
        if r + 1 < n_sub:
            y_next = branches(r + 1)
        merged = (jax.nn.sigmoid(ga_ref[rows[r], :].astype(f32) + bga_ref[...]) * ya
                  + jax.nn.sigmoid(gb_ref[rows[r], :].astype(f32) + bgb_ref[...]) * yb)
        z = _dot(merged.astype(bf16), wo_ref[...])
        if z_prev is not None:
            finish(r - 1, z_prev)
        z_prev = z
    finish(n_sub - 1, z_prev)


def _merge(x2, h_a, h_b, proj, wa, wb, wo, bga, bgb, npost, npre, *, tm=1024, sub=512):
    t = x2.shape[0]
    tm = min(tm, t)
    tok = lambda i: (i, 0)
    const = lambda i: (0, 0)
    wspec = pl.BlockSpec((D_MODEL, D_MODEL), const, pipeline_mode=pl.Buffered(1))
    vspec = pl.BlockSpec((1, D_MODEL), const)
    return pl.pallas_call(
        functools.partial(_merge_kernel, sub=sub),
        grid=(t // tm,),
        in_specs=[
            pl.BlockSpec((tm, D_MODEL), tok),
            pl.BlockSpec((tm, ML_V), tok),
            pl.BlockSpec((tm, FOX_W), tok),
            pl.BlockSpec((tm, D_MODEL), lambda i: (i, COL_GA // D_MODEL)),
            pl.BlockSpec((tm, D_MODEL), lambda i: (i, COL_GB // D_MODEL)),
            wspec, wspec, wspec, vspec, vspec, vspec, vspec,
        ],
        out_specs=[pl.BlockSpec((tm, D_MODEL), tok), pl.BlockSpec((tm, D_MODEL), tok)],
        out_shape=[jax.ShapeDtypeStruct((t, D_MODEL), f32), jax.ShapeDtypeStruct((t, D_MODEL), bf16)],
        compiler_params=pltpu.CompilerParams(
            dimension_semantics=("arbitrary",), vmem_limit_bytes=VMEM_LIMIT),
        name="merge",
    )(x2, h_a, h_b, proj, proj, wa, wb, wo, bga, bgb, npost, npre)


def _ffn_kernel(x1_ref, h2_ref, wup_ref, cw_ref, cb_ref, wdn_ref, npost_ref, out_ref,
                halo_sc, ubuf_sc, act_sc, acc_sc, *, blocks_per_seq, tf, down_group):
    i = pl.program_id(0)
    tm = x1_ref.shape[0]

    @pl.when(i % blocks_per_seq == 0)
    def _():
        halo_sc[...] = jnp.zeros_like(halo_sc)

    h2 = h2_ref[...]
    n_chunks = D_FF // tf

    def up(c):
        for half in range(2):
            col0 = half * D_FF + c * tf
            slot = 2 * (c % 2) + half
            u = _dot(h2, wup_ref[:, col0:col0 + tf])
            ubuf_sc[slot, 0:SUBLANES, :] = halo_sc[:, col0:col0 + tf]
            ubuf_sc[slot, SUBLANES:SUBLANES + tm, :] = u
            halo_sc[:, col0:col0 + tf] = u[tm - SUBLANES:tm, :]

    def conv(c, half):
        col0 = half * D_FF + c * tf
        slot = 2 * (c % 2) + half
        w = cw_ref[:, col0:col0 + tf]
        y = cb_ref[:, col0:col0 + tf]
        for j in range(CONV_W):
            first = SUBLANES - (CONV_W - 1 - j)
            y = y + w[j:j + 1] * ubuf_sc[slot, first:first + tm, :]
        return y

    def gate(g, a):
        inner = g * (GELU_K1 + (GELU_K1 * GELU_CUBIC) * (g * g))
        return (g * a) * (1.0 / (1.0 + jnp.exp2(inner)))

    def down(c0, c1):
        return _dot(act_sc[:, c0 * tf:c1 * tf], wdn_ref[c0 * tf:c1 * tf, :])

    up(0)
    pending, done = None, 0
    for c in range(n_chunks):
        if c + 1 < n_chunks:
            up(c + 1)
        if pending is not None:
            if pending[0] == 0:
                acc_sc[...] = down(*pending)
            else:
                acc_sc[...] += down(*pending)
            pending = None
        act_sc[:, c * tf:(c + 1) * tf] = gate(conv(c, 1), conv(c, 0)).astype(bf16)
        if (c + 1) % down_group == 0 or c + 1 == n_chunks:
            pending, done = (done, c + 1), c + 1

    y = down(*pending)
    if pending[0] != 0:
        y = y + acc_sc[...]
    out_ref[...] = x1_ref[...] + _rms(y) * npost_ref[...]


def _ffn(x1, h2, wup, cw, cb, wdn, npost, *, seq, tm=512, tf=256, down_group=2):
    t = x1.shape[0]
    tok = lambda i: (i, 0)
    const = lambda i: (0, 0)
    kern = functools.partial(_ffn_kernel, blocks_per_seq=seq // tm, tf=tf, down_group=down_group)
    return pl.pallas_call(
        kern,
        grid=(t // tm,),
        in_specs=[
            pl.BlockSpec((tm, D_MODEL), tok),
            pl.BlockSpec((tm, D_MODEL), tok),
            pl.BlockSpec((D_MODEL, 2 * D_FF), const, pipeline_mode=pl.Buffered(1)),
            pl.BlockSpec((CONV_W, 2 * D_FF), const),
            pl.BlockSpec((1, 2 * D_FF), const),
            pl.BlockSpec((D_FF, D_MODEL), const, pipeline_mode=pl.Buffered(1)),
            pl.BlockSpec((1, D_MODEL), const),
        ],
        out_specs=pl.BlockSpec((tm, D_MODEL), tok),
        out_shape=jax.ShapeDtypeStruct((t, D_MODEL), f32),
        scratch_shapes=[pltpu.VMEM((SUBLANES, 2 * D_FF), f32),
                        pltpu.VMEM((4, tm + SUBLANES, tf), f32),
                        pltpu.VMEM((tm, D_FF), bf16),
                        pltpu.VMEM((tm, D_MODEL), f32)],
        compiler_params=pltpu.CompilerParams(
            dimension_semantics=("arbitrary",), vmem_limit_bytes=VMEM_LIMIT),
        name="ffn",
    )(x1, h2, wup, cw, cb, wdn, npost)


def _layer(x2, p, l, *, batch, seq):
    w_in = p["w_in"][l]
    o_i = 2 * ML_QK + ML_V
    o_o = o_i + 2 * ML_HEADS
    o_qf = o_o + ML_V
    o_ff = o_qf + 3 * FOX_W
    o_ga = o_ff + FOX_HEADS
    w_pieces = [
        w_in[:, :o_i].astype(bf16),
        w_in[:, o_o:o_ff].astype(bf16),
        w_in[:, o_ga:].astype(bf16),
    ]
    w_gate = jnp.concatenate([
        w_in[:, o_i:o_o], w_in[:, o_ff:o_ga],
        jnp.zeros((D_MODEL, GATE_W - N_GATES), f32)], axis=1).astype(bf16)
    b_gate = jnp.concatenate([
        p["b_ml_i"][l], p["b_ml_f"][l], p["b_fox_f"][l], jnp.zeros((GATE_W - N_GATES,), f32)])[None, :]
    tri = jnp.tril(jnp.ones((ML_CHUNK, ML_CHUNK), f32)).astype(bf16)
    row = lambda v: v[None, :].astype(f32)

    proj, gcol, grow = _inproj(x2, row(p["norm_mix_pre"][l]), w_pieces, w_gate, b_gate, tri, seq=seq)
    h_a = _mlstm(proj, gcol, grow, row(p["ml_head_norm"][l]), batch=batch, seq=seq)
    h_b = _fox(proj, gcol, batch=batch, seq=seq)
    x1, h2 = _merge(x2, h_a, h_b, proj,
                    p["w_branch_a"][l].astype(bf16), p["w_branch_b"][l].astype(bf16),
                    p["w_out"][l].astype(bf16), row(p["b_gate_a"][l]), row(p["b_gate_b"][l]),
                    row(p["norm_mix_post"][l]), row(p["norm_ffn_pre"][l]))
    return _ffn(x1, h2, p["w_up"][l].astype(bf16), p["conv_w"][l], row(p["conv_b"][l]),
                p["w_down"][l].astype(bf16), row(p["norm_ffn_post"][l]), seq=seq)


def kernel(x, norm_mix_pre, w_in, b_ml_i, b_ml_f, ml_head_norm, b_fox_f, b_gate_a, b_gate_b,
           w_branch_a, w_branch_b, w_out, norm_mix_post, norm_ffn_pre, w_up, conv_w, conv_b,
           w_down, norm_ffn_post):
    batch, seq, _ = x.shape
    p = dict(norm_mix_pre=norm_mix_pre, w_in=w_in, b_ml_i=b_ml_i, b_ml_f=b_ml_f,
             ml_head_norm=ml_head_norm, b_fox_f=b_fox_f, b_gate_a=b_gate_a, b_gate_b=b_gate_b,
             w_branch_a=w_branch_a, w_branch_b=w_branch_b, w_out=w_out,
             norm_mix_post=norm_mix_post, norm_ffn_pre=norm_ffn_pre, w_up=w_up, conv_w=conv_w,
             conv_b=conv_b, w_down=w_down, norm_ffn_post=norm_ffn_post)
    x2 = x.reshape(batch * seq, D_MODEL)
    for l in range(w_in.shape[0]):
        x2 = _layer(x2, p, l, batch=batch, seq=seq)
    return x2.reshape(batch, seq, D_MODEL)
```

```python
import functools

import jax
import jax.numpy as jnp
from jax import lax
from jax.experimental import pallas as pl
from jax.experimental.pallas import tpu as pltpu

D_MODEL = 1024
ML_HEADS = 4
ML_DQK = 128
ML_DV = 256
ML_QK = ML_HEADS * ML_DQK
ML_V = ML_HEADS * ML_DV
FOX_HEADS = 8
FOX_DH = 128
FOX_W = FOX_HEADS * FOX_DH
D_FF = 2816
CONV_W = 3
GATE_CAP = 15.0
EPS = 1e-6
LOG2E = 1.4426950408889634
GELU_CUBIC = 0.044715
GELU_K1 = -2.0 * (2.0 / 3.141592653589793) ** 0.5 * LOG2E

LANES = 128
SUBLANES = 8
GATE_W = LANES
N_GATES = 2 * ML_HEADS + FOX_HEADS
ML_CHUNK = 128
VMEM_LIMIT = 56 * 1024 * 1024

PROJ_W = 2 * ML_QK + 2 * ML_V + 3 * FOX_W + 2 * D_MODEL
COL_QM, COL_KM, COL_VM, COL_OM = 0, ML_QK, 2 * ML_QK, 2 * ML_QK + ML_V
COL_QF = COL_OM + ML_V
COL_KF = COL_QF + FOX_W
COL_VF = COL_KF + FOX_W
COL_GA = COL_VF + FOX_W
COL_GB = COL_GA + D_MODEL

f32 = jnp.float32
bf16 = jnp.bfloat16


def _log_sigmoid(z):
    return jnp.minimum(z, 0.0) - jnp.log1p(jnp.exp(-jnp.abs(z)))


def _rms(v):
    return v * lax.rsqrt(jnp.mean(v * v, axis=-1, keepdims=True) + EPS)


def _dot(a, b):
    return jnp.dot(a, b, preferred_element_type=f32)


def _dot_nt(a, b):
    return lax.dot_general(a, b, (((1,), (1,)), ((), ())), preferred_element_type=f32)


def _dot_tn(a, b):
    return lax.dot_general(a, b, (((0,), (0,)), ((), ())), preferred_element_type=f32)


def _inproj_kernel(*refs, blocks_per_seq, piece, w_blocks):
    x_ref, nw_ref, wg_ref, bg_ref, tri_ref, cs_ref = refs[:6]
    w_refs = refs[6:6 + len(w_blocks)]
    proj_ref, gcol_ref, grow_ref, h_sc, carry_sc = refs[6 + len(w_blocks):]
    assert w_blocks[0][0] == 0
    i = pl.program_id(0)
    j = pl.program_id(1)
    tm = x_ref.shape[0]

    def project(hb, w_ref=w_refs[0]):
        return (_dot(hb, w_ref[...]) * cs_ref[pl.ds(j, 1), :]).astype(bf16)

    @pl.when(j == 0)
    def _():
        @pl.when(i % blocks_per_seq == 0)
        def _():
            carry_sc[...] = jnp.zeros_like(carry_sc)

        tri = tri_ref[...]
        lane = lax.broadcasted_iota(jnp.int32, (ML_CHUNK, GATE_W), 1)
        per_piece = piece // ML_CHUNK

        def normed(p):
            prow = slice(p * piece, (p + 1) * piece)
            hb = (_rms(x_ref[prow, :]) * nw_ref[...]).astype(bf16)
            h_sc[prow, :] = hb
            return hb

        acts = []
        hb = normed(0)
        for p in range(tm // piece):
            proj_ref[p * piece:(p + 1) * piece, :] = project(hb)
            g_piece = _dot(hb, wg_ref[...]) + bg_ref[...]
            if (p + 1) * piece < tm:
                hb = normed(p + 1)
            for q in range(per_piece):
                g = g_piece[q * ML_CHUNK:(q + 1) * ML_CHUNK]
                cap = GATE_CAP * jnp.tanh(g / GATE_CAP)
                a = jnp.where(lane < ML_HEADS, cap,
                              jnp.where(lane < 2 * ML_HEADS, _log_sigmoid(cap), _log_sigmoid(g)))
                acts.append(jnp.where(lane < N_GATES, a, 0.0))

        for r in range(tm // ML_CHUNK):
            rows = slice(r * ML_CHUNK, (r + 1) * ML_CHUNK)
            a = acts[r]
            a_hi = a.astype(bf16)
            r1 = a - a_hi.astype(f32)
            a_mid = r1.astype(bf16)
            a_lo = (r1 - a_mid.astype(f32)).astype(bf16)
            cs3 = _dot(tri, jnp.concatenate([a_hi, a_mid, a_lo], axis=1))
            cs = cs3[:, :GATE_W] + cs3[:, GATE_W:2 * GATE_W] + cs3[:, 2 * GATE_W:]
            glob = cs + carry_sc[...]
            carry_sc[...] = glob[ML_CHUNK - 1:ML_CHUNK, :]
            out = jnp.where(lane < ML_HEADS, a, jnp.where(lane < 2 * ML_HEADS, cs, glob))
            gcol_ref[rows, :] = out
            grow_ref[:, rows] = out.T[:N_GATES, :]

    for w_ref, (first, count) in zip(w_refs, w_blocks):
        @pl.when((j >= max(first, 1)) & (j < first + count))
        def _(w_ref=w_ref):
            proj_ref[...] = project(h_sc[...], w_ref)


def _proj_col_scale():
    cs = jnp.ones((1, PROJ_W), f32)
    cs = cs.at[:, COL_QM:COL_QM + ML_QK].set(ML_DQK ** -0.5)
    return cs.at[:, COL_QF:COL_QF + FOX_W].set(LOG2E * FOX_DH ** -0.5)


def _inproj(x2, nw, w_pieces, w_gate, b_gate, tri, *, seq, tm=2048, tn=1024, piece=512):
    t = x2.shape[0]
    tm = min(tm, seq)
    assert seq % tm == 0 and tm % piece == 0
    piece = max(piece, ML_CHUNK)
    w_blocks, first = [], 0
    for w in w_pieces:
        w_blocks.append((first, w.shape[1] // tn))
        first += w.shape[1] // tn
    assert first == PROJ_W // tn
    kern = functools.partial(_inproj_kernel, blocks_per_seq=seq // tm, piece=piece, w_blocks=tuple(w_blocks))

    def w_spec(first, count):
        return pl.BlockSpec((D_MODEL, tn), lambda i, j: (
            0, jnp.where(j < first - 1, count - 1, jnp.clip(j - first, 0, count - 1))))

    return pl.pallas_call(
        kern,
        grid=(t // tm, PROJ_W // tn),
        in_specs=[
            pl.BlockSpec((tm, D_MODEL), lambda i, j: (i, 0)),
            pl.BlockSpec((1, D_MODEL), lambda i, j: (0, 0)),
            pl.BlockSpec((D_MODEL, GATE_W), lambda i, j: (0, 0)),
            pl.BlockSpec((1, GATE_W), lambda i, j: (0, 0)),
            pl.BlockSpec((ML_CHUNK, ML_CHUNK), lambda i, j: (0, 0)),
            pl.BlockSpec((PROJ_W // tn, tn), lambda i, j: (0, 0)),
        ] + [w_spec(f, n) for f, n in w_blocks],
        out_specs=[
            pl.BlockSpec((tm, tn), lambda i, j: (i, j)),
            pl.BlockSpec((tm, GATE_W), lambda i, j: (i, 0)),
            pl.BlockSpec((N_GATES, tm), lambda i, j: (0, i)),
        ],
        out_shape=[
            jax.ShapeDtypeStruct((t, PROJ_W), bf16),
            jax.ShapeDtypeStruct((t, GATE_W), f32),
            jax.ShapeDtypeStruct((N_GATES, t), f32),
        ],
        scratch_shapes=[pltpu.VMEM((tm, D_MODEL), bf16), pltpu.VMEM((1, GATE_W), f32)],
        compiler_params=pltpu.CompilerParams(
            dimension_semantics=("arbitrary", "arbitrary"), vmem_limit_bytes=VMEM_LIMIT),
        name="inproj",
    )(x2, nw, w_gate, b_gate, tri, _proj_col_scale().reshape(PROJ_W // tn, tn), *w_pieces)


def _mlstm_kernel(qkvo_ref, gcol_ref, grow_ref, hnw_ref, spread_ref, out_ref, s_sc, m_sc):
    c = pl.program_id(1)
    L = ML_CHUNK
    nb = qkvo_ref.shape[0]
    assert L == LANES

    @pl.when(c == 0)
    def _():
        s_sc[...] = jnp.zeros_like(s_sc)
        m_sc[...] = jnp.zeros_like(m_sc)

    row = lax.broadcasted_iota(jnp.int32, (L, L), 0)
    col = lax.broadcasted_iota(jnp.int32, (L, L), 1)
    causal = col <= row
    ones_blk = jnp.ones((L, LANES), bf16)

    def wide(tile, n):
        return jnp.concatenate([tile] * n, axis=1)

    P = [(n, h) for n in range(nb) for h in range(ML_HEADS)]
    C = range(len(P))
    qk_cols = lambda h, base: slice(base + h * ML_DQK, base + (h + 1) * ML_DQK)
    v_cols = lambda h, base=0: slice(base + h * ML_DV, base + (h + 1) * ML_DV)
    spread = []
    for n in range(nb):
        g = gcol_ref[n]
        g_hi = g.astype(bf16)
        r1 = g - g_hi.astype(f32)
        g_mid = r1.astype(bf16)
        g_lo = (r1 - g_mid.astype(f32)).astype(bf16)
        spread.append(_dot(jnp.concatenate([g_hi, g_mid, g_lo], axis=1), spread_ref[...]))
    tile = lambda n, j: spread[n][:, j * LANES:(j + 1) * LANES]
    b_t = [tile(n, h) for n, h in P]
    a_t = [tile(n, ML_HEADS + h) for n, h in P]
    gr = [grow_ref[n] for n in range(nb)]
    a_r = [gr[n][h:h + 1, :] - gr[n][ML_HEADS + h:ML_HEADS + h + 1, :] for n, h in P]
    q = [qkvo_ref[n, :, qk_cols(h, COL_QM)] for n, h in P]
    k = [qkvo_ref[n, :, qk_cols(h, COL_KM)] for n, h in P]
    v_aug = [jnp.concatenate([qkvo_ref[n, :, v_cols(h, COL_VM)], ones_blk], axis=1) for n, h in P]
    m_prev = [m_sc[i:i + 1, :] for i in C]
    state = [s_sc[i] for i in C]

    qk = [_dot_nt(q[i], k[i]) for i in C]
    amat = [jnp.where(causal, a_r[i], -jnp.inf) for i in C]
    mx = [jnp.maximum(m_prev[i], jnp.broadcast_to(jnp.max(amat[i], axis=-1, keepdims=True), (L, LANES)))
          for i in C]

    mx_last = [mx[i][L - 1:L, :] for i in C]
    b_last = [b_t[i][L - 1:L, :] for i in C]
    wk = [jnp.exp(a_t[i] - mx_last[i]) for i in C]
    upd = [_dot_tn((wk[i] * k[i].astype(f32)).astype(bf16), v_aug[i]) for i in C]

    s = [(qk[i] * jnp.exp(amat[i] - mx[i])).astype(bf16) for i in C]
    q_in = [(jnp.exp(m_prev[i] - mx[i]) * q[i].astype(f32)).astype(bf16) for i in C]
    num = [_dot(jnp.concatenate([s[i], q_in[i]], axis=1),
                jnp.concatenate([v_aug[i], state[i].astype(bf16)], axis=0)) for i in C]
    for i in C:
        s_sc[i] = wide(jnp.exp(m_prev[i] - mx_last[i]), 3) * state[i] + upd[i]
        m_sc[i:i + 1, :] = b_last[i] + mx_last[i]

    inv = [1.0 / jnp.maximum(jnp.abs(num[i][:, ML_DV:]), jnp.exp(-b_t[i] - mx[i])) for i in C]
    msq = [jnp.broadcast_to(jnp.mean(num[i][:, :ML_DV] * num[i][:, :ML_DV], axis=-1, keepdims=True),
                            (L, LANES)) for i in C]
    fac = [inv[i] * lax.rsqrt(inv[i] * inv[i] * msq[i] + EPS) for i in C]
    for i, (n, h) in enumerate(P):
        gate = jax.nn.sigmoid(qkvo_ref[n, :, v_cols(h, COL_OM)].astype(f32)) * hnw_ref[:, v_cols(h)]
        out_ref[n, :, v_cols(h)] = (num[i][:, :ML_DV] * wide(fac[i], ML_DV // LANES) * gate).astype(bf16)


def _mlstm(proj, gcol, grow, hnw, *, batch, seq, nb=4):
    while batch % nb:
        nb //= 2
    nc = seq // ML_CHUNK
    L = ML_CHUNK
    proj4 = proj.reshape(batch // nb, nb, seq, PROJ_W)
    gcol4 = gcol.reshape(batch // nb, nb, seq, GATE_W)
    grow4 = grow.reshape(N_GATES, batch // nb, nb, seq).transpose(1, 2, 0, 3)
    src = lax.broadcasted_iota(jnp.int32, (GATE_W, 2 * ML_HEADS * LANES), 0)
    dst = lax.broadcasted_iota(jnp.int32, (GATE_W, 2 * ML_HEADS * LANES), 1) // LANES
    spread = (jnp.where(src == ML_HEADS + dst % ML_HEADS, jnp.where(dst < ML_HEADS, 1.0, -1.0), 0.0)
              + jnp.where((dst >= ML_HEADS) & (src == dst - ML_HEADS), 1.0, 0.0)).astype(bf16)
    spread = jnp.concatenate([spread] * 3, axis=0)
    out = pl.pallas_call(
        _mlstm_kernel,
        grid=(batch // nb, nc),
        in_specs=[
            pl.BlockSpec((None, nb, L, COL_QF), lambda b, c: (b, 0, c, 0)),
            pl.BlockSpec((None, nb, L, GATE_W), lambda b, c: (b, 0, c, 0)),
            pl.BlockSpec((None, nb, N_GATES, L), lambda b, c: (b, 0, 0, c)),
            pl.BlockSpec((1, ML_V), lambda b, c: (0, 0)),
            pl.BlockSpec((3 * GATE_W, 2 * ML_HEADS * LANES), lambda b, c: (0, 0)),
        ],
        out_specs=pl.BlockSpec((None, nb, L, ML_V), lambda b, c: (b, 0, c, 0)),
        out_shape=jax.ShapeDtypeStruct((batch // nb, nb, seq, ML_V), bf16),
        scratch_shapes=[pltpu.VMEM((nb * ML_HEADS, ML_DQK, ML_DV + LANES), f32),
                        pltpu.VMEM((nb * ML_HEADS, LANES), f32)],
        compiler_params=pltpu.CompilerParams(
            dimension_semantics=("arbitrary", "arbitrary"), vmem_limit_bytes=VMEM_LIMIT),
        name="mlstm",
    )(proj4, gcol4, grow4, hnw, spread)
    return out.reshape(batch * seq, ML_V)


def _fox_kernel(q_ref, k_ref, v_ref, gcol_ref, out_ref, kaug_sc, st_sc, mx_sc, m_sc, l_sc, acc_sc,
                *, blk, setup_rows, ahead):
    hg = pl.program_id(1)
    seq = k_ref.shape[0]
    nh = kaug_sc.shape[0]
    head_cols = [slice(i * FOX_DH, (i + 1) * FOX_DH) for i in range(nh)]

    r = lax.broadcasted_iota(jnp.int32, (GATE_W, nh * LANES), 0)
    c = lax.broadcasted_iota(jnp.int32, (GATE_W, nh * LANES), 1)
    src = 2 * ML_HEADS + hg * nh + c // LANES
    pick = jnp.concatenate([((r == src) & (c % LANES == j)).astype(bf16) for j in range(3)], axis=0)
    for n in range(seq // setup_rows):
        rows = pl.ds(n * setup_rows, setup_rows)
        a = gcol_ref[rows, :] * (-LOG2E)
        a_hi = a.astype(bf16)
        r1 = a - a_hi.astype(f32)
        a_mid = r1.astype(bf16)
        a_lo = (r1 - a_mid.astype(f32)).astype(bf16)
        aug = _dot(jnp.concatenate([a_hi, a_mid, a_lo], axis=1), pick).astype(bf16)
        for i in range(nh):
            kaug_sc[i, rows, :FOX_DH] = k_ref[rows, head_cols[i]]
            kaug_sc[i, rows, FOX_DH:] = aug[:, i * LANES:(i + 1) * LANES]

    ones3 = (lax.broadcasted_iota(jnp.int32, (blk, LANES), 1) < 3).astype(bf16)

    def query_block(qi, _):
        qrows = pl.ds(pl.multiple_of(qi * blk, blk), blk)
        q_aug = [jnp.concatenate([q_ref[qrows, head_cols[i]], ones3], axis=1) for i in range(nh)]

        m_sc[...] = jnp.full(m_sc.shape, -jnp.inf, f32)
        l_sc[...] = jnp.zeros(l_sc.shape, f32)
        acc_sc[...] = jnp.zeros(acc_sc.shape, f32)

        def logits_stage(i, j, masked):
            start = pl.multiple_of(j * blk, blk)
            st = _dot_nt(kaug_sc[i, pl.ds(start, blk), :], q_aug[i])
            if masked:
                row = lax.broadcasted_iota(jnp.int32, (blk, blk), 0)
                col = lax.broadcasted_iota(jnp.int32, (blk, blk), 1)
                st = jnp.where(row <= col, st, -jnp.inf)
            st_sc[i] = st
            mx_sc[i] = jnp.max(st, axis=0, keepdims=True)

        def softmax_stage(i, j):
            start = pl.multiple_of(j * blk, blk)
            m = m_sc[i]
            m_new = jnp.maximum(m, mx_sc[i])
            alpha = jnp.exp2(m - m_new)
            p = jnp.exp2(st_sc[i] - m_new)
            l_sc[i] = alpha * l_sc[i] + jnp.sum(p, axis=0, keepdims=True)
            acc_sc[i] = alpha * acc_sc[i] + _dot_tn(v_ref[pl.ds(start, blk), head_cols[i]], p.astype(bf16))
            m_sc[i] = m_new

        def round_robin(cur, nxt, masked):
            for i in range(nh):
                if i + ahead < nh:
                    logits_stage(i + ahead, cur, masked)
                else:
                    logits_stage(i + ahead - nh, nxt, False)
                softmax_stage(i, cur)

        for i in range(ahead):
            logits_stage(i, qi, True)
        round_robin(qi, 0, True)

        def body(k, _):
            round_robin(k - 1, jnp.minimum(k, qi - 1), False)
            return 0

        lax.fori_loop(1, qi + 1, body, 0)

        for i in range(nh):
            out_ref[qrows, head_cols[i]] = (acc_sc[i] * (1.0 / l_sc[i])).T.astype(bf16)
        return 0

    lax.fori_loop(0, seq // blk, query_block, 0)


def _fox(proj, gcol, *, batch, seq, blk=512, nh=4, ahead=1):
    assert 1 <= ahead < nh
    assert seq % blk == 0 and FOX_HEADS % nh == 0
    t = proj.shape[0]
    w = nh * FOX_DH
    kern = functools.partial(_fox_kernel, blk=blk, setup_rows=blk, ahead=ahead)
    return pl.pallas_call(
        kern,
        grid=(batch, FOX_HEADS // nh),
        in_specs=[
            pl.BlockSpec((seq, w), lambda b, h: (b, COL_QF // w + h)),
            pl.BlockSpec((seq, w), lambda b, h: (b, COL_KF // w + h)),
            pl.BlockSpec((seq, w), lambda b, h: (b, COL_VF // w + h)),
            pl.BlockSpec((seq, GATE_W), lambda b, h: (b, 0)),
        ],
        out_specs=pl.BlockSpec((seq, w), lambda b, h: (b, h)),
        out_shape=jax.ShapeDtypeStruct((t, FOX_W), bf16),
        scratch_shapes=[pltpu.VMEM((nh, seq, FOX_DH + LANES), bf16),
                        pltpu.VMEM((nh, blk, blk), f32),
                        pltpu.VMEM((nh, 1, blk), f32),
                        pltpu.VMEM((nh, 1, blk), f32),
                        pltpu.VMEM((nh, 1, blk), f32),
                        pltpu.VMEM((nh, FOX_DH, blk), f32)],
        compiler_params=pltpu.CompilerParams(
            dimension_semantics=("arbitrary", "arbitrary"), vmem_limit_bytes=VMEM_LIMIT),
        name="fox",
    )(proj, proj, proj, gcol)


def _merge_kernel(x_ref, ha_ref, hb_ref, ga_ref, gb_ref, wa_ref, wb_ref, wo_ref,
                  bga_ref, bgb_ref, npost_ref, npre_ref, x1_ref, h2_ref, *, sub):
    n_sub = x_ref.shape[0] // sub
    rows = [slice(r * sub, (r + 1) * sub) for r in range(n_sub)]

    def branches(r):
        return _dot(ha_ref[rows[r], :], wa_ref[...]), _dot(hb_ref[rows[r], :], wb_ref[...])

    def finish(r, z):
        x1 = x_ref[rows[r], :] + _rms(z) * npost_ref[...]
        x1_ref[rows[r], :] = x1
        h2_ref[rows[r], :] = (_rms(x1) * npre_ref[...]).astype(bf16)

    y_next = branches(0)
    z_prev = None
    for r in range(n_sub):
        ya, yb = y_next
        if r + 1 < n_sub:
            y_next = branches(r + 1)
        merged = (jax.nn.sigmoid(ga_ref[rows[r], :].astype(f32) + bga_ref[...]) * ya
                  + jax.nn.sigmoid(gb_ref[rows[r], :].astype(f32) + bgb_ref[...]) * yb)
        z = _dot(merged.astype(bf16), wo_ref[...])
        if z_prev is not None:
            finish(r - 1, z_prev)
        z_prev = z
    finish(n_sub - 1, z_prev)


def _merge(x2, h_a, h_b, proj, wa, wb, wo, bga, bgb, npost, npre, *, tm=1024, sub=512):
    t = x2.shape[0]
    tm = min(tm, t)
    tok = lambda i: (i, 0)
    const = lambda i: (0, 0)
    wspec = pl.BlockSpec((D_MODEL, D_MODEL), const, pipeline_mode=pl.Buffered(1))
    vspec = pl.BlockSpec((1, D_MODEL), const)
    return pl.pallas_call(
        functools.partial(_merge_kernel, sub=sub),
        grid=(t // tm,),
        in_specs=[
            pl.BlockSpec((tm, D_MODEL), tok),
            pl.BlockSpec((tm, ML_V), tok),
            pl.BlockSpec((tm, FOX_W), tok),
            pl.BlockSpec((tm, D_MODEL), lambda i: (i, COL_GA // D_MODEL)),
            pl.BlockSpec((tm, D_MODEL), lambda i: (i, COL_GB // D_MODEL)),
            wspec, wspec, wspec, vspec, vspec, vspec, vspec,
        ],
        out_specs=[pl.BlockSpec((tm, D_MODEL), tok), pl.BlockSpec((tm, D_MODEL), tok)],
        out_shape=[jax.ShapeDtypeStruct((t, D_MODEL), f32), jax.ShapeDtypeStruct((t, D_MODEL), bf16)],
        compiler_params=pltpu.CompilerParams(
            dimension_semantics=("arbitrary",), vmem_limit_bytes=VMEM_LIMIT),
        name="merge",
    )(x2, h_a, h_b, proj, proj, wa, wb, wo, bga, bgb, npost, npre)


def _ffn_kernel(x1_ref, h2_ref, wup_ref, cw_ref, cb_ref, wdn_ref, npost_ref, out_ref,
                halo_sc, ubuf_sc, act_sc, acc_sc, *, blocks_per_seq, tf, down_group):
    i = pl.program_id(0)
    tm = x1_ref.shape[0]

    @pl.when(i % blocks_per_seq == 0)
    def _():
        halo_sc[...] = jnp.zeros_like(halo_sc)

    h2 = h2_ref[...]
    n_chunks = D_FF // tf

    def up(c):
        for half in range(2):
            col0 = half * D_FF + c * tf
            slot = 2 * (c % 2) + half
            u = _dot(h2, wup_ref[:, col0:col0 + tf])
            ubuf_sc[slot, 0:SUBLANES, :] = halo_sc[:, col0:col0 + tf]
            ubuf_sc[slot, SUBLANES:SUBLANES + tm, :] = u
            halo_sc[:, col0:col0 + tf] = u[tm - SUBLANES:tm, :]

    def conv(c, half):
        col0 = half * D_FF + c * tf
        slot = 2 * (c % 2) + half
        w = cw_ref[:, col0:col0 + tf]
        y = cb_ref[:, col0:col0 + tf]
        for j in range(CONV_W):
            first = SUBLANES - (CONV_W - 1 - j)
            y = y + w[j:j + 1] * ubuf_sc[slot, first:first + tm, :]
        return y

    def gate(g, a):
        inner = g * (GELU_K1 + (GELU_K1 * GELU_CUBIC) * (g * g))
        return (g * a) * (1.0 / (1.0 + jnp.exp2(inner)))

    def down(c0, c1):
        return _dot(act_sc[:, c0 * tf:c1 * tf], wdn_ref[c0 * tf:c1 * tf, :])

    up(0)
    pending, done = None, 0
    for c in range(n_chunks):
        if c + 1 < n_chunks:
            up(c + 1)
        if pending is not None:
            if pending[0] == 0:
                acc_sc[...] = down(*pending)
            else:
                acc_sc[...] += down(*pending)
            pending = None
        act_sc[:, c * tf:(c + 1) * tf] = gate(conv(c, 1), conv(c, 0)).astype(bf16)
        if (c + 1) % down_group == 0 or c + 1 == n_chunks:
            pending, done = (done, c + 1), c + 1

    y = down(*pending)
    if pending[0] != 0:
        y = y + acc_sc[...]
    out_ref[...] = x1_ref[...] + _rms(y) * npost_ref[...]


def _ffn(x1, h2, wup, cw, cb, wdn, npost, *, seq, tm=512, tf=256, down_group=2):
    t = x1.shape[0]
    tok = lambda i: (i, 0)
    const = lambda i: (0, 0)
    kern = functools.partial(_ffn_kernel, blocks_per_seq=seq // tm, tf=tf, down_group=down_group)
    return pl.pallas_call(
        kern,
        grid=(t // tm,),
        in_specs=[
            pl.BlockSpec((tm, D_MODEL), tok),
            pl.BlockSpec((tm, D_MODEL), tok),
            pl.BlockSpec((D_MODEL, 2 * D_FF), const, pipeline_mode=pl.Buffered(1)),
            pl.BlockSpec((CONV_W, 2 * D_FF), const),
            pl.BlockSpec((1, 2 * D_FF), const),
            pl.BlockSpec((D_FF, D_MODEL), const, pipeline_mode=pl.Buffered(1)),
            pl.BlockSpec((1, D_MODEL), const),
        ],
        out_specs=pl.BlockSpec((tm, D_MODEL), tok),
        out_shape=jax.ShapeDtypeStruct((t, D_MODEL), f32),
        scratch_shapes=[pltpu.VMEM((SUBLANES, 2 * D_FF), f32),
                        pltpu.VMEM((4, tm + SUBLANES, tf), f32),
                        pltpu.VMEM((tm, D_FF), bf16),
                        pltpu.VMEM((tm, D_MODEL), f32)],
        compiler_params=pltpu.CompilerParams(
            dimension_semantics=("arbitrary",), vmem_limit_bytes=VMEM_LIMIT),
        name="ffn",
    )(x1, h2, wup, cw, cb, wdn, npost)


def _layer(x2, p, l, *, batch, seq):
    w_in = p["w_in"][l]
    o_i = 2 * ML_QK + ML_V
    o_o = o_i + 2 * ML_HEADS
    o_qf = o_o + ML_V
    o_ff = o_qf + 3 * FOX_W
    o_ga = o_ff + FOX_HEADS
    w_pieces = [
        w_in[:, :o_i].astype(bf16),
        w_in[:, o_o:o_ff].astype(bf16),
        w_in[:, o_ga:].astype(bf16),
    ]
    w_gate = jnp.concatenate([
        w_in[:, o_i:o_o], w_in[:, o_ff:o_ga],
        jnp.zeros((D_MODEL, GATE_W - N_GATES), f32)], axis=1).astype(bf16)
    b_gate = jnp.concatenate([
        p["b_ml_i"][l], p["b_ml_f"][l], p["b_fox_f"][l], jnp.zeros((GATE_W - N_GATES,), f32)])[None, :]
    tri = jnp.tril(jnp.ones((ML_CHUNK, ML_CHUNK), f32)).astype(bf16)
    row = lambda v: v[None, :].astype(f32)

    proj, gcol, grow = _inproj(x2, row(p["norm_mix_pre"][l]), w_pieces, w_gate, b_gate, tri, seq=seq)
    h_a = _mlstm(proj, gcol, grow, row(p["ml_head_norm"][l]), batch=batch, seq=seq)
    h_b = _fox(proj, gcol, batch=batch, seq=seq)
    x1, h2 = _merge(x2, h_a, h_b, proj,
                    p["w_branch_a"][l].astype(bf16), p["w_branch_b"][l].astype(bf16),
                    p["w_out"][l].astype(bf16), row(p["b_gate_a"][l]), row(p["b_gate_b"][l]),
                    row(p["norm_mix_post"][l]), row(p["norm_ffn_pre"][l]))
    return _ffn(x1, h2, p["w_up"][l].astype(bf16), p["conv_w"][l], row(p["conv_b"][l]),
                p["w_down"][l].astype(bf16), row(p["norm_ffn_post"][l]), seq=seq)


def kernel(x, norm_mix_pre, w_in, b_ml_i, b_ml_f, ml_head_norm, b_fox_f, b_gate_a, b_gate_b,
           w_branch_a, w_branch_b, w_out, norm_mix_post, norm_ffn_pre, w_up, conv_w, conv_b,
           w_down, norm_ffn_post):
    batch, seq, _ = x.shape
    p = dict(norm_mix_pre=norm_mix_pre, w_in=w_in, b_ml_i=b_ml_i, b_ml_f=b_ml_f,
             ml_head_norm=ml_head_norm, b_fox_f=b_fox_f, b_gate_a=b_gate_a, b_gate_b=b_gate_b,
             w_branch_a=w_branch_a, w_branch_b=w_branch_b, w_out=w_out,
             norm_mix_post=norm_mix_post, norm_ffn_pre=norm_ffn_pre, w_up=w_up, conv_w=conv_w,
             conv_b=conv_b, w_down=w_down, norm_ffn_post=norm_ffn_post)
    x2 = x.reshape(batch * seq, D_MODEL)
    for l in range(w_in.shape[0]):
        x2 = _layer(x2, p, l, batch=batch, seq=seq)
    return x2.reshape(batch, seq, D_MODEL)
```

```python
import functools

import jax
import jax.numpy as jnp
from jax import lax
from jax.experimental import pallas as pl
from jax.experimental.pallas import tpu as pltpu

D_MODEL = 1024
ML_HEADS = 4
ML_DQK = 128
ML_DV = 256
ML_QK = ML_HEADS * ML_DQK
ML_V = ML_HEADS * ML_DV
FOX_HEADS = 8
FOX_DH = 128
FOX_W = FOX_HEADS * FOX_DH
D_FF = 2816
CONV_W = 3
GATE_CAP = 15.0
EPS = 1e-6
LOG2E = 1.4426950408889634
GELU_CUBIC = 0.044715
GELU_K1 = -2.0 * (2.0 / 3.141592653589793) ** 0.5 * LOG2E

LANES = 128
SUBLANES = 8
GATE_W = LANES
N_GATES = 2 * ML_HEADS + FOX_HEADS
ML_CHUNK = 128
VMEM_LIMIT = 56 * 1024 * 1024

PROJ_W = 2 * ML_QK + 2 * ML_V + 3 * FOX_W + 2 * D_MODEL
COL_QM, COL_KM, COL_VM, COL_OM = 0, ML_QK, 2 * ML_QK, 2 * ML_QK + ML_V
COL_QF = COL_OM + ML_V
COL_KF = COL_QF + FOX_W
COL_VF = COL_KF + FOX_W
COL_GA = COL_VF + FOX_W
COL_GB = COL_GA + D_MODEL

f32 = jnp.float32
bf16 = jnp.bfloat16


def _log_sigmoid(z):
    return jnp.minimum(z, 0.0) - jnp.log1p(jnp.exp(-jnp.abs(z)))


def _rms(v):
    return v * lax.rsqrt(jnp.mean(v * v, axis=-1, keepdims=True) + EPS)


def _dot(a, b):
    return jnp.dot(a, b, preferred_element_type=f32)


def _dot_nt(a, b):
    return lax.dot_general(a, b, (((1,), (1,)), ((), ())), preferred_element_type=f32)


def _dot_tn(a, b):
    return lax.dot_general(a, b, (((0,), (0,)), ((), ())), preferred_element_type=f32)


def _inproj_kernel(*refs, blocks_per_seq, piece, w_blocks):
    x_ref, nw_ref, wg_ref, bg_ref, tri_ref, cs_ref = refs[:6]
    w_refs = refs[6:6 + len(w_blocks)]
    proj_ref, gcol_ref, grow_ref, h_sc, carry_sc = refs[6 + len(w_blocks):]
    assert w_blocks[0][0] == 0
    i = pl.program_id(0)
    j = pl.program_id(1)
    tm = x_ref.shape[0]

    def project(hb, w_ref=w_refs[0]):
        return (_dot(hb, w_ref[...]) * cs_ref[...]).astype(bf16)

    @pl.when(j == 0)
    def _():
        @pl.when(i % blocks_per_seq == 0)
        def _():
            carry_sc[...] = jnp.zeros_like(carry_sc)

        tri = tri_ref[...]
        lane = lax.broadcasted_iota(jnp.int32, (ML_CHUNK, GATE_W), 1)
        per_piece = piece // ML_CHUNK

        def normed(p):
            prow = slice(p * piece, (p + 1) * piece)
            hb = (_rms(x_ref[prow, :]) * nw_ref[...]).astype(bf16)
            h_sc[prow, :] = hb
            return hb

        acts = []
        hb = normed(0)
        for p in range(tm // piece):
            proj_ref[p * piece:(p + 1) * piece, :] = project(hb)
            g_piece = _dot(hb, wg_ref[...]) + bg_ref[...]
            if (p + 1) * piece < tm:
                hb = normed(p + 1)
            for q in range(per_piece):
                g = g_piece[q * ML_CHUNK:(q + 1) * ML_CHUNK]
                cap = GATE_CAP * jnp.tanh(g / GATE_CAP)
                a = jnp.where(lane < ML_HEADS, cap,
                              jnp.where(lane < 2 * ML_HEADS, _log_sigmoid(cap), _log_sigmoid(g)))
                acts.append(jnp.where(lane < N_GATES, a, 0.0))

        for r in range(tm // ML_CHUNK):
            rows = slice(r * ML_CHUNK, (r + 1) * ML_CHUNK)
            a = acts[r]
            a_hi = a.astype(bf16)
            r1 = a - a_hi.astype(f32)
            a_mid = r1.astype(bf16)
            a_lo = (r1 - a_mid.astype(f32)).astype(bf16)
            cs3 = _dot(tri, jnp.concatenate([a_hi, a_mid, a_lo], axis=1))
            cs = cs3[:, :GATE_W] + cs3[:, GATE_W:2 * GATE_W] + cs3[:, 2 * GATE_W:]
            glob = cs + carry_sc[...]
            carry_sc[...] = glob[ML_CHUNK - 1:ML_CHUNK, :]
            out = jnp.where(lane < ML_HEADS, a, jnp.where(lane < 2 * ML_HEADS, cs, glob))
            gcol_ref[rows, :] = out
            grow_ref[:, rows] = out.T[:N_GATES, :]

    for w_ref, (first, count) in zip(w_refs, w_blocks):
        @pl.when((j >= max(first, 1)) & (j < first + count))
        def _(w_ref=w_ref):
            proj_ref[...] = project(h_sc[...], w_ref)


def _proj_col_scale():
    cs = jnp.ones((1, PROJ_W), f32)
    cs = cs.at[:, COL_QM:COL_QM + ML_QK].set(ML_DQK ** -0.5)
    return cs.at[:, COL_QF:COL_QF + FOX_W].set(LOG2E * FOX_DH ** -0.5)


def _inproj(x2, nw, w_pieces, w_gate, b_gate, tri, *, seq, tm=2048, tn=1024, piece=512):
    t = x2.shape[0]
    tm = min(tm, seq)
    assert seq % tm == 0 and tm % piece == 0
    piece = max(piece, ML_CHUNK)
    w_blocks, first = [], 0
    for w in w_pieces:
        w_blocks.append((first, w.shape[1] // tn))
        first += w.shape[1] // tn
    assert first == PROJ_W // tn
    kern = functools.partial(_inproj_kernel, blocks_per_seq=seq // tm, piece=piece, w_blocks=tuple(w_blocks))

    def w_spec(first, count):
        return pl.BlockSpec((D_MODEL, tn), lambda i, j: (
            0, jnp.where(j < first - 1, count - 1, jnp.clip(j - first, 0, count - 1))))

    return pl.pallas_call(
        kern,
        grid=(t // tm, PROJ_W // tn),
        in_specs=[
            pl.BlockSpec((tm, D_MODEL), lambda i, j: (i, 0)),
            pl.BlockSpec((1, D_MODEL), lambda i, j: (0, 0)),
            pl.BlockSpec((D_MODEL, GATE_W), lambda i, j: (0, 0)),
            pl.BlockSpec((1, GATE_W), lambda i, j: (0, 0)),
            pl.BlockSpec((ML_CHUNK, ML_CHUNK), lambda i, j: (0, 0)),
            pl.BlockSpec((1, tn), lambda i, j: (0, j)),
        ] + [w_spec(f, n) for f, n in w_blocks],
        out_specs=[
            pl.BlockSpec((tm, tn), lambda i, j: (i, j)),
            pl.BlockSpec((tm, GATE_W), lambda i, j: (i, 0)),
            pl.BlockSpec((N_GATES, tm), lambda i, j: (0, i)),
        ],
        out_shape=[
            jax.ShapeDtypeStruct((t, PROJ_W), bf16),
            jax.ShapeDtypeStruct((t, GATE_W), f32),
            jax.ShapeDtypeStruct((N_GATES, t), f32),
        ],
        scratch_shapes=[pltpu.VMEM((tm, D_MODEL), bf16), pltpu.VMEM((1, GATE_W), f32)],
        compiler_params=pltpu.CompilerParams(
            dimension_semantics=("arbitrary", "arbitrary"), vmem_limit_bytes=VMEM_LIMIT),
        name="inproj",
    )(x2, nw, w_gate, b_gate, tri, _proj_col_scale(), *w_pieces)


def _mlstm_kernel(qkvo_ref, gcol_ref, grow_ref, hnw_ref, spread_ref, out_ref, s_sc, m_sc):
    c = pl.program_id(1)
    L = ML_CHUNK
    nb = qkvo_ref.shape[0]
    assert L == LANES

    @pl.when(c == 0)
    def _():
        s_sc[...] = jnp.zeros_like(s_sc)
        m_sc[...] = jnp.zeros_like(m_sc)

    row = lax.broadcasted_iota(jnp.int32, (L, L), 0)
    col = lax.broadcasted_iota(jnp.int32, (L, L), 1)
    causal = col <= row
    ones_blk = jnp.ones((L, LANES), bf16)

    def wide(tile, n):
        return jnp.concatenate([tile] * n, axis=1)

    P = [(n, h) for n in range(nb) for h in range(ML_HEADS)]
    C = range(len(P))
    qk_cols = lambda h, base: slice(base + h * ML_DQK, base + (h + 1) * ML_DQK)
    v_cols = lambda h, base=0: slice(base + h * ML_DV, base + (h + 1) * ML_DV)
    spread = []
    for n in range(nb):
        g = gcol_ref[n]
        g_hi = g.astype(bf16)
        r1 = g - g_hi.astype(f32)
        g_mid = r1.astype(bf16)
        g_lo = (r1 - g_mid.astype(f32)).astype(bf16)
        spread.append(_dot(jnp.concatenate([g_hi, g_mid, g_lo], axis=1), spread_ref[...]))
    tile = lambda n, j: spread[n][:, j * LANES:(j + 1) * LANES]
    b_t = [tile(n, h) for n, h in P]
    a_t = [tile(n, ML_HEADS + h) for n, h in P]
    gr = [grow_ref[n] for n in range(nb)]
    a_r = [gr[n][h:h + 1, :] - gr[n][ML_HEADS + h:ML_HEADS + h + 1, :] for n, h in P]
    q = [qkvo_ref[n, :, qk_cols(h, COL_QM)] for n, h in P]
    k = [qkvo_ref[n, :, qk_cols(h, COL_KM)] for n, h in P]
    v_aug = [jnp.concatenate([qkvo_ref[n, :, v_cols(h, COL_VM)], ones_blk], axis=1) for n, h in P]
    m_prev = [m_sc[i:i + 1, :] for i in C]
    state = [s_sc[i] for i in C]

    qk = [_dot_nt(q[i], k[i]) for i in C]
    amat = [jnp.where(causal, a_r[i], -jnp.inf) for i in C]
    mx = [jnp.maximum(m_prev[i], jnp.broadcast_to(jnp.max(amat[i], axis=-1, keepdims=True), (L, LANES)))
          for i in C]

    mx_last = [mx[i][L - 1:L, :] for i in C]
    b_last = [b_t[i][L - 1:L, :] for i in C]
    wk = [jnp.exp(a_t[i] - mx_last[i]) for i in C]
    upd = [_dot_tn((wk[i] * k[i].astype(f32)).astype(bf16), v_aug[i]) for i in C]

    s = [(qk[i] * jnp.exp(amat[i] - mx[i])).astype(bf16) for i in C]
    q_in = [(jnp.exp(m_prev[i] - mx[i]) * q[i].astype(f32)).astype(bf16) for i in C]
    num = [_dot(jnp.concatenate([s[i], q_in[i]], axis=1),
                jnp.concatenate([v_aug[i], state[i].astype(bf16)], axis=0)) for i in C]
    for i in C:
        s_sc[i] = wide(jnp.exp(m_prev[i] - mx_last[i]), 3) * state[i] + upd[i]
        m_sc[i:i + 1, :] = b_last[i] + mx_last[i]

    inv = [1.0 / jnp.maximum(jnp.abs(num[i][:, ML_DV:]), jnp.exp(-b_t[i] - mx[i])) for i in C]
    msq = [jnp.broadcast_to(jnp.mean(num[i][:, :ML_DV] * num[i][:, :ML_DV], axis=-1, keepdims=True),
                            (L, LANES)) for i in C]
    fac = [inv[i] * lax.rsqrt(inv[i] * inv[i] * msq[i] + EPS) for i in C]
    for i, (n, h) in enumerate(P):
        gate = jax.nn.sigmoid(qkvo_ref[n, :, v_cols(h, COL_OM)].astype(f32)) * hnw_ref[:, v_cols(h)]
        out_ref[n, :, v_cols(h)] = (num[i][:, :ML_DV] * wide(fac[i], ML_DV // LANES) * gate).astype(bf16)


def _mlstm(proj, gcol, grow, hnw, *, batch, seq, nb=4):
    while batch % nb:
        nb //= 2
    nc = seq // ML_CHUNK
    L = ML_CHUNK
    proj4 = proj.reshape(batch // nb, nb, seq, PROJ_W)
    gcol4 = gcol.reshape(batch // nb, nb, seq, GATE_W)
    grow4 = grow.reshape(N_GATES, batch // nb, nb, seq).transpose(1, 2, 0, 3)
    src = lax.broadcasted_iota(jnp.int32, (GATE_W, 2 * ML_HEADS * LANES), 0)
    dst = lax.broadcasted_iota(jnp.int32, (GATE_W, 2 * ML_HEADS * LANES), 1) // LANES
    spread = (jnp.where(src == ML_HEADS + dst % ML_HEADS, jnp.where(dst < ML_HEADS, 1.0, -1.0), 0.0)
              + jnp.where((dst >= ML_HEADS) & (src == dst - ML_HEADS), 1.0, 0.0)).astype(bf16)
    spread = jnp.concatenate([spread] * 3, axis=0)
    out = pl.pallas_call(
        _mlstm_kernel,
        grid=(batch // nb, nc),
        in_specs=[
            pl.BlockSpec((None, nb, L, COL_QF), lambda b, c: (b, 0, c, 0)),
            pl.BlockSpec((None, nb, L, GATE_W), lambda b, c: (b, 0, c, 0)),
            pl.BlockSpec((None, nb, N_GATES, L), lambda b, c: (b, 0, 0, c)),
            pl.BlockSpec((1, ML_V), lambda b, c: (0, 0)),
            pl.BlockSpec((3 * GATE_W, 2 * ML_HEADS * LANES), lambda b, c: (0, 0)),
        ],
        out_specs=pl.BlockSpec((None, nb, L, ML_V), lambda b, c: (b, 0, c, 0)),
        out_shape=jax.ShapeDtypeStruct((batch // nb, nb, seq, ML_V), bf16),
        scratch_shapes=[pltpu.VMEM((nb * ML_HEADS, ML_DQK, ML_DV + LANES), f32),
                        pltpu.VMEM((nb * ML_HEADS, LANES), f32)],
        compiler_params=pltpu.CompilerParams(
            dimension_semantics=("arbitrary", "arbitrary"), vmem_limit_bytes=VMEM_LIMIT),
        name="mlstm",
    )(proj4, gcol4, grow4, hnw, spread)
    return out.reshape(batch * seq, ML_V)


def _fox_kernel(q_ref, k_ref, v_ref, gcol_ref, out_ref, kaug_sc, st_sc, mx_sc, m_sc, l_sc, acc_sc,
                *, blk, setup_rows, ahead):
    hg = pl.program_id(1)
    seq = k_ref.shape[0]
    nh = kaug_sc.shape[0]
    head_cols = [slice(i * FOX_DH, (i + 1) * FOX_DH) for i in range(nh)]

    r = lax.broadcasted_iota(jnp.int32, (GATE_W, nh * LANES), 0)
    c = lax.broadcasted_iota(jnp.int32, (GATE_W, nh * LANES), 1)
    src = 2 * ML_HEADS + hg * nh + c // LANES
    pick = jnp.concatenate([((r == src) & (c % LANES == j)).astype(bf16) for j in range(3)], axis=0)
    for n in range(seq // setup_rows):
        rows = pl.ds(n * setup_rows, setup_rows)
        a = gcol_ref[rows, :] * (-LOG2E)
        a_hi = a.astype(bf16)
        r1 = a - a_hi.astype(f32)
        a_mid = r1.astype(bf16)
        a_lo = (r1 - a_mid.astype(f32)).astype(bf16)
        aug = _dot(jnp.concatenate([a_hi, a_mid, a_lo], axis=1), pick).astype(bf16)
        for i in range(nh):
            kaug_sc[i, rows, :FOX_DH] = k_ref[rows, head_cols[i]]
            kaug_sc[i, rows, FOX_DH:] = aug[:, i * LANES:(i + 1) * LANES]

    ones3 = (lax.broadcasted_iota(jnp.int32, (blk, LANES), 1) < 3).astype(bf16)

    def query_block(qi, _):
        qrows = pl.ds(pl.multiple_of(qi * blk, blk), blk)
        q_aug = [jnp.concatenate([q_ref[qrows, head_cols[i]], ones3], axis=1) for i in range(nh)]

        m_sc[...] = jnp.full(m_sc.shape, -jnp.inf, f32)
        l_sc[...] = jnp.zeros(l_sc.shape, f32)
        acc_sc[...] = jnp.zeros(acc_sc.shape, f32)

        def logits_stage(i, j, masked):
            start = pl.multiple_of(j * blk, blk)
            st = _dot_nt(kaug_sc[i, pl.ds(start, blk), :], q_aug[i])
            if masked:
                row = lax.broadcasted_iota(jnp.int32, (blk, blk), 0)
                col = lax.broadcasted_iota(jnp.int32, (blk, blk), 1)
                st = jnp.where(row <= col, st, -jnp.inf)
            st_sc[i] = st
            mx_sc[i] = jnp.max(st, axis=0, keepdims=True)

        def softmax_stage(i, j):
            start = pl.multiple_of(j * blk, blk)
            m = m_sc[i]
            m_new = jnp.maximum(m, mx_sc[i])
            alpha = jnp.exp2(m - m_new)
            p = jnp.exp2(st_sc[i] - m_new)
            l_sc[i] = alpha * l_sc[i] + jnp.sum(p, axis=0, keepdims=True)
            acc_sc[i] = alpha * acc_sc[i] + _dot_tn(v_ref[pl.ds(start, blk), head_cols[i]], p.astype(bf16))
            m_sc[i] = m_new

        def round_robin(cur, nxt, masked):
            for i in range(nh):
                if i + ahead < nh:
                    logits_stage(i + ahead, cur, masked)
                else:
                    logits_stage(i + ahead - nh, nxt, False)
                softmax_stage(i, cur)

        for i in range(ahead):
            logits_stage(i, qi, True)
        round_robin(qi, 0, True)

        def body(k, _):
            round_robin(k - 1, jnp.minimum(k, qi - 1), False)
            return 0

        lax.fori_loop(1, qi + 1, body, 0)

        for i in range(nh):
            out_ref[qrows, head_cols[i]] = (acc_sc[i] * (1.0 / l_sc[i])).T.astype(bf16)
        return 0

    lax.fori_loop(0, seq // blk, query_block, 0)


def _fox(proj, gcol, *, batch, seq, blk=512, nh=4, ahead=1):
    assert 1 <= ahead < nh
    assert seq % blk == 0 and FOX_HEADS % nh == 0
    t = proj.shape[0]
    w = nh * FOX_DH
    kern = functools.partial(_fox_kernel, blk=blk, setup_rows=blk, ahead=ahead)
    return pl.pallas_call(
        kern,
        grid=(batch, FOX_HEADS // nh),
        in_specs=[
            pl.BlockSpec((seq, w), lambda b, h: (b, COL_QF // w + h)),
            pl.BlockSpec((seq, w), lambda b, h: (b, COL_KF // w + h)),
            pl.BlockSpec((seq, w), lambda b, h: (b, COL_VF // w + h)),
            pl.BlockSpec((seq, GATE_W), lambda b, h: (b, 0)),
        ],
        out_specs=pl.BlockSpec((seq, w), lambda b, h: (b, h)),
        out_shape=jax.ShapeDtypeStruct((t, FOX_W), bf16),
        scratch_shapes=[pltpu.VMEM((nh, seq, FOX_DH + LANES), bf16),
                        pltpu.VMEM((nh, blk, blk), f32),
                        pltpu.VMEM((nh, 1, blk), f32),
                        pltpu.VMEM((nh, 1, blk), f32),
                        pltpu.VMEM((nh, 1, blk), f32),
                        pltpu.VMEM((nh, FOX_DH, blk), f32)],
        compiler_params=pltpu.CompilerParams(
            dimension_semantics=("arbitrary", "arbitrary"), vmem_limit_bytes=VMEM_LIMIT),
        name="fox",
    )(proj, proj, proj, gcol)


def _merge_kernel(x_ref, ha_ref, hb_ref, ga_ref, gb_ref, wa_ref, wb_ref, wo_ref,
                  bga_ref, bgb_ref, npost_ref, npre_ref, x1_ref, h2_ref, *, sub):
    n_sub = x_ref.shape[0] // sub
    rows = [slice(r * sub, (r + 1) * sub) for r in range(n_sub)]

    def branches(r):
        return _dot(ha_ref[rows[r], :], wa_ref[...]), _dot(hb_ref[rows[r], :], wb_ref[...])

    def finish(r, z):
        x1 = x_ref[rows[r], :] + _rms(z) * npost_ref[...]
        x1_ref[rows[r], :] = x1
        h2_ref[rows[r], :] = (_rms(x1) * npre_ref[...]).astype(bf16)

    y_next = branches(0)
    z_prev = None
    for r in range(n_sub):
        ya, yb = y_next
        if r + 1 < n_sub:
            y_next = branches(r + 1)
        merged = (jax.nn.sigmoid(ga_ref[rows[r], :].astype(f32) + bga_ref[...]) * ya
                  + jax.nn.sigmoid(gb_ref[rows[r], :].astype(f32) + bgb_ref[...]) * yb)
        z = _dot(merged.astype(bf16), wo_ref[...])
        if z_prev is not None:
            finish(r - 1, z_prev)
        z_prev = z
    finish(n_sub - 1, z_prev)


def _merge(x2, h_a, h_b, proj, wa, wb, wo, bga, bgb, npost, npre, *, tm=1024, sub=512):
    t = x2.shape[0]
    tm = min(tm, t)
    tok = lambda i: (i, 0)
    const = lambda i: (0, 0)
    wspec = pl.BlockSpec((D_MODEL, D_MODEL), const, pipeline_mode=pl.Buffered(1))
    vspec = pl.BlockSpec((1, D_MODEL), const)
    return pl.pallas_call(
        functools.partial(_merge_kernel, sub=sub),
        grid=(t // tm,),
        in_specs=[
            pl.BlockSpec((tm, D_MODEL), tok),
            pl.BlockSpec((tm, ML_V), tok),
            pl.BlockSpec((tm, FOX_W), tok),
            pl.BlockSpec((tm, D_MODEL), lambda i: (i, COL_GA // D_MODEL)),
            pl.BlockSpec((tm, D_MODEL), lambda i: (i, COL_GB // D_MODEL)),
            wspec, wspec, wspec, vspec, vspec, vspec, vspec,
        ],
        out_specs=[pl.BlockSpec((tm, D_MODEL), tok), pl.BlockSpec((tm, D_MODEL), tok)],
        out_shape=[jax.ShapeDtypeStruct((t, D_MODEL), f32), jax.ShapeDtypeStruct((t, D_MODEL), bf16)],
        compiler_params=pltpu.CompilerParams(
            dimension_semantics=("arbitrary",), vmem_limit_bytes=VMEM_LIMIT),
        name="merge",
    )(x2, h_a, h_b, proj, proj, wa, wb, wo, bga, bgb, npost, npre)


def _ffn_kernel(x1_ref, h2_ref, wup_ref, cw_ref, cb_ref, wdn_ref, npost_ref, out_ref,
                halo_sc, ubuf_sc, act_sc, acc_sc, ys_sc, *, blocks_per_seq, tf, down_group):
    i = pl.program_id(0)
    tm = x1_ref.shape[0]
    hm = tm // 2
    n_lt = tf // LANES

    @pl.when(i % blocks_per_seq == 0)
    def _():
        halo_sc[...] = jnp.zeros_like(halo_sc)

    h2 = h2_ref[...]
    n_chunks = D_FF // tf

    def up(c):
        for half in range(2):
            col0 = half * D_FF + c * tf
            slot = 2 * (c % 2) + half
            u = _dot(h2, wup_ref[:, col0:col0 + tf])
            for l in range(n_lt):
                lanes = slice(l * LANES, (l + 1) * LANES)
                ubuf_sc[slot, l, 0:SUBLANES, :] = halo_sc[:, col0 + l * LANES:col0 + (l + 1) * LANES]
                ubuf_sc[slot, l, SUBLANES:SUBLANES + tm, :] = u[:, lanes]
            halo_sc[:, col0:col0 + tf] = u[tm - SUBLANES:tm, :]

    def conv(c, half):
        col0 = half * D_FF + c * tf
        slot = 2 * (c % 2) + half
        blocks = []
        for parity in range(2):
            tiles = []
            for l in range(n_lt):
                cols = slice(col0 + l * LANES, col0 + (l + 1) * LANES)
                w = cw_ref[:, cols]
                y = cb_ref[:, cols]
                for j in range(CONV_W):
                    first = SUBLANES + parity - (CONV_W - 1 - j)
                    y = y + w[j:j + 1] * ubuf_sc[slot, l, pl.ds(first, hm, stride=2), :]
                tiles.append(y)
            blocks.append(jnp.concatenate(tiles, axis=1))
        return jnp.concatenate(blocks, axis=0)

    def gate(g, a):
        inner = g * (GELU_K1 + (GELU_K1 * GELU_CUBIC) * (g * g))
        return (g * a) * (1.0 / (1.0 + jnp.exp2(inner)))

    def down(c0, c1):
        return _dot(act_sc[:, c0 * tf:c1 * tf], wdn_ref[c0 * tf:c1 * tf, :])

    up(0)
    pending, done = None, 0
    for c in range(n_chunks):
        if c + 1 < n_chunks:
            up(c + 1)
        if pending is not None:
            if pending[0] == 0:
                acc_sc[...] = down(*pending)
            else:
                acc_sc[...] += down(*pending)
            pending = None
        act_sc[:, c * tf:(c + 1) * tf] = gate(conv(c, 1), conv(c, 0)).astype(bf16)
        if (c + 1) % down_group == 0 or c + 1 == n_chunks:
            pending, done = (done, c + 1), c + 1

    y = down(*pending)
    if pending[0] != 0:
        y = y + acc_sc[...]
    z = _rms(y) * npost_ref[...]
    for l in range(D_MODEL // LANES):
        lanes = slice(l * LANES, (l + 1) * LANES)
        for parity in range(2):
            ys_sc[l, pl.ds(parity, hm, stride=2), :] = z[parity * hm:(parity + 1) * hm, lanes]
        out_ref[:, lanes] = x1_ref[:, lanes] + ys_sc[l]


def _ffn(x1, h2, wup, cw, cb, wdn, npost, *, seq, tm=512, tf=256, down_group=2):
    t = x1.shape[0]
    tok = lambda i: (i, 0)
    const = lambda i: (0, 0)
    kern = functools.partial(_ffn_kernel, blocks_per_seq=seq // tm, tf=tf, down_group=down_group)
    return pl.pallas_call(
        kern,
        grid=(t // tm,),
        in_specs=[
            pl.BlockSpec((tm, D_MODEL), tok),
            pl.BlockSpec((tm, D_MODEL), tok),
            pl.BlockSpec((D_MODEL, 2 * D_FF), const, pipeline_mode=pl.Buffered(1)),
            pl.BlockSpec((CONV_W, 2 * D_FF), const),
            pl.BlockSpec((1, 2 * D_FF), const),
            pl.BlockSpec((D_FF, D_MODEL), const, pipeline_mode=pl.Buffered(1)),
            pl.BlockSpec((1, D_MODEL), const),
        ],
        out_specs=pl.BlockSpec((tm, D_MODEL), tok),
        out_shape=jax.ShapeDtypeStruct((t, D_MODEL), f32),
        scratch_shapes=[pltpu.VMEM((SUBLANES, 2 * D_FF), f32),
                        pltpu.VMEM((4, tf // LANES, tm + SUBLANES, LANES), f32),
                        pltpu.VMEM((tm, D_FF), bf16),
                        pltpu.VMEM((tm, D_MODEL), f32),
                        pltpu.VMEM((D_MODEL // LANES, tm, LANES), f32)],
        compiler_params=pltpu.CompilerParams(
            dimension_semantics=("arbitrary",), vmem_limit_bytes=VMEM_LIMIT),
        name="ffn",
    )(x1, h2, wup, cw, cb, wdn, npost)


def _layer(x2, p, l, *, batch, seq):
    w_in = p["w_in"][l]
    o_i = 2 * ML_QK + ML_V
    o_o = o_i + 2 * ML_HEADS
    o_qf = o_o + ML_V
    o_ff = o_qf + 3 * FOX_W
    o_ga = o_ff + FOX_HEADS
    w_pieces = [
        w_in[:, :o_i].astype(bf16),
        w_in[:, o_o:o_ff].astype(bf16),
        w_in[:, o_ga:].astype(bf16),
    ]
    w_gate = jnp.concatenate([
        w_in[:, o_i:o_o], w_in[:, o_ff:o_ga],
        jnp.zeros((D_MODEL, GATE_W - N_GATES), f32)], axis=1).astype(bf16)
    b_gate = jnp.concatenate([
        p["b_ml_i"][l], p["b_ml_f"][l], p["b_fox_f"][l], jnp.zeros((GATE_W - N_GATES,), f32)])[None, :]
    tri = jnp.tril(jnp.ones((ML_CHUNK, ML_CHUNK), f32)).astype(bf16)
    row = lambda v: v[None, :].astype(f32)

    proj, gcol, grow = _inproj(x2, row(p["norm_mix_pre"][l]), w_pieces, w_gate, b_gate, tri, seq=seq)
    h_a = _mlstm(proj, gcol, grow, row(p["ml_head_norm"][l]), batch=batch, seq=seq)
    h_b = _fox(proj, gcol, batch=batch, seq=seq)
    x1, h2 = _merge(x2, h_a, h_b, proj,
                    p["w_branch_a"][l].astype(bf16), p["w_branch_b"][l].astype(bf16),
                    p["w_out"][l].astype(bf16), row(p["b_gate_a"][l]), row(p["b_gate_b"][l]),
                    row(p["norm_mix_post"][l]), row(p["norm_ffn_pre"][l]))
    return _ffn(x1, h2, p["w_up"][l].astype(bf16), p["conv_w"][l], row(p["conv_b"][l]),
                p["w_down"][l].astype(bf16), row(p["norm_ffn_post"][l]), seq=seq)


def kernel(x, norm_mix_pre, w_in, b_ml_i, b_ml_f, ml_head_norm, b_fox_f, b_gate_a, b_gate_b,
           w_branch_a, w_branch_b, w_out, norm_mix_post, norm_ffn_pre, w_up, conv_w, conv_b,
           w_down, norm_ffn_post):
    batch, seq, _ = x.shape
    p = dict(norm_mix_pre=norm_mix_pre, w_in=w_in, b_ml_i=b_ml_i, b_ml_f=b_ml_f,
             ml_head_norm=ml_head_norm, b_fox_f=b_fox_f, b_gate_a=b_gate_a, b_gate_b=b_gate_b,
             w_branch_a=w_branch_a, w_branch_b=w_branch_b, w_out=w_out,
             norm_mix_post=norm_mix_post, norm_ffn_pre=norm_ffn_pre, w_up=w_up, conv_w=conv_w,
             conv_b=conv_b, w_down=w_down, norm_ffn_post=norm_ffn_post)
    x2 = x.reshape(batch * seq, D_MODEL)
    for l in range(w_in.shape[0]):
        x2 = _layer(x2, p, l, batch=batch, seq=seq)
    return x2.reshape(batch, seq, D_MODEL)
```

```python
import functools

import jax
import jax.numpy as jnp
from jax import lax
from jax.experimental import pallas as pl
from jax.experimental.pallas import tpu as pltpu

D_MODEL = 1024
ML_HEADS = 4
ML_DQK = 128
ML_DV = 256
ML_QK = ML_HEADS * ML_DQK
ML_V = ML_HEADS * ML_DV
FOX_HEADS = 8
FOX_DH = 128
FOX_W = FOX_HEADS * FOX_DH
D_FF = 2816
CONV_W = 3
GATE_CAP = 15.0
EPS = 1e-6
LOG2E = 1.4426950408889634
GELU_CUBIC = 0.044715
GELU_K1 = -2.0 * (2.0 / 3.141592653589793) ** 0.5 * LOG2E

LANES = 128
SUBLANES = 8
GATE_W = LANES
N_GATES = 2 * ML_HEADS + FOX_HEADS
ML_CHUNK = 128
VMEM_LIMIT = 56 * 1024 * 1024

PROJ_W = 2 * ML_QK + 2 * ML_V + 3 * FOX_W + 2 * D_MODEL
COL_QM, COL_KM, COL_VM, COL_OM = 0, ML_QK, 2 * ML_QK, 2 * ML_QK + ML_V
COL_QF = COL_OM + ML_V
COL_KF = COL_QF + FOX_W
COL_VF = COL_KF + FOX_W
COL_GA = COL_VF + FOX_W
COL_GB = COL_GA + D_MODEL

f32 = jnp.float32
bf16 = jnp.bfloat16


def _log_sigmoid(z):
    return jnp.minimum(z, 0.0) - jnp.log1p(jnp.exp(-jnp.abs(z)))


def _rms(v):
    return v * lax.rsqrt(jnp.mean(v * v, axis=-1, keepdims=True) + EPS)


def _dot(a, b):
    return jnp.dot(a, b, preferred_element_type=f32)


def _dot_nt(a, b):
    return lax.dot_general(a, b, (((1,), (1,)), ((), ())), preferred_element_type=f32)


def _dot_tn(a, b):
    return lax.dot_general(a, b, (((0,), (0,)), ((), ())), preferred_element_type=f32)


def _inproj_kernel(*refs, blocks_per_seq, piece, w_blocks):
    x_ref, nw_ref, wg_ref, bg_ref, tri_ref, cs_ref = refs[:6]
    w_refs = refs[6:6 + len(w_blocks)]
    proj_ref, gcol_ref, grow_ref, h_sc, carry_sc = refs[6 + len(w_blocks):]
    assert w_blocks[0][0] == 0
    i = pl.program_id(0)
    j = pl.program_id(1)
    tm = x_ref.shape[0]

    def project(hb, w_ref=w_refs[0]):
        return (_dot(hb, w_ref[...]) * cs_ref[...]).astype(bf16)

    @pl.when(j == 0)
    def _():
        @pl.when(i % blocks_per_seq == 0)
        def _():
            carry_sc[...] = jnp.zeros_like(carry_sc)

        tri = tri_ref[...]
        lane = lax.broadcasted_iota(jnp.int32, (ML_CHUNK, GATE_W), 1)
        per_piece = piece // ML_CHUNK

        def normed(p):
            prow = slice(p * piece, (p + 1) * piece)
            hb = (_rms(x_ref[prow, :]) * nw_ref[...]).astype(bf16)
            h_sc[prow, :] = hb
            return hb

        acts = []
        hb = normed(0)
        for p in range(tm // piece):
            proj_ref[p * piece:(p + 1) * piece, :] = project(hb)
            g_piece = _dot(hb, wg_ref[...]) + bg_ref[...]
            if (p + 1) * piece < tm:
                hb = normed(p + 1)
            for q in range(per_piece):
                g = g_piece[q * ML_CHUNK:(q + 1) * ML_CHUNK]
                cap = GATE_CAP * jnp.tanh(g / GATE_CAP)
                a = jnp.where(lane < ML_HEADS, cap,
                              jnp.where(lane < 2 * ML_HEADS, _log_sigmoid(cap), _log_sigmoid(g)))
                acts.append(jnp.where(lane < N_GATES, a, 0.0))

        for r in range(tm // ML_CHUNK):
            rows = slice(r * ML_CHUNK, (r + 1) * ML_CHUNK)
            a = acts[r]
            a_hi = a.astype(bf16)
            r1 = a - a_hi.astype(f32)
            a_mid = r1.astype(bf16)
            a_lo = (r1 - a_mid.astype(f32)).astype(bf16)
            cs3 = _dot(tri, jnp.concatenate([a_hi, a_mid, a_lo], axis=1))
            cs = cs3[:, :GATE_W] + cs3[:, GATE_W:2 * GATE_W] + cs3[:, 2 * GATE_W:]
            glob = cs + carry_sc[...]
            carry_sc[...] = glob[ML_CHUNK - 1:ML_CHUNK, :]
            out = jnp.where(lane < ML_HEADS, a, jnp.where(lane < 2 * ML_HEADS, cs, glob))
            gcol_ref[rows, :] = out
            grow_ref[:, rows] = out.T[:N_GATES, :]

    for w_ref, (first, count) in zip(w_refs, w_blocks):
        @pl.when((j >= max(first, 1)) & (j < first + count))
        def _(w_ref=w_ref):
            proj_ref[...] = project(h_sc[...], w_ref)


def _proj_col_scale():
    cs = jnp.ones((1, PROJ_W), f32)
    cs = cs.at[:, COL_QM:COL_QM + ML_QK].set(ML_DQK ** -0.5)
    return cs.at[:, COL_QF:COL_QF + FOX_W].set(LOG2E * FOX_DH ** -0.5)


def _inproj(x2, nw, w_pieces, w_gate, b_gate, tri, *, seq, tm=2048, tn=1024, piece=512):
    t = x2.shape[0]
    tm = min(tm, seq)
    assert seq % tm == 0 and tm % piece == 0
    piece = max(piece, ML_CHUNK)
    w_blocks, first = [], 0
    for w in w_pieces:
        w_blocks.append((first, w.shape[1] // tn))
        first += w.shape[1] // tn
    assert first == PROJ_W // tn
    kern = functools.partial(_inproj_kernel, blocks_per_seq=seq // tm, piece=piece, w_blocks=tuple(w_blocks))

    def w_spec(first, count):
        return pl.BlockSpec((D_MODEL, tn), lambda i, j: (
            0, jnp.where(j < first - 1, count - 1, jnp.clip(j - first, 0, count - 1))))

    return pl.pallas_call(
        kern,
        grid=(t // tm, PROJ_W // tn),
        in_specs=[
            pl.BlockSpec((tm, D_MODEL), lambda i, j: (i, 0)),
            pl.BlockSpec((1, D_MODEL), lambda i, j: (0, 0)),
            pl.BlockSpec((D_MODEL, GATE_W), lambda i, j: (0, 0)),
            pl.BlockSpec((1, GATE_W), lambda i, j: (0, 0)),
            pl.BlockSpec((ML_CHUNK, ML_CHUNK), lambda i, j: (0, 0)),
            pl.BlockSpec((1, tn), lambda i, j: (0, j)),
        ] + [w_spec(f, n) for f, n in w_blocks],
        out_specs=[
            pl.BlockSpec((tm, tn), lambda i, j: (i, j)),
            pl.BlockSpec((tm, GATE_W), lambda i, j: (i, 0)),
            pl.BlockSpec((N_GATES, tm), lambda i, j: (0, i)),
        ],
        out_shape=[
            jax.ShapeDtypeStruct((t, PROJ_W), bf16),
            jax.ShapeDtypeStruct((t, GATE_W), f32),
            jax.ShapeDtypeStruct((N_GATES, t), f32),
        ],
        scratch_shapes=[pltpu.VMEM((tm, D_MODEL), bf16), pltpu.VMEM((1, GATE_W), f32)],
        compiler_params=pltpu.CompilerParams(
            dimension_semantics=("arbitrary", "arbitrary"), vmem_limit_bytes=VMEM_LIMIT),
        name="inproj",
    )(x2, nw, w_gate, b_gate, tri, _proj_col_scale(), *w_pieces)


def _mlstm_kernel(qkvo_ref, gcol_ref, grow_ref, hnw_ref, spread_ref, out_ref, s_sc, m_sc):
    c = pl.program_id(1)
    L = ML_CHUNK
    nb = qkvo_ref.shape[0]
    assert L == LANES

    @pl.when(c == 0)
    def _():
        s_sc[...] = jnp.zeros_like(s_sc)
        m_sc[...] = jnp.zeros_like(m_sc)

    row = lax.broadcasted_iota(jnp.int32, (L, L), 0)
    col = lax.broadcasted_iota(jnp.int32, (L, L), 1)
    causal = col <= row
    ones_blk = jnp.ones((L, LANES), bf16)

    def wide(tile, n):
        return jnp.concatenate([tile] * n, axis=1)

    P = [(n, h) for n in range(nb) for h in range(ML_HEADS)]
    C = range(len(P))
    qk_cols = lambda h, base: slice(base + h * ML_DQK, base + (h + 1) * ML_DQK)
    v_cols = lambda h, base=0: slice(base + h * ML_DV, base + (h + 1) * ML_DV)
    spread = []
    for n in range(nb):
        g = gcol_ref[n]
        g_hi = g.astype(bf16)
        r1 = g - g_hi.astype(f32)
        g_mid = r1.astype(bf16)
        g_lo = (r1 - g_mid.astype(f32)).astype(bf16)
        spread.append(_dot(jnp.concatenate([g_hi, g_mid, g_lo], axis=1), spread_ref[...]))
    tile = lambda n, j: spread[n][:, j * LANES:(j + 1) * LANES]
    b_t = [tile(n, h) for n, h in P]
    a_t = [tile(n, ML_HEADS + h) for n, h in P]
    gr = [grow_ref[n] for n in range(nb)]
    a_r = [gr[n][h:h + 1, :] - gr[n][ML_HEADS + h:ML_HEADS + h + 1, :] for n, h in P]
    q = [qkvo_ref[n, :, qk_cols(h, COL_QM)] for n, h in P]
    k = [qkvo_ref[n, :, qk_cols(h, COL_KM)] for n, h in P]
    v_aug = [jnp.concatenate([qkvo_ref[n, :, v_cols(h, COL_VM)], ones_blk], axis=1) for n, h in P]
    m_prev = [m_sc[i:i + 1, :] for i in C]
    state = [s_sc[i] for i in C]

    qk = [_dot_nt(q[i], k[i]) for i in C]
    amat = [jnp.where(causal, a_r[i], -jnp.inf) for i in C]
    mx = [jnp.maximum(m_prev[i], jnp.broadcast_to(jnp.max(amat[i], axis=-1, keepdims=True), (L, LANES)))
          for i in C]

    mx_last = [mx[i][L - 1:L, :] for i in C]
    b_last = [b_t[i][L - 1:L, :] for i in C]
    wk = [jnp.exp(a_t[i] - mx_last[i]) for i in C]
    upd = [_dot_tn((wk[i] * k[i].astype(f32)).astype(bf16), v_aug[i]) for i in C]

    s = [(qk[i] * jnp.exp(amat[i] - mx[i])).astype(bf16) for i in C]
    q_in = [(jnp.exp(m_prev[i] - mx[i]) * q[i].astype(f32)).astype(bf16) for i in C]
    num = [_dot(jnp.concatenate([s[i], q_in[i]], axis=1),
                jnp.concatenate([v_aug[i], state[i].astype(bf16)], axis=0)) for i in C]
    for i in C:
        s_sc[i] = wide(jnp.exp(m_prev[i] - mx_last[i]), 3) * state[i] + upd[i]
        m_sc[i:i + 1, :] = b_last[i] + mx_last[i]

    inv = [1.0 / jnp.maximum(jnp.abs(num[i][:, ML_DV:]), jnp.exp(-b_t[i] - mx[i])) for i in C]
    msq = [jnp.broadcast_to(jnp.mean(num[i][:, :ML_DV] * num[i][:, :ML_DV], axis=-1, keepdims=True),
                            (L, LANES)) for i in C]
    fac = [inv[i] * lax.rsqrt(inv[i] * inv[i] * msq[i] + EPS) for i in C]
    for i, (n, h) in enumerate(P):
        gate = jax.nn.sigmoid(qkvo_ref[n, :, v_cols(h, COL_OM)].astype(f32)) * hnw_ref[:, v_cols(h)]
        out_ref[n, :, v_cols(h)] = (num[i][:, :ML_DV] * wide(fac[i], ML_DV // LANES) * gate).astype(bf16)


def _mlstm(proj, gcol, grow, hnw, *, batch, seq, nb=4):
    while batch % nb:
        nb //= 2
    nc = seq // ML_CHUNK
    L = ML_CHUNK
    proj4 = proj.reshape(batch // nb, nb, seq, PROJ_W)
    gcol4 = gcol.reshape(batch // nb, nb, seq, GATE_W)
    grow4 = grow.reshape(N_GATES, batch // nb, nb, seq).transpose(1, 2, 0, 3)
    src = lax.broadcasted_iota(jnp.int32, (GATE_W, 2 * ML_HEADS * LANES), 0)
    dst = lax.broadcasted_iota(jnp.int32, (GATE_W, 2 * ML_HEADS * LANES), 1) // LANES
    spread = (jnp.where(src == ML_HEADS + dst % ML_HEADS, jnp.where(dst < ML_HEADS, 1.0, -1.0), 0.0)
              + jnp.where((dst >= ML_HEADS) & (src == dst - ML_HEADS), 1.0, 0.0)).astype(bf16)
    spread = jnp.concatenate([spread] * 3, axis=0)
    out = pl.pallas_call(
        _mlstm_kernel,
        grid=(batch // nb, nc),
        in_specs=[
            pl.BlockSpec((None, nb, L, COL_QF), lambda b, c: (b, 0, c, 0)),
            pl.BlockSpec((None, nb, L, GATE_W), lambda b, c: (b, 0, c, 0)),
            pl.BlockSpec((None, nb, N_GATES, L), lambda b, c: (b, 0, 0, c)),
            pl.BlockSpec((1, ML_V), lambda b, c: (0, 0)),
            pl.BlockSpec((3 * GATE_W, 2 * ML_HEADS * LANES), lambda b, c: (0, 0)),
        ],
        out_specs=pl.BlockSpec((None, nb, L, ML_V), lambda b, c: (b, 0, c, 0)),
        out_shape=jax.ShapeDtypeStruct((batch // nb, nb, seq, ML_V), bf16),
        scratch_shapes=[pltpu.VMEM((nb * ML_HEADS, ML_DQK, ML_DV + LANES), f32),
                        pltpu.VMEM((nb * ML_HEADS, LANES), f32)],
        compiler_params=pltpu.CompilerParams(
            dimension_semantics=("arbitrary", "arbitrary"), vmem_limit_bytes=VMEM_LIMIT),
        name="mlstm",
    )(proj4, gcol4, grow4, hnw, spread)
    return out.reshape(batch * seq, ML_V)


def _fox_kernel(q_ref, k_ref, v_ref, gcol_ref, out_ref, kaug_sc, st_sc, mx_sc, m_sc, l_sc, acc_sc,
                *, blk, setup_rows, ahead):
    hg = pl.program_id(1)
    seq = k_ref.shape[0]
    nh = kaug_sc.shape[0]
    head_cols = [slice(i * FOX_DH, (i + 1) * FOX_DH) for i in range(nh)]

    r = lax.broadcasted_iota(jnp.int32, (GATE_W, nh * LANES), 0)
    c = lax.broadcasted_iota(jnp.int32, (GATE_W, nh * LANES), 1)
    src = 2 * ML_HEADS + hg * nh + c // LANES
    pick = jnp.concatenate([((r == src) & (c % LANES == j)).astype(bf16) for j in range(3)], axis=0)
    for n in range(seq // setup_rows):
        rows = pl.ds(n * setup_rows, setup_rows)
        a = gcol_ref[rows, :] * (-LOG2E)
        a_hi = a.astype(bf16)
        r1 = a - a_hi.astype(f32)
        a_mid = r1.astype(bf16)
        a_lo = (r1 - a_mid.astype(f32)).astype(bf16)
        aug = _dot(jnp.concatenate([a_hi, a_mid, a_lo], axis=1), pick).astype(bf16)
        for i in range(nh):
            kaug_sc[i, rows, :FOX_DH] = k_ref[rows, head_cols[i]]
            kaug_sc[i, rows, FOX_DH:] = aug[:, i * LANES:(i + 1) * LANES]

    ones3 = (lax.broadcasted_iota(jnp.int32, (blk, LANES), 1) < 3).astype(bf16)

    def query_block(qi, _):
        qrows = pl.ds(pl.multiple_of(qi * blk, blk), blk)
        q_aug = [jnp.concatenate([q_ref[qrows, head_cols[i]], ones3], axis=1) for i in range(nh)]

        m_sc[...] = jnp.full(m_sc.shape, -jnp.inf, f32)
        l_sc[...] = jnp.zeros(l_sc.shape, f32)
        acc_sc[...] = jnp.zeros(acc_sc.shape, f32)

        def logits_stage(i, j, masked):
            start = pl.multiple_of(j * blk, blk)
            st = _dot_nt(kaug_sc[i, pl.ds(start, blk), :], q_aug[i])
            if masked:
                row = lax.broadcasted_iota(jnp.int32, (blk, blk), 0)
                col = lax.broadcasted_iota(jnp.int32, (blk, blk), 1)
                st = jnp.where(row <= col, st, -jnp.inf)
            st_sc[i] = st
            mx_sc[i] = jnp.max(st, axis=0, keepdims=True)

        def softmax_stage(i, j):
            start = pl.multiple_of(j * blk, blk)
            m = m_sc[i]
            m_new = jnp.maximum(m, mx_sc[i])
            alpha = jnp.exp2(m - m_new)
            p = jnp.exp2(st_sc[i] - m_new)
            l_sc[i] = alpha * l_sc[i] + jnp.sum(p, axis=0, keepdims=True)
            acc_sc[i] = alpha * acc_sc[i] + _dot_tn(v_ref[pl.ds(start, blk), head_cols[i]], p.astype(bf16))
            m_sc[i] = m_new

        def round_robin(cur, nxt, masked):
            for i in range(nh):
                if i + ahead < nh:
                    logits_stage(i + ahead, cur, masked)
                else:
                    logits_stage(i + ahead - nh, nxt, False)
                softmax_stage(i, cur)

        for i in range(ahead):
            logits_stage(i, qi, True)
        round_robin(qi, 0, True)

        def body(k, _):
            round_robin(k - 1, jnp.minimum(k, qi - 1), False)
            return 0

        lax.fori_loop(1, qi + 1, body, 0)

        for i in range(nh):
            out_ref[qrows, head_cols[i]] = (acc_sc[i] * (1.0 / l_sc[i])).T.astype(bf16)
        return 0

    lax.fori_loop(0, seq // blk, query_block, 0)


def _fox(proj, gcol, *, batch, seq, blk=512, nh=4, ahead=1):
    assert 1 <= ahead < nh
    assert seq % blk == 0 and FOX_HEADS % nh == 0
    t = proj.shape[0]
    w = nh * FOX_DH
    kern = functools.partial(_fox_kernel, blk=blk, setup_rows=blk, ahead=ahead)
    return pl.pallas_call(
        kern,
        grid=(batch, FOX_HEADS // nh),
        in_specs=[
            pl.BlockSpec((seq, w), lambda b, h: (b, COL_QF // w + h)),
            pl.BlockSpec((seq, w), lambda b, h: (b, COL_KF // w + h)),
            pl.BlockSpec((seq, w), lambda b, h: (b, COL_VF // w + h)),
            pl.BlockSpec((seq, GATE_W), lambda b, h: (b, 0)),
        ],
        out_specs=pl.BlockSpec((seq, w), lambda b, h: (b, h)),
        out_shape=jax.ShapeDtypeStruct((t, FOX_W), bf16),
        scratch_shapes=[pltpu.VMEM((nh, seq, FOX_DH + LANES), bf16),
                        pltpu.VMEM((nh, blk, blk), f32),
                        pltpu.VMEM((nh, 1, blk), f32),
                        pltpu.VMEM((nh, 1, blk), f32),
                        pltpu.VMEM((nh, 1, blk), f32),
                        pltpu.VMEM((nh, FOX_DH, blk), f32)],
        compiler_params=pltpu.CompilerParams(
            dimension_semantics=("arbitrary", "arbitrary"), vmem_limit_bytes=VMEM_LIMIT),
        name="fox",
    )(proj, proj, proj, gcol)


def _merge_kernel(x_ref, ha_ref, hb_ref, ga_ref, gb_ref, wa_ref, wb_ref, wo_ref,
                  bga_ref, bgb_ref, npost_ref, npre_ref, x1_ref, h2_ref, *, sub):
    n_sub = x_ref.shape[0] // sub
    rows = [slice(r * sub, (r + 1) * sub) for r in range(n_sub)]

    def branches(r):
        return _dot(ha_ref[rows[r], :], wa_ref[...]), _dot(hb_ref[rows[r], :], wb_ref[...])

    def finish(r, z):
        x1 = x_ref[rows[r], :] + _rms(z) * npost_ref[...]
        x1_ref[rows[r], :] = x1
        h2_ref[rows[r], :] = (_rms(x1) * npre_ref[...]).astype(bf16)

    y_next = branches(0)
    z_prev = None
    for r in range(n_sub):
        ya, yb = y_next
        if r + 1 < n_sub:
            y_next = branches(r + 1)
        merged = (jax.nn.sigmoid(ga_ref[rows[r], :].astype(f32) + bga_ref[...]) * ya
                  + jax.nn.sigmoid(gb_ref[rows[r], :].astype(f32) + bgb_ref[...]) * yb)
        z = _dot(merged.astype(bf16), wo_ref[...])
        if z_prev is not None:
            finish(r - 1, z_prev)
        z_prev = z
    finish(n_sub - 1, z_prev)


def _merge(x2, h_a, h_b, proj, wa, wb, wo, bga, bgb, npost, npre, *, tm=1024, sub=512):
    t = x2.shape[0]
    tm = min(tm, t)
    tok = lambda i: (i, 0)
    const = lambda i: (0, 0)
    wspec = pl.BlockSpec((D_MODEL, D_MODEL), const, pipeline_mode=pl.Buffered(1))
    vspec = pl.BlockSpec((1, D_MODEL), const)
    return pl.pallas_call(
        functools.partial(_merge_kernel, sub=sub),
        grid=(t // tm,),
        in_specs=[
            pl.BlockSpec((tm, D_MODEL), tok),
            pl.BlockSpec((tm, ML_V), tok),
            pl.BlockSpec((tm, FOX_W), tok),
            pl.BlockSpec((tm, D_MODEL), lambda i: (i, COL_GA // D_MODEL)),
            pl.BlockSpec((tm, D_MODEL), lambda i: (i, COL_GB // D_MODEL)),
            wspec, wspec, wspec, vspec, vspec, vspec, vspec,
        ],
        out_specs=[pl.BlockSpec((tm, D_MODEL), tok), pl.BlockSpec((tm, D_MODEL), tok)],
        out_shape=[jax.ShapeDtypeStruct((t, D_MODEL), f32), jax.ShapeDtypeStruct((t, D_MODEL), bf16)],
        compiler_params=pltpu.CompilerParams(
            dimension_semantics=("arbitrary",), vmem_limit_bytes=VMEM_LIMIT),
        name="merge",
    )(x2, h_a, h_b, proj, proj, wa, wb, wo, bga, bgb, npost, npre)


def _ffn_kernel(x1_ref, h2_ref, wup_ref, cw_ref, cb_ref, wdn_ref, npost_ref, out_ref,
                halo_sc, ubuf_sc, act_sc, acc_sc, ys_sc, *, blocks_per_seq, tf, down_group):
    i = pl.program_id(0)
    tm = x1_ref.shape[0]
    hm = tm // 2
    n_lt = tf // LANES

    @pl.when(i % blocks_per_seq == 0)
    def _():
        halo_sc[...] = jnp.zeros_like(halo_sc)

    h2 = h2_ref[...]
    n_chunks = D_FF // tf

    def up(c):
        for half in range(2):
            col0 = half * D_FF + c * tf
            slot = 2 * (c % 2) + half
            u = _dot(h2, wup_ref[:, col0:col0 + tf])
            for l in range(n_lt):
                lanes = slice(l * LANES, (l + 1) * LANES)
                ubuf_sc[slot, l, 0:SUBLANES, :] = halo_sc[:, col0 + l * LANES:col0 + (l + 1) * LANES]
                ubuf_sc[slot, l, SUBLANES:SUBLANES + tm, :] = u[:, lanes]
            halo_sc[:, col0:col0 + tf] = u[tm - SUBLANES:tm, :]

    def conv(c, half):
        col0 = half * D_FF + c * tf
        slot = 2 * (c % 2) + half
        blocks = []
        for parity in range(2):
            tiles = []
            for l in range(n_lt):
                cols = slice(col0 + l * LANES, col0 + (l + 1) * LANES)
                w = cw_ref[:, cols]
                y = cb_ref[:, cols]
                for j in range(CONV_W):
                    first = SUBLANES + parity - (CONV_W - 1 - j)
                    y = y + w[j:j + 1] * ubuf_sc[slot, l, pl.ds(first, hm, stride=2), :]
                tiles.append(y)
            blocks.append(jnp.concatenate(tiles, axis=1))
        return jnp.concatenate(blocks, axis=0)

    def gate(g, a):
        inner = g * (GELU_K1 + (GELU_K1 * GELU_CUBIC) * (g * g))
        return (g * a) * (1.0 / (1.0 + jnp.exp2(inner)))

    def down(c0, c1):
        return _dot(act_sc[:, c0 * tf:c1 * tf], wdn_ref[c0 * tf:c1 * tf, :])

    up(0)
    pending, done = None, 0
    for c in range(n_chunks):
        if c + 1 < n_chunks:
            up(c + 1)
        if pending is not None:
            if pending[0] == 0:
                acc_sc[...] = down(*pending)
            else:
                acc_sc[...] += down(*pending)
            pending = None
        act_sc[:, c * tf:(c + 1) * tf] = gate(conv(c, 1), conv(c, 0)).astype(bf16)
        if (c + 1) % down_group == 0 or c + 1 == n_chunks:
            pending, done = (done, c + 1), c + 1

    y = down(*pending)
    if pending[0] != 0:
        y = y + acc_sc[...]
    z = _rms(y) * npost_ref[...]
    for l in range(D_MODEL // LANES):
        lanes = slice(l * LANES, (l + 1) * LANES)
        for parity in range(2):
            ys_sc[l, pl.ds(parity, hm, stride=2), :] = z[parity * hm:(parity + 1) * hm, lanes]
        out_ref[:, lanes] = x1_ref[:, lanes] + ys_sc[l]


def _ffn(x1, h2, wup, cw, cb, wdn, npost, *, seq, tm=512, tf=256, down_group=1):
    t = x1.shape[0]
    tok = lambda i: (i, 0)
    const = lambda i: (0, 0)
    kern = functools.partial(_ffn_kernel, blocks_per_seq=seq // tm, tf=tf, down_group=down_group)
    return pl.pallas_call(
        kern,
        grid=(t // tm,),
        in_specs=[
            pl.BlockSpec((tm, D_MODEL), tok),
            pl.BlockSpec((tm, D_MODEL), tok),
            pl.BlockSpec((D_MODEL, 2 * D_FF), const, pipeline_mode=pl.Buffered(1)),
            pl.BlockSpec((CONV_W, 2 * D_FF), const),
            pl.BlockSpec((1, 2 * D_FF), const),
            pl.BlockSpec((D_FF, D_MODEL), const, pipeline_mode=pl.Buffered(1)),
            pl.BlockSpec((1, D_MODEL), const),
        ],
        out_specs=pl.BlockSpec((tm, D_MODEL), tok),
        out_shape=jax.ShapeDtypeStruct((t, D_MODEL), f32),
        scratch_shapes=[pltpu.VMEM((SUBLANES, 2 * D_FF), f32),
                        pltpu.VMEM((4, tf // LANES, tm + SUBLANES, LANES), f32),
                        pltpu.VMEM((tm, D_FF), bf16),
                        pltpu.VMEM((tm, D_MODEL), f32),
                        pltpu.VMEM((D_MODEL // LANES, tm, LANES), f32)],
        compiler_params=pltpu.CompilerParams(
            dimension_semantics=("arbitrary",), vmem_limit_bytes=VMEM_LIMIT),
        name="ffn",
    )(x1, h2, wup, cw, cb, wdn, npost)


def _layer(x2, p, l, *, batch, seq):
    w_in = p["w_in"][l]
    o_i = 2 * ML_QK + ML_V
    o_o = o_i + 2 * ML_HEADS
    o_qf = o_o + ML_V
    o_ff = o_qf + 3 * FOX_W
    o_ga = o_ff + FOX_HEADS
    w_pieces = [
        w_in[:, :o_i].astype(bf16),
        w_in[:, o_o:o_ff].astype(bf16),
        w_in[:, o_ga:].astype(bf16),
    ]
    w_gate = jnp.concatenate([
        w_in[:, o_i:o_o], w_in[:, o_ff:o_ga],
        jnp.zeros((D_MODEL, GATE_W - N_GATES), f32)], axis=1).astype(bf16)
    b_gate = jnp.concatenate([
        p["b_ml_i"][l], p["b_ml_f"][l], p["b_fox_f"][l], jnp.zeros((GATE_W - N_GATES,), f32)])[None, :]
    tri = jnp.tril(jnp.ones((ML_CHUNK, ML_CHUNK), f32)).astype(bf16)
    row = lambda v: v[None, :].astype(f32)

    proj, gcol, grow = _inproj(x2, row(p["norm_mix_pre"][l]), w_pieces, w_gate, b_gate, tri, seq=seq)
    h_a = _mlstm(proj, gcol, grow, row(p["ml_head_norm"][l]), batch=batch, seq=seq)
    h_b = _fox(proj, gcol, batch=batch, seq=seq)
    x1, h2 = _merge(x2, h_a, h_b, proj,
                    p["w_branch_a"][l].astype(bf16), p["w_branch_b"][l].astype(bf16),
                    p["w_out"][l].astype(bf16), row(p["b_gate_a"][l]), row(p["b_gate_b"][l]),
                    row(p["norm_mix_post"][l]), row(p["norm_ffn_pre"][l]))
    return _ffn(x1, h2, p["w_up"][l].astype(bf16), p["conv_w"][l], row(p["conv_b"][l]),
                p["w_down"][l].astype(bf16), row(p["norm_ffn_post"][l]), seq=seq)


def kernel(x, norm_mix_pre, w_in, b_ml_i, b_ml_f, ml_head_norm, b_fox_f, b_gate_a, b_gate_b,
           w_branch_a, w_branch_b, w_out, norm_mix_post, norm_ffn_pre, w_up, conv_w, conv_b,
           w_down, norm_ffn_post):
    batch, seq, _ = x.shape
    p = dict(norm_mix_pre=norm_mix_pre, w_in=w_in, b_ml_i=b_ml_i, b_ml_f=b_ml_f,
             ml_head_norm=ml_head_norm, b_fox_f=b_fox_f, b_gate_a=b_gate_a, b_gate_b=b_gate_b,
             w_branch_a=w_branch_a, w_branch_b=w_branch_b, w_out=w_out,
             norm_mix_post=norm_mix_post, norm_ffn_pre=norm_ffn_pre, w_up=w_up, conv_w=conv_w,
             conv_b=conv_b, w_down=w_down, norm_ffn_post=norm_ffn_post)
    x2 = x.reshape(batch * seq, D_MODEL)
    for l in range(w_in.shape[0]):
        x2 = _layer(x2, p, l, batch=batch, seq=seq)
    return x2.reshape(batch, seq, D_MODEL)
```

```python
import functools

import jax
import jax.numpy as jnp
from jax import lax
from jax.experimental import pallas as pl
from jax.experimental.pallas import tpu as pltpu

D_MODEL = 1024
ML_HEADS = 4
ML_DQK = 128
ML_DV = 256
ML_QK = ML_HEADS * ML_DQK
ML_V = ML_HEADS * ML_DV
FOX_HEADS = 8
FOX_DH = 128
FOX_W = FOX_HEADS * FOX_DH
D_FF = 2816
CONV_W = 3
GATE_CAP = 15.0
EPS = 1e-6
LOG2E = 1.4426950408889634
GELU_CUBIC = 0.044715
GELU_K1 = -2.0 * (2.0 / 3.141592653589793) ** 0.5 * LOG2E

LANES = 128
SUBLANES = 8
GATE_W = LANES
N_GATES = 2 * ML_HEADS + FOX_HEADS
ML_CHUNK = 128
VMEM_LIMIT = 56 * 1024 * 1024

PROJ_W = 2 * ML_QK + 2 * ML_V + 3 * FOX_W + 2 * D_MODEL
COL_QM, COL_KM, COL_VM, COL_OM = 0, ML_QK, 2 * ML_QK, 2 * ML_QK + ML_V
COL_QF = COL_OM + ML_V
COL_KF = COL_QF + FOX_W
COL_VF = COL_KF + FOX_W
COL_GA = COL_VF + FOX_W
COL_GB = COL_GA + D_MODEL

f32 = jnp.float32
bf16 = jnp.bfloat16


def _log_sigmoid(z):
    return jnp.minimum(z, 0.0) - jnp.log1p(jnp.exp(-jnp.abs(z)))


def _rms(v):
    return v * lax.rsqrt(jnp.mean(v * v, axis=-1, keepdims=True) + EPS)


def _dot(a, b):
    return jnp.dot(a, b, preferred_element_type=f32)


def _dot_nt(a, b):
    return lax.dot_general(a, b, (((1,), (1,)), ((), ())), preferred_element_type=f32)


def _dot_tn(a, b):
    return lax.dot_general(a, b, (((0,), (0,)), ((), ())), preferred_element_type=f32)


def _inproj_kernel(*refs, blocks_per_seq, piece, w_blocks):
    x_ref, nw_ref, wg_ref, bg_ref, tri_ref, cs_ref = refs[:6]
    w_refs = refs[6:6 + len(w_blocks)]
    proj_ref, gcol_ref, grow_ref, h_sc, carry_sc = refs[6 + len(w_blocks):]
    assert w_blocks[0][0] == 0
    i = pl.program_id(0)
    j = pl.program_id(1)
    tm = x_ref.shape[0]

    def project(hb, w_ref=w_refs[0]):
        return (_dot(hb, w_ref[...]) * cs_ref[...]).astype(bf16)

    @pl.when(j == 0)
    def _():
        @pl.when(i % blocks_per_seq == 0)
        def _():
            carry_sc[...] = jnp.zeros_like(carry_sc)

        tri = tri_ref[...]
        lane = lax.broadcasted_iota(jnp.int32, (ML_CHUNK, GATE_W), 1)
        per_piece = piece // ML_CHUNK

        def normed(p):
            prow = slice(p * piece, (p + 1) * piece)
            hb = (_rms(x_ref[prow, :]) * nw_ref[...]).astype(bf16)
            h_sc[prow, :] = hb
            return hb

        acts = []
        hb = normed(0)
        for p in range(tm // piece):
            proj_ref[p * piece:(p + 1) * piece, :] = project(hb)
            g_piece = _dot(hb, wg_ref[...]) + bg_ref[...]
            if (p + 1) * piece < tm:
                hb = normed(p + 1)
            for q in range(per_piece):
                g = g_piece[q * ML_CHUNK:(q + 1) * ML_CHUNK]
                cap = GATE_CAP * jnp.tanh(g / GATE_CAP)
                a = jnp.where(lane < ML_HEADS, cap,
                              jnp.where(lane < 2 * ML_HEADS, _log_sigmoid(cap), _log_sigmoid(g)))
                acts.append(jnp.where(lane < N_GATES, a, 0.0))

        for r in range(tm // ML_CHUNK):
            rows = slice(r * ML_CHUNK, (r + 1) * ML_CHUNK)
            a = acts[r]
            a_hi = a.astype(bf16)
            r1 = a - a_hi.astype(f32)
            a_mid = r1.astype(bf16)
            a_lo = (r1 - a_mid.astype(f32)).astype(bf16)
            cs3 = _dot(tri, jnp.concatenate([a_hi, a_mid, a_lo], axis=1))
            cs = cs3[:, :GATE_W] + cs3[:, GATE_W:2 * GATE_W] + cs3[:, 2 * GATE_W:]
            glob = cs + carry_sc[...]
            carry_sc[...] = glob[ML_CHUNK - 1:ML_CHUNK, :]
            out = jnp.where(lane < ML_HEADS, a, jnp.where(lane < 2 * ML_HEADS, cs, glob))
            gcol_ref[rows, :] = out
            grow_ref[:, rows] = out.T[:N_GATES, :]

    for w_ref, (first, count) in zip(w_refs, w_blocks):
        @pl.when((j >= max(first, 1)) & (j < first + count))
        def _(w_ref=w_ref):
            proj_ref[...] = project(h_sc[...], w_ref)


def _proj_col_scale():
    cs = jnp.ones((1, PROJ_W), f32)
    cs = cs.at[:, COL_QM:COL_QM + ML_QK].set(ML_DQK ** -0.5)
    return cs.at[:, COL_QF:COL_QF + FOX_W].set(LOG2E * FOX_DH ** -0.5)


def _inproj(x2, nw, w_pieces, w_gate, b_gate, tri, *, seq, tm=2048, tn=1024, piece=512):
    t = x2.shape[0]
    tm = min(tm, seq)
    assert seq % tm == 0 and tm % piece == 0
    piece = max(piece, ML_CHUNK)
    w_blocks, first = [], 0
    for w in w_pieces:
        w_blocks.append((first, w.shape[1] // tn))
        first += w.shape[1] // tn
    assert first == PROJ_W // tn
    kern = functools.partial(_inproj_kernel, blocks_per_seq=seq // tm, piece=piece, w_blocks=tuple(w_blocks))

    def w_spec(first, count):
        return pl.BlockSpec((D_MODEL, tn), lambda i, j: (
            0, jnp.where(j < first - 1, count - 1, jnp.clip(j - first, 0, count - 1))))

    return pl.pallas_call(
        kern,
        grid=(t // tm, PROJ_W // tn),
        in_specs=[
            pl.BlockSpec((tm, D_MODEL), lambda i, j: (i, 0)),
            pl.BlockSpec((1, D_MODEL), lambda i, j: (0, 0)),
            pl.BlockSpec((D_MODEL, GATE_W), lambda i, j: (0, 0)),
            pl.BlockSpec((1, GATE_W), lambda i, j: (0, 0)),
            pl.BlockSpec((ML_CHUNK, ML_CHUNK), lambda i, j: (0, 0)),
            pl.BlockSpec((1, tn), lambda i, j: (0, j)),
        ] + [w_spec(f, n) for f, n in w_blocks],
        out_specs=[
            pl.BlockSpec((tm, tn), lambda i, j: (i, j)),
            pl.BlockSpec((tm, GATE_W), lambda i, j: (i, 0)),
            pl.BlockSpec((N_GATES, tm), lambda i, j: (0, i)),
        ],
        out_shape=[
            jax.ShapeDtypeStruct((t, PROJ_W), bf16),
            jax.ShapeDtypeStruct((t, GATE_W), f32),
            jax.ShapeDtypeStruct((N_GATES, t), f32),
        ],
        scratch_shapes=[pltpu.VMEM((tm, D_MODEL), bf16), pltpu.VMEM((1, GATE_W), f32)],
        compiler_params=pltpu.CompilerParams(
            dimension_semantics=("arbitrary", "arbitrary"), vmem_limit_bytes=VMEM_LIMIT),
        name="inproj",
    )(x2, nw, w_gate, b_gate, tri, _proj_col_scale(), *w_pieces)


def _mlstm_kernel(qkvo_ref, gcol_ref, grow_ref, hnw_ref, spread_ref, out_ref, s_sc, m_sc):
    c = pl.program_id(1)
    L = ML_CHUNK
    nb = qkvo_ref.shape[0]
    assert L == LANES

    @pl.when(c == 0)
    def _():
        s_sc[...] = jnp.zeros_like(s_sc)
        m_sc[...] = jnp.zeros_like(m_sc)

    row = lax.broadcasted_iota(jnp.int32, (L, L), 0)
    col = lax.broadcasted_iota(jnp.int32, (L, L), 1)
    causal = col <= row
    ones_blk = jnp.ones((L, LANES), bf16)

    def wide(tile, n):
        return jnp.concatenate([tile] * n, axis=1)

    P = [(n, h) for n in range(nb) for h in range(ML_HEADS)]
    C = range(len(P))
    qk_cols = lambda h, base: slice(base + h * ML_DQK, base + (h + 1) * ML_DQK)
    v_cols = lambda h, base=0: slice(base + h * ML_DV, base + (h + 1) * ML_DV)
    spread = []
    for n in range(nb):
        g = gcol_ref[n]
        g_hi = g.astype(bf16)
        r1 = g - g_hi.astype(f32)
        g_mid = r1.astype(bf16)
        g_lo = (r1 - g_mid.astype(f32)).astype(bf16)
        spread.append(_dot(jnp.concatenate([g_hi, g_mid, g_lo], axis=1), spread_ref[...]))
    tile = lambda n, j: spread[n][:, j * LANES:(j + 1) * LANES]
    b_t = [tile(n, h) for n, h in P]
    a_t = [tile(n, ML_HEADS + h) for n, h in P]
    gr = [grow_ref[n] for n in range(nb)]
    a_r = [gr[n][h:h + 1, :] - gr[n][ML_HEADS + h:ML_HEADS + h + 1, :] for n, h in P]
    q = [qkvo_ref[n, :, qk_cols(h, COL_QM)] for n, h in P]
    k = [qkvo_ref[n, :, qk_cols(h, COL_KM)] for n, h in P]
    v_aug = [jnp.concatenate([qkvo_ref[n, :, v_cols(h, COL_VM)], ones_blk], axis=1) for n, h in P]
    m_prev = [m_sc[i:i + 1, :] for i in C]
    state = [s_sc[i] for i in C]

    qk = [_dot_nt(q[i], k[i]) for i in C]
    amat = [jnp.where(causal, a_r[i], -jnp.inf) for i in C]
    mx = [jnp.maximum(m_prev[i], jnp.broadcast_to(jnp.max(amat[i], axis=-1, keepdims=True), (L, LANES)))
          for i in C]

    mx_last = [mx[i][L - 1:L, :] for i in C]
    b_last = [b_t[i][L - 1:L, :] for i in C]
    wk = [jnp.exp(a_t[i] - mx_last[i]) for i in C]
    upd = [_dot_tn((wk[i] * k[i].astype(f32)).astype(bf16), v_aug[i]) for i in C]

    s = [(qk[i] * jnp.exp(amat[i] - mx[i])).astype(bf16) for i in C]
    q_in = [(jnp.exp(m_prev[i] - mx[i]) * q[i].astype(f32)).astype(bf16) for i in C]
    num = [_dot(jnp.concatenate([s[i], q_in[i]], axis=1),
                jnp.concatenate([v_aug[i], state[i].astype(bf16)], axis=0)) for i in C]
    for i in C:
        s_sc[i] = wide(jnp.exp(m_prev[i] - mx_last[i]), 3) * state[i] + upd[i]
        m_sc[i:i + 1, :] = b_last[i] + mx_last[i]

    inv = [1.0 / jnp.maximum(jnp.abs(num[i][:, ML_DV:]), jnp.exp(-b_t[i] - mx[i])) for i in C]
    msq = [jnp.broadcast_to(jnp.mean(num[i][:, :ML_DV] * num[i][:, :ML_DV], axis=-1, keepdims=True),
                            (L, LANES)) for i in C]
    fac = [inv[i] * lax.rsqrt(inv[i] * inv[i] * msq[i] + EPS) for i in C]
    for i, (n, h) in enumerate(P):
        gate = jax.nn.sigmoid(qkvo_ref[n, :, v_cols(h, COL_OM)].astype(f32)) * hnw_ref[:, v_cols(h)]
        out_ref[n, :, v_cols(h)] = (num[i][:, :ML_DV] * wide(fac[i], ML_DV // LANES) * gate).astype(bf16)


def _mlstm(proj, gcol, grow, hnw, *, batch, seq, nb=4):
    while batch % nb:
        nb //= 2
    nc = seq // ML_CHUNK
    L = ML_CHUNK
    proj4 = proj.reshape(batch // nb, nb, seq, PROJ_W)
    gcol4 = gcol.reshape(batch // nb, nb, seq, GATE_W)
    grow4 = grow.reshape(N_GATES, batch // nb, nb, seq).transpose(1, 2, 0, 3)
    src = lax.broadcasted_iota(jnp.int32, (GATE_W, 2 * ML_HEADS * LANES), 0)
    dst = lax.broadcasted_iota(jnp.int32, (GATE_W, 2 * ML_HEADS * LANES), 1) // LANES
    spread = (jnp.where(src == ML_HEADS + dst % ML_HEADS, jnp.where(dst < ML_HEADS, 1.0, -1.0), 0.0)
              + jnp.where((dst >= ML_HEADS) & (src == dst - ML_HEADS), 1.0, 0.0)).astype(bf16)
    spread = jnp.concatenate([spread] * 3, axis=0)
    out = pl.pallas_call(
        _mlstm_kernel,
        grid=(batch // nb, nc),
        in_specs=[
            pl.BlockSpec((None, nb, L, COL_QF), lambda b, c: (b, 0, c, 0)),
            pl.BlockSpec((None, nb, L, GATE_W), lambda b, c: (b, 0, c, 0)),
            pl.BlockSpec((None, nb, N_GATES, L), lambda b, c: (b, 0, 0, c)),
            pl.BlockSpec((1, ML_V), lambda b, c: (0, 0)),
            pl.BlockSpec((3 * GATE_W, 2 * ML_HEADS * LANES), lambda b, c: (0, 0)),
        ],
        out_specs=pl.BlockSpec((None, nb, L, ML_V), lambda b, c: (b, 0, c, 0)),
        out_shape=jax.ShapeDtypeStruct((batch // nb, nb, seq, ML_V), bf16),
        scratch_shapes=[pltpu.VMEM((nb * ML_HEADS, ML_DQK, ML_DV + LANES), f32),
                        pltpu.VMEM((nb * ML_HEADS, LANES), f32)],
        compiler_params=pltpu.CompilerParams(
            dimension_semantics=("arbitrary", "arbitrary"), vmem_limit_bytes=VMEM_LIMIT),
        name="mlstm",
    )(proj4, gcol4, grow4, hnw, spread)
    return out.reshape(batch * seq, ML_V)


def _fox_kernel(q_ref, k_ref, v_ref, gcol_ref, out_ref, kaug_sc, st_sc, mx_sc, m_sc, l_sc, acc_sc,
                *, blk, setup_rows, ahead):
    hg = pl.program_id(1)
    seq = k_ref.shape[0]
    nh = kaug_sc.shape[0]
    head_cols = [slice(i * FOX_DH, (i + 1) * FOX_DH) for i in range(nh)]

    r = lax.broadcasted_iota(jnp.int32, (GATE_W, nh * LANES), 0)
    c = lax.broadcasted_iota(jnp.int32, (GATE_W, nh * LANES), 1)
    src = 2 * ML_HEADS + hg * nh + c // LANES
    pick = jnp.concatenate([((r == src) & (c % LANES == j)).astype(bf16) for j in range(3)], axis=0)
    for n in range(seq // setup_rows):
        rows = pl.ds(n * setup_rows, setup_rows)
        a = gcol_ref[rows, :] * (-LOG2E)
        a_hi = a.astype(bf16)
        r1 = a - a_hi.astype(f32)
        a_mid = r1.astype(bf16)
        a_lo = (r1 - a_mid.astype(f32)).astype(bf16)
        aug = _dot(jnp.concatenate([a_hi, a_mid, a_lo], axis=1), pick).astype(bf16)
        for i in range(nh):
            kaug_sc[i, rows, :FOX_DH] = k_ref[rows, head_cols[i]]
            kaug_sc[i, rows, FOX_DH:] = aug[:, i * LANES:(i + 1) * LANES]

    ones3 = (lax.broadcasted_iota(jnp.int32, (blk, LANES), 1) < 3).astype(bf16)

    def query_block(qi, _):
        qrows = pl.ds(pl.multiple_of(qi * blk, blk), blk)
        q_aug = [jnp.concatenate([q_ref[qrows, head_cols[i]], ones3], axis=1) for i in range(nh)]

        m_sc[...] = jnp.full(m_sc.shape, -jnp.inf, f32)
        l_sc[...] = jnp.zeros(l_sc.shape, f32)
        acc_sc[...] = jnp.zeros(acc_sc.shape, f32)

        def logits_stage(i, j, masked):
            start = pl.multiple_of(j * blk, blk)
            st = _dot_nt(kaug_sc[i, pl.ds(start, blk), :], q_aug[i])
            if masked:
                row = lax.broadcasted_iota(jnp.int32, (blk, blk), 0)
                col = lax.broadcasted_iota(jnp.int32, (blk, blk), 1)
                st = jnp.where(row <= col, st, -jnp.inf)
            st_sc[i] = st
            mx_sc[i] = jnp.max(st, axis=0, keepdims=True)

        def softmax_stage(i, j):
            start = pl.multiple_of(j * blk, blk)
            m = m_sc[i]
            m_new = jnp.maximum(m, mx_sc[i])
            alpha = jnp.exp2(m - m_new)
            p = jnp.exp2(st_sc[i] - m_new)
            l_sc[i] = alpha * l_sc[i] + jnp.sum(p, axis=0, keepdims=True)
            acc_sc[i] = alpha * acc_sc[i] + _dot_tn(v_ref[pl.ds(start, blk), head_cols[i]], p.astype(bf16))
            m_sc[i] = m_new

        def round_robin(cur, nxt, masked):
            for i in range(nh):
                if i + ahead < nh:
                    logits_stage(i + ahead, cur, masked)
                else:
                    logits_stage(i + ahead - nh, nxt, False)
                softmax_stage(i, cur)

        for i in range(ahead):
            logits_stage(i, qi, True)
        round_robin(qi, 0, True)

        def body(k, _):
            round_robin(k - 1, jnp.minimum(k, qi - 1), False)
            return 0

        lax.fori_loop(1, qi + 1, body, 0)

        for i in range(nh):
            out_ref[qrows, head_cols[i]] = (acc_sc[i] * (1.0 / l_sc[i])).T.astype(bf16)
        return 0

    lax.fori_loop(0, seq // blk, query_block, 0)


def _fox(proj, gcol, *, batch, seq, blk=512, nh=4, ahead=1):
    assert 1 <= ahead < nh
    assert seq % blk == 0 and FOX_HEADS % nh == 0
    t = proj.shape[0]
    w = nh * FOX_DH
    kern = functools.partial(_fox_kernel, blk=blk, setup_rows=blk, ahead=ahead)
    return pl.pallas_call(
        kern,
        grid=(batch, FOX_HEADS // nh),
        in_specs=[
            pl.BlockSpec((seq, w), lambda b, h: (b, COL_QF // w + h)),
            pl.BlockSpec((seq, w), lambda b, h: (b, COL_KF // w + h)),
            pl.BlockSpec((seq, w), lambda b, h: (b, COL_VF // w + h)),
            pl.BlockSpec((seq, GATE_W), lambda b, h: (b, 0)),
        ],
        out_specs=pl.BlockSpec((seq, w), lambda b, h: (b, h)),
        out_shape=jax.ShapeDtypeStruct((t, FOX_W), bf16),
        scratch_shapes=[pltpu.VMEM((nh, seq, FOX_DH + LANES), bf16),
                        pltpu.VMEM((nh, blk, blk), f32),
                        pltpu.VMEM((nh, 1, blk), f32),
                        pltpu.VMEM((nh, 1, blk), f32),
                        pltpu.VMEM((nh, 1, blk), f32),
                        pltpu.VMEM((nh, FOX_DH, blk), f32)],
        compiler_params=pltpu.CompilerParams(
            dimension_semantics=("arbitrary", "arbitrary"), vmem_limit_bytes=VMEM_LIMIT),
        name="fox",
    )(proj, proj, proj, gcol)


def _merge_kernel(x_ref, ha_ref, hb_ref, ga_ref, gb_ref, wa_ref, wb_ref, wo_ref,
                  bga_ref, bgb_ref, npost_ref, npre_ref, x1_ref, h2_ref, *, sub):
    n_sub = x_ref.shape[0] // sub
    rows = [slice(r * sub, (r + 1) * sub) for r in range(n_sub)]

    def branches(r):
        return _dot(ha_ref[rows[r], :], wa_ref[...]), _dot(hb_ref[rows[r], :], wb_ref[...])

    def finish(r, z):
        x1 = x_ref[rows[r], :] + _rms(z) * npost_ref[...]
        x1_ref[rows[r], :] = x1
        h2_ref[rows[r], :] = (_rms(x1) * npre_ref[...]).astype(bf16)

    y_next = branches(0)
    z_prev = None
    for r in range(n_sub):
        ya, yb = y_next
        if r + 1 < n_sub:
            y_next = branches(r + 1)
        merged = (jax.nn.sigmoid(ga_ref[rows[r], :].astype(f32) + bga_ref[...]) * ya
                  + jax.nn.sigmoid(gb_ref[rows[r], :].astype(f32) + bgb_ref[...]) * yb)
        z = _dot(merged.astype(bf16), wo_ref[...])
        if z_prev is not None:
            finish(r - 1, z_prev)
        z_prev = z
    finish(n_sub - 1, z_prev)


def _merge(x2, h_a, h_b, proj, wa, wb, wo, bga, bgb, npost, npre, *, tm=1024, sub=512):
    t = x2.shape[0]
    tm = min(tm, t)
    tok = lambda i: (i, 0)
    const = lambda i: (0, 0)
    wspec = pl.BlockSpec((D_MODEL, D_MODEL), const, pipeline_mode=pl.Buffered(1))
    vspec = pl.BlockSpec((1, D_MODEL), const)
    return pl.pallas_call(
        functools.partial(_merge_kernel, sub=sub),
        grid=(t // tm,),
        in_specs=[
            pl.BlockSpec((tm, D_MODEL), tok),
            pl.BlockSpec((tm, ML_V), tok),
            pl.BlockSpec((tm, FOX_W), tok),
            pl.BlockSpec((tm, D_MODEL), lambda i: (i, COL_GA // D_MODEL)),
            pl.BlockSpec((tm, D_MODEL), lambda i: (i, COL_GB // D_MODEL)),
            wspec, wspec, wspec, vspec, vspec, vspec, vspec,
        ],
        out_specs=[pl.BlockSpec((tm, D_MODEL), tok), pl.BlockSpec((tm, D_MODEL), tok)],
        out_shape=[jax.ShapeDtypeStruct((t, D_MODEL), f32), jax.ShapeDtypeStruct((t, D_MODEL), bf16)],
        compiler_params=pltpu.CompilerParams(
            dimension_semantics=("arbitrary",), vmem_limit_bytes=VMEM_LIMIT),
        name="merge",
    )(x2, h_a, h_b, proj, proj, wa, wb, wo, bga, bgb, npost, npre)


UP_RING = 3


def _ffn_kernel(x1_ref, h2_ref, wup_ref, cw_ref, cb_ref, wdn_ref, npost_ref, out_ref,
                halo_sc, ubuf_sc, act_sc, acc_sc, ys_sc, *, blocks_per_seq, tf, down_group):
    i = pl.program_id(0)
    tm = x1_ref.shape[0]
    hm = tm // 2
    n_lt = tf // LANES

    @pl.when(i % blocks_per_seq == 0)
    def _():
        halo_sc[...] = jnp.zeros_like(halo_sc)

    h2 = h2_ref[...]
    n_chunks = D_FF // tf

    def up(c):
        for half in range(2):
            col0 = half * D_FF + c * tf
            slot = 2 * (c % UP_RING) + half
            u = _dot(h2, wup_ref[:, col0:col0 + tf])
            for l in range(n_lt):
                lanes = slice(l * LANES, (l + 1) * LANES)
                ubuf_sc[slot, l, 0:SUBLANES, :] = halo_sc[:, col0 + l * LANES:col0 + (l + 1) * LANES]
                ubuf_sc[slot, l, SUBLANES:SUBLANES + tm, :] = u[:, lanes]
            halo_sc[:, col0:col0 + tf] = u[tm - SUBLANES:tm, :]

    def conv(c, half):
        col0 = half * D_FF + c * tf
        slot = 2 * (c % UP_RING) + half
        blocks = []
        for parity in range(2):
            tiles = []
            for l in range(n_lt):
                cols = slice(col0 + l * LANES, col0 + (l + 1) * LANES)
                w = cw_ref[:, cols]
                y = cb_ref[:, cols]
                for j in range(CONV_W):
                    first = SUBLANES + parity - (CONV_W - 1 - j)
                    y = y + w[j:j + 1] * ubuf_sc[slot, l, pl.ds(first, hm, stride=2), :]
                tiles.append(y)
            blocks.append(jnp.concatenate(tiles, axis=1))
        return jnp.concatenate(blocks, axis=0)

    def gate(g, a):
        inner = g * (GELU_K1 + (GELU_K1 * GELU_CUBIC) * (g * g))
        return (g * a) * (1.0 / (1.0 + jnp.exp2(inner)))

    def down(c0, c1):
        return _dot(act_sc[:, c0 * tf:c1 * tf], wdn_ref[c0 * tf:c1 * tf, :])

    ahead = UP_RING - 1
    for c in range(min(ahead, n_chunks)):
        up(c)
    pending, done = None, 0
    for c in range(n_chunks):
        if c + ahead < n_chunks:
            up(c + ahead)
        if pending is not None:
            if pending[0] == 0:
                acc_sc[...] = down(*pending)
            else:
                acc_sc[...] += down(*pending)
            pending = None
        act_sc[:, c * tf:(c + 1) * tf] = gate(conv(c, 1), conv(c, 0)).astype(bf16)
        if (c + 1) % down_group == 0 or c + 1 == n_chunks:
            pending, done = (done, c + 1), c + 1

    y = down(*pending)
    if pending[0] != 0:
        y = y + acc_sc[...]
    z = _rms(y) * npost_ref[...]
    for l in range(D_MODEL // LANES):
        lanes = slice(l * LANES, (l + 1) * LANES)
        for parity in range(2):
            ys_sc[l, pl.ds(parity, hm, stride=2), :] = z[parity * hm:(parity + 1) * hm, lanes]
        out_ref[:, lanes] = x1_ref[:, lanes] + ys_sc[l]


def _ffn(x1, h2, wup, cw, cb, wdn, npost, *, seq, tm=512, tf=256, down_group=1):
    t = x1.shape[0]
    tok = lambda i: (i, 0)
    const = lambda i: (0, 0)
    kern = functools.partial(_ffn_kernel, blocks_per_seq=seq // tm, tf=tf, down_group=down_group)
    return pl.pallas_call(
        kern,
        grid=(t // tm,),
        in_specs=[
            pl.BlockSpec((tm, D_MODEL), tok),
            pl.BlockSpec((tm, D_MODEL), tok),
            pl.BlockSpec((D_MODEL, 2 * D_FF), const, pipeline_mode=pl.Buffered(1)),
            pl.BlockSpec((CONV_W, 2 * D_FF), const),
            pl.BlockSpec((1, 2 * D_FF), const),
            pl.BlockSpec((D_FF, D_MODEL), const, pipeline_mode=pl.Buffered(1)),
            pl.BlockSpec((1, D_MODEL), const),
        ],
        out_specs=pl.BlockSpec((tm, D_MODEL), tok),
        out_shape=jax.ShapeDtypeStruct((t, D_MODEL), f32),
        scratch_shapes=[pltpu.VMEM((SUBLANES, 2 * D_FF), f32),
                        pltpu.VMEM((2 * UP_RING, tf // LANES, tm + SUBLANES, LANES), f32),
                        pltpu.VMEM((tm, D_FF), bf16),
                        pltpu.VMEM((tm, D_MODEL), f32),
                        pltpu.VMEM((D_MODEL // LANES, tm, LANES), f32)],
        compiler_params=pltpu.CompilerParams(
            dimension_semantics=("arbitrary",), vmem_limit_bytes=VMEM_LIMIT),
        name="ffn",
    )(x1, h2, wup, cw, cb, wdn, npost)


def _layer(x2, p, l, *, batch, seq):
    w_in = p["w_in"][l]
    o_i = 2 * ML_QK + ML_V
    o_o = o_i + 2 * ML_HEADS
    o_qf = o_o + ML_V
    o_ff = o_qf + 3 * FOX_W
    o_ga = o_ff + FOX_HEADS
    w_pieces = [
        w_in[:, :o_i].astype(bf16),
        w_in[:, o_o:o_ff].astype(bf16),
        w_in[:, o_ga:].astype(bf16),
    ]
    w_gate = jnp.concatenate([
        w_in[:, o_i:o_o], w_in[:, o_ff:o_ga],
        jnp.zeros((D_MODEL, GATE_W - N_GATES), f32)], axis=1).astype(bf16)
    b_gate = jnp.concatenate([
        p["b_ml_i"][l], p["b_ml_f"][l], p["b_fox_f"][l], jnp.zeros((GATE_W - N_GATES,), f32)])[None, :]
    tri = jnp.tril(jnp.ones((ML_CHUNK, ML_CHUNK), f32)).astype(bf16)
    row = lambda v: v[None, :].astype(f32)

    proj, gcol, grow = _inproj(x2, row(p["norm_mix_pre"][l]), w_pieces, w_gate, b_gate, tri, seq=seq)
    h_a = _mlstm(proj, gcol, grow, row(p["ml_head_norm"][l]), batch=batch, seq=seq)
    h_b = _fox(proj, gcol, batch=batch, seq=seq)
    x1, h2 = _merge(x2, h_a, h_b, proj,
                    p["w_branch_a"][l].astype(bf16), p["w_branch_b"][l].astype(bf16),
                    p["w_out"][l].astype(bf16), row(p["b_gate_a"][l]), row(p["b_gate_b"][l]),
                    row(p["norm_mix_post"][l]), row(p["norm_ffn_pre"][l]))
    return _ffn(x1, h2, p["w_up"][l].astype(bf16), p["conv_w"][l], row(p["conv_b"][l]),
                p["w_down"][l].astype(bf16), row(p["norm_ffn_post"][l]), seq=seq)


def kernel(x, norm_mix_pre, w_in, b_ml_i, b_ml_f, ml_head_norm, b_fox_f, b_gate_a, b_gate_b,
           w_branch_a, w_branch_b, w_out, norm_mix_post, norm_ffn_pre, w_up, conv_w, conv_b,
           w_down, norm_ffn_post):
    batch, seq, _ = x.shape
    p = dict(norm_mix_pre=norm_mix_pre, w_in=w_in, b_ml_i=b_ml_i, b_ml_f=b_ml_f,
             ml_head_norm=ml_head_norm, b_fox_f=b_fox_f, b_gate_a=b_gate_a, b_gate_b=b_gate_b,
             w_branch_a=w_branch_a, w_branch_b=w_branch_b, w_out=w_out,
             norm_mix_post=norm_mix_post, norm_ffn_pre=norm_ffn_pre, w_up=w_up, conv_w=conv_w,
             conv_b=conv_b, w_down=w_down, norm_ffn_post=norm_ffn_post)
    x2 = x.reshape(batch * seq, D_MODEL)
    for l in range(w_in.shape[0]):
        x2 = _layer(x2, p, l, batch=batch, seq=seq)
    return x2.reshape(batch, seq, D_MODEL)
```
